```python
import math
import jax, jax.numpy as jnp
from jax import lax
import numpy as np

D_MODEL = 1024
BATCH = 4
SEQ = 8192
DEPTH = 1

POOL_WIDTH = 512
POOL_WINDOWS = (2, 4, 8, 16)
POOL_GROUP = POOL_WIDTH // len(POOL_WINDOWS)
N_DIFF_HEADS = 4
DIFF_HEAD_DIM = 64
ATTN_WIDTH = N_DIFF_HEADS * 2 * DIFF_HEAD_DIM
QBLK = 128
N_BRANCHES = 2
IN_WIDTH = POOL_WIDTH + 3 * ATTN_WIDTH + N_BRANCHES * D_MODEL
REL_BUCKETS = 32
REL_MAX_DIST = 128
N_EXPERTS = 256
TOP_K = 8
N_EXPERT_GROUPS = 8
TOPK_GROUPS = 4
EXPERT_DIM = 256
SHARED_DIM = 256
ROUTED_SCALE = 2.5
MOE_BLK = 128
ALPHA = (2 * DEPTH) ** 0.25
BETA = (8 * DEPTH) ** -0.25
LN_EPS = 1e-5

kernel_name = "hybrid_pool_diffattn_moe_deepnorm_encoder"


def lambda_init_fn(layer_idx):
    return 0.8 - 0.6 * math.exp(-0.3 * layer_idx)


def layer_norm(x, g, b):
    xf = x.astype(jnp.float32)
    mu = jnp.mean(xf, axis=-1, keepdims=True)
    var = jnp.mean(jnp.square(xf - mu), axis=-1, keepdims=True)
    y = (xf - mu) * lax.rsqrt(var + LN_EPS) * g.astype(jnp.float32) + b.astype(jnp.float32)
    return y.astype(x.dtype)


def rms_norm(x, g):
    xf = x.astype(jnp.float32)
    y = xf * lax.rsqrt(jnp.mean(jnp.square(xf), axis=-1, keepdims=True) + LN_EPS) * g.astype(jnp.float32)
    return y.astype(x.dtype)


def t5_bucket(rel):
    half = REL_BUCKETS // 2
    max_exact = half // 2
    ret = jnp.where(rel > 0, half, 0)
    n = jnp.abs(rel)
    nf = jnp.maximum(n, 1).astype(jnp.float32)
    large = max_exact + (jnp.log(nf / max_exact) / math.log(REL_MAX_DIST / max_exact)
                         * (half - max_exact)).astype(jnp.int32)
    large = jnp.minimum(large, half - 1)
    return ret + jnp.where(n < max_exact, n, large)


def relative_distance_bias(table, S):
    rel = jnp.arange(-(S - 1), S, dtype=jnp.int32)
    return table[t5_bucket(rel)].astype(jnp.float32).T


def multiscale_pool(u, w_grp, scale):
    B, S, C = u.shape
    uf = u.astype(jnp.float32)
    cs = jnp.concatenate([jnp.zeros((B, 1, C), jnp.float32), jnp.cumsum(uf, axis=1)], axis=1)
    t = jnp.arange(S)
    means = []
    for g, w in enumerate(POOL_WINDOWS):
        lo = jnp.clip(t - w // 2, 0, S)
        hi = jnp.clip(t + w - w // 2, 0, S)
        seg = cs[:, :, g * POOL_GROUP:(g + 1) * POOL_GROUP]
        win_sum = seg[:, hi] - seg[:, lo]
        means.append(win_sum / (hi - lo).astype(jnp.float32)[None, :, None])
    pooled = (jnp.concatenate(means, axis=-1) - uf).astype(u.dtype)
    pooled = pooled.reshape(B, S, len(POOL_WINDOWS), POOL_GROUP)
    mixed = jnp.einsum('bsgc,gcd->bsgd', pooled, w_grp).reshape(B, S, C)
    return mixed * scale


def diff_attention(q, k, v, lam, dist_bias):
    B, S, H, _, hd = q.shape
    nb = S // QBLK
    qb = q.reshape(B, nb, QBLK, H, 2, hd).transpose(1, 0, 2, 3, 4, 5)
    k_pos = jnp.arange(S)
    scale = hd ** -0.5

    def block(args):
        qi, q0 = args
        logits = jnp.einsum('bqhcd,bkhcd->bhcqk', qi, k,
                            preferred_element_type=jnp.float32) * scale
        rel = k_pos[None, :] - (q0 + jnp.arange(QBLK))[:, None] + (S - 1)
        bias = dist_bias[:, rel]
        p = jax.nn.softmax(logits + bias[None, :, None], axis=-1)
        a = p[:, :, 0] - lam * p[:, :, 1]
        return jnp.einsum('bhqk,bkhe->bqhe', a.astype(v.dtype), v)

    out = lax.map(block, (qb, jnp.arange(nb, dtype=jnp.int32) * QBLK))
    return out.transpose(1, 0, 2, 3, 4).reshape(B, S, H, 2 * hd)


def mixer_sublayer(h, dist_bias, w_in, b_gate, w_pool_grp, pool_scale, w_branch_pool,
                   lambda_q1, lambda_k1, lambda_q2, lambda_k2, subln_gain, w_branch_attn,
                   w_out, ln_g, ln_b, lam_init):
    B, S, D = h.shape
    P, A = POOL_WIDTH, ATTN_WIDTH
    proj = h @ w_in
    u_pool, q, k, v, g_logits = jnp.split(proj, [P, P + A, P + 2 * A, P + 3 * A], axis=-1)
    gates = jax.nn.sigmoid((g_logits + b_gate).astype(jnp.float32)).astype(h.dtype)
    g_pool, g_attn = jnp.split(gates, N_BRANCHES, axis=-1)
    y_pool = multiscale_pool(u_pool, w_pool_grp, pool_scale) @ w_branch_pool
    q = q.reshape(B, S, N_DIFF_HEADS, 2, DIFF_HEAD_DIM)
    k = k.reshape(B, S, N_DIFF_HEADS, 2, DIFF_HEAD_DIM)
    v = v.reshape(B, S, N_DIFF_HEADS, 2 * DIFF_HEAD_DIM)
    f32 = jnp.float32
    lam = (jnp.exp(jnp.sum(lambda_q1.astype(f32) * lambda_k1.astype(f32)))
           - jnp.exp(jnp.sum(lambda_q2.astype(f32) * lambda_k2.astype(f32))) + lam_init)
    o = diff_attention(q, k, v, lam, dist_bias)
    o = rms_norm(o, subln_gain) * (1.0 - lam_init)
    y_attn = o.reshape(B, S, A) @ w_branch_attn
    y = (g_pool * y_pool + g_attn * y_attn) @ w_out
    return layer_norm(ALPHA * h + y, ln_g, ln_b)


def route(t, w_router, router_bias):
    N = t.shape[0]
    s = jax.nn.sigmoid(t.astype(jnp.float32) @ w_router.astype(jnp.float32))
    biased = s + router_bias.astype(jnp.float32)
    grp = biased.reshape(N, N_EXPERT_GROUPS, N_EXPERTS // N_EXPERT_GROUPS)
    grp_score = jnp.sum(lax.top_k(grp, 2)[0], axis=-1)
    _, gidx = lax.top_k(grp_score, TOPK_GROUPS)
    gmask = jnp.any(gidx[:, :, None] == jnp.arange(N_EXPERT_GROUPS)[None, None, :], axis=1)
    masked = jnp.where(gmask[:, :, None], grp, -jnp.inf).reshape(N, N_EXPERTS)
    _, idx = lax.top_k(masked, TOP_K)
    w = jnp.take_along_axis(s, idx, axis=1)
    w = w / jnp.sum(w, axis=-1, keepdims=True) * ROUTED_SCALE
    return idx, w


def routed_experts(t, idx, wts, w_gate, w_up, w_down):
    N, D = t.shape
    K = idx.shape[1]
    E = w_gate.shape[0]
    NK = N * K
    flat_e = idx.reshape(-1)
    flat_t = jnp.repeat(jnp.arange(N, dtype=jnp.int32), K)
    flat_w = wts.reshape(-1)
    order = jnp.argsort(flat_e)
    se, st, sw = flat_e[order], flat_t[order], flat_w[order]
    counts = jnp.bincount(flat_e, length=E)
    padded = (counts + MOE_BLK - 1) // MOE_BLK * MOE_BLK
    start = jnp.cumsum(counts) - counts
    pend = jnp.cumsum(padded)
    pstart = pend - padded
    dest = pstart[se] + (jnp.arange(NK) - start[se])
    R = (NK + E * (MOE_BLK - 1) + MOE_BLK - 1) // MOE_BLK * MOE_BLK
    nblk = R // MOE_BLK
    row_tok = jnp.full((R,), N, jnp.int32).at[dest].set(st)
    row_w = jnp.zeros((R,), t.dtype).at[dest].set(sw)
    blk_e = jnp.minimum(jnp.searchsorted(pend, jnp.arange(nblk) * MOE_BLK, side='right'), E - 1)
    t_pad = jnp.concatenate([t, jnp.zeros((1, D), t.dtype)], axis=0)

    def block(args):
        e, tok, w = args
        xb = t_pad[tok]
        hid = jax.nn.silu(xb @ w_gate[e]) * (xb @ w_up[e])
        return (hid @ w_down[e]) * w[:, None]

    out = lax.map(block, (blk_e, row_tok.reshape(nblk, MOE_BLK), row_w.reshape(nblk, MOE_BLK)))
    return jax.ops.segment_sum(out.reshape(R, D), row_tok, num_segments=N + 1)[:N]


def moe_sublayer(h, w_router, router_bias, w_exp_gate, w_exp_up, w_exp_down,
                 w_sh_gate, w_sh_up, w_sh_down, ln_g, ln_b):
    B, S, D = h.shape
    t = h.reshape(B * S, D)
    idx, wts = route(t, w_router, router_bias)
    routed = routed_experts(t, idx, wts.astype(t.dtype), w_exp_gate, w_exp_up, w_exp_down)
    shared = (jax.nn.silu(t @ w_sh_gate) * (t @ w_sh_up)) @ w_sh_down
    y = (routed + shared).reshape(B, S, D)
    return layer_norm(ALPHA * h + y, ln_g, ln_b)


def setup_inputs(seed: int = 0) -> dict:
    key = jax.random.key(seed)
    ks = jax.random.split(key, 32)
    L, D, E = DEPTH, D_MODEL, N_EXPERTS
    HD = DIFF_HEAD_DIM

    def nrm(k, shape, s):
        return jax.random.normal(k, shape, jnp.float32) * s

    return {
        "x": nrm(ks[0], (BATCH, SEQ, D), 1.0),
        "rel_bias_table": nrm(ks[1], (REL_BUCKETS, N_DIFF_HEADS), 0.5),
        "w_in": nrm(ks[2], (L, D, IN_WIDTH), D ** -0.5),
        "b_gate": nrm(ks[3], (L, N_BRANCHES * D), 0.02),
        "w_pool_grp": nrm(ks[4], (L, len(POOL_WINDOWS), POOL_GROUP, POOL_GROUP), POOL_GROUP ** -0.5),
        "pool_scale": 1.0 + nrm(ks[5], (L, POOL_WIDTH), 0.02),
        "w_branch_pool": nrm(ks[6], (L, POOL_WIDTH, D), POOL_WIDTH ** -0.5 * BETA),
        "lambda_q1": nrm(ks[7], (L, HD), 0.1),
        "lambda_k1": nrm(ks[8], (L, HD), 0.1),
        "lambda_q2": nrm(ks[9], (L, HD), 0.1),
        "lambda_k2": nrm(ks[10], (L, HD), 0.1),
        "subln_gain": 1.0 + nrm(ks[11], (L, 2 * HD), 0.02),
        "w_branch_attn": nrm(ks[12], (L, ATTN_WIDTH, D), ATTN_WIDTH ** -0.5 * BETA),
        "w_out": nrm(ks[13], (L, D, D), D ** -0.5 * BETA),
        "ln1_g": 1.0 + nrm(ks[14], (L, D), 0.02),
        "ln1_b": nrm(ks[15], (L, D), 0.02),
        "w_router": nrm(ks[16], (L, D, E), D ** -0.5),
        "router_bias": nrm(ks[17], (L, E), 0.01),
        "w_exp_gate": nrm(ks[18], (L, E, D, EXPERT_DIM), D ** -0.5),
        "w_exp_up": nrm(ks[19], (L, E, D, EXPERT_DIM), D ** -0.5),
        "w_exp_down": nrm(ks[20], (L, E, EXPERT_DIM, D), EXPERT_DIM ** -0.5 * BETA),
        "w_sh_gate": nrm(ks[21], (L, D, SHARED_DIM), D ** -0.5),
        "w_sh_up": nrm(ks[22], (L, D, SHARED_DIM), D ** -0.5),
        "w_sh_down": nrm(ks[23], (L, SHARED_DIM, D), SHARED_DIM ** -0.5 * BETA),
        "ln2_g": 1.0 + nrm(ks[24], (L, D), 0.02),
        "ln2_b": nrm(ks[25], (L, D), 0.02),
    }


def reference(x, rel_bias_table, w_in, b_gate, w_pool_grp, pool_scale, w_branch_pool,
              lambda_q1, lambda_k1, lambda_q2, lambda_k2, subln_gain, w_branch_attn, w_out,
              ln1_g, ln1_b, w_router, router_bias, w_exp_gate, w_exp_up, w_exp_down,
              w_sh_gate, w_sh_up, w_sh_down, ln2_g, ln2_b):
    S = x.shape[1]
    dist_bias = relative_distance_bias(rel_bias_table, S)
    h = x
    for i in range(DEPTH):
        h = mixer_sublayer(h, dist_bias, w_in[i], b_gate[i], w_pool_grp[i], pool_scale[i],
                           w_branch_pool[i], lambda_q1[i], lambda_k1[i], lambda_q2[i],
                           lambda_k2[i], subln_gain[i], w_branch_attn[i], w_out[i],
                           ln1_g[i], ln1_b[i], lambda_init_fn(i))
        h = moe_sublayer(h, w_router[i], router_bias[i], w_exp_gate[i], w_exp_up[i],
                         w_exp_down[i], w_sh_gate[i], w_sh_up[i], w_sh_down[i],
                         ln2_g[i], ln2_b[i])
    return h
```

```python
import functools
import math

import numpy as np
import jax
import jax.numpy as jnp
from jax import lax
from jax.experimental import pallas as pl
from jax.experimental.pallas import tpu as pltpu

F32 = jnp.float32
BF16 = jnp.bfloat16

POOL_WINDOWS = (2, 4, 8, 16)
N_DIFF_HEADS = 4
DIFF_HEAD_DIM = 64
REL_BUCKETS = 32
REL_MAX_DIST = 128
TOP_K = 8
N_EXPERT_GROUPS = 8
TOPK_GROUPS = 4
ROUTED_SCALE = 2.5
LN_EPS = 1e-5

LANES = 128
SUBLANES = 8
VMEM_LIMIT_BYTES = 56 * 1024 * 1024

TOKEN_TILE = 512
MERGE_TILE = 256
EXPERT_BLOCK = 256
COMBINE_TILE = 128

_NT = (((1,), (1,)), ((), ()))


def _lambda_init(layer_idx):
    return 0.8 - 0.6 * math.exp(-0.3 * layer_idx)


def _dot(a, b):
    return jnp.dot(a, b, preferred_element_type=F32)


def _split_bf16(a):
    hi = a.astype(BF16)
    lo = (a - hi.astype(F32)).astype(BF16)
    return hi, lo


def _sigmoid(z):
    return 1.0 / (1.0 + jnp.exp(-z))


def _layer_norm(z, g, b):
    mu = jnp.mean(z, axis=-1, keepdims=True)
    zc = z - mu
    var = jnp.mean(zc * zc, axis=-1, keepdims=True)
    return zc * lax.rsqrt(var + LN_EPS) * g + b


def _params(*sem):
    return pltpu.CompilerParams(dimension_semantics=sem, vmem_limit_bytes=VMEM_LIMIT_BYTES)


def _const_spec(shape):
    nd = len(shape)
    return pl.BlockSpec(shape, lambda *_: (0,) * nd)


def _inproj_kernel(x_ref, w_ref, u_ref, qk_ref, vt_ref, *, pool_w, attn_w, q_scale):
    xb = x_ref[...].astype(BF16)
    p = _dot(xb, w_ref[...])
    u_ref[...] = p[:, :pool_w]
    q = p[:, pool_w:pool_w + attn_w] * q_scale
    k = p[:, pool_w + attn_w:pool_w + 2 * attn_w]
    qk_ref[:, :attn_w] = q.astype(BF16)
    qk_ref[:, attn_w:] = k.astype(BF16)
    v = p[:, pool_w + 2 * attn_w:pool_w + 3 * attn_w]
    vt_ref[...] = v.T.astype(BF16)


def _inproj(x, w_uqkv, pool_w, attn_w):
    B, S, D = x.shape
    T = TOKEN_TILE
    ns = S // T
    width = w_uqkv.shape[1]
    kern = functools.partial(_inproj_kernel, pool_w=pool_w, attn_w=attn_w,
                             q_scale=DIFF_HEAD_DIM ** -0.5)
    return pl.pallas_call(
        kern,
        grid=(B, ns),
        in_specs=[pl.BlockSpec((None, T, D), lambda b, s: (b, s, 0)),
                  _const_spec((D, width))],
        out_specs=[pl.BlockSpec((None, T, pool_w), lambda b, s: (b, s, 0)),
                   pl.BlockSpec((None, T, 2 * attn_w), lambda b, s: (b, s, 0)),
                   pl.BlockSpec((None, None, attn_w, T), lambda b, s: (b, s, 0, 0))],
        out_shape=[jax.ShapeDtypeStruct((B, S, pool_w), F32),
                   jax.ShapeDtypeStruct((B, S, 2 * attn_w), BF16),
                   jax.ShapeDtypeStruct((B, ns, attn_w, T), BF16)],
        compiler_params=_params("parallel", "parallel"),
        name="inproj",
    )(x, w_uqkv)


def _t5_bucket_np(rel):
    half = REL_BUCKETS // 2
    max_exact = half // 2
    ret = np.where(rel > 0, half, 0)
    n = np.abs(rel)
    nf = np.maximum(n, 1).astype(np.float32)
    large = max_exact + (np.log(nf / max_exact) / math.log(REL_MAX_DIST / max_exact)
                         * (half - max_exact)).astype(np.int32)
    large = np.minimum(large, half - 1)
    return ret + np.where(n < max_exact, n, large)


def _t5_bucket(rel):
    half = REL_BUCKETS // 2
    max_exact = half // 2
    ret = jnp.where(rel > 0, half, 0)
    n = jnp.abs(rel)
    nf = jnp.maximum(n, 1).astype(F32)
    large = max_exact + (jnp.log(nf / max_exact) / math.log(REL_MAX_DIST / max_exact)
                         * (half - max_exact)).astype(jnp.int32)
    large = jnp.minimum(large, half - 1)
    return ret + jnp.where(n < max_exact, n, large)


def _bias_tables(table, S, T):
    far = np.arange(T + 1, S)
    if far.size:
        assert np.all(_t5_bucket_np(-far) == _t5_bucket_np(-far[-1]))
        assert np.all(_t5_bucket_np(far) == _t5_bucket_np(far[-1]))
    rel = jnp.arange(-(S - 1), S, dtype=jnp.int32)
    dist_bias = table[_t5_bucket(rel)].astype(F32).T
    key = np.arange(T)[:, None]
    qry = np.arange(T)[None, :]
    idx = np.stack([(d - 1) * T + key - qry + (S - 1) for d in range(3)])
    idx = np.clip(idx, 0, 2 * S - 2)
    band = dist_bias[:, idx]
    far_c = jnp.stack([dist_bias[:, 0], dist_bias[:, 2 * S - 2]], axis=1)
    return band, far_c


def _attn_kernel(far_ref, q_ref, k_ref, vt_ref, band_ref, lq1_ref, lk1_ref, lq2_ref, lk2_ref,
                 gain_ref, o_ref, acc_ref, m_ref, l_ref, *, T, nk, lam_init):
    h = pl.program_id(1)
    i = pl.program_id(2)
    q = q_ref[...]
    lane = lax.broadcasted_iota(jnp.int32, q.shape, 1)
    zero = jnp.zeros_like(q)
    qz = (jnp.where(lane < DIFF_HEAD_DIM, q, zero), jnp.where(lane >= DIFF_HEAD_DIM, q, zero))

    m_ref[...] = jnp.full(m_ref.shape, -jnp.inf, F32)
    l_ref[...] = jnp.zeros(l_ref.shape, F32)
    acc_ref[...] = jnp.zeros(acc_ref.shape, F32)

    def tile(j, bias, const):
        kt = k_ref[pl.ds(pl.multiple_of(j * T, T), T), :]
        vt = vt_ref[j]
        for c in range(2):
            s = lax.dot_general(kt, qz[c], _NT, preferred_element_type=F32)
            if bias is not None:
                s = s + bias
            mt = jnp.max(s, axis=0, keepdims=True)
            if const is not None:
                mt = mt + const
            m_old = m_ref[c]
            m_new = jnp.maximum(m_old, mt)
            alpha = jnp.exp(m_old - m_new)
            shift = m_new if const is None else m_new - const
            e = jnp.exp(s - shift)
            l_ref[c] = alpha * l_ref[c] + jnp.sum(e, axis=0, keepdims=True)
            acc_ref[c] = alpha * acc_ref[c] + _dot(vt, e.astype(BF16))
            m_ref[c] = m_new

    def far_left(j, carry):
        tile(j, None, far_ref[h, 0])
        return carry

    def far_right(j, carry):
        tile(j, None, far_ref[h, 1])
        return carry

    lax.fori_loop(0, jnp.maximum(i - 1, 0), far_left, 0)
    for d in range(3):
        j = i - 1 + d

        @pl.when(jnp.logical_and(j >= 0, j < nk))
        def _():
            tile(jnp.clip(j, 0, nk - 1), band_ref[d], None)

    lax.fori_loop(i + 2, nk, far_right, 0)

    lam = (jnp.exp(jnp.sum(lq1_ref[...] * lk1_ref[...], keepdims=True))
           - jnp.exp(jnp.sum(lq2_ref[...] * lk2_ref[...], keepdims=True)) + lam_init)
    o = acc_ref[0] / l_ref[0] - lam * (acc_ref[1] / l_ref[1])
    ms = jnp.mean(o * o, axis=0, keepdims=True)
    y = o * lax.rsqrt(ms + LN_EPS) * gain_ref[...] * (1.0 - lam_init)
    o_ref[...] = y.T.astype(BF16)


def _attention(qk, vt, band, far_c, lq1, lk1, lq2, lk2, gain, lam_init):
    B, S, _ = qk.shape
    T = TOKEN_TILE
    nk = S // T
    H = N_DIFF_HEADS
    hw = 2 * DIFF_HEAD_DIM
    kern = functools.partial(_attn_kernel, T=T, nk=nk, lam_init=lam_init)
    vec = lambda a: a.reshape(1, DIFF_HEAD_DIM).astype(F32)
    return pl.pallas_call(
        kern,
        grid=(B, H, nk),
        in_specs=[pl.BlockSpec(memory_space=pltpu.SMEM),
                  pl.BlockSpec((None, T, hw), lambda b, h, i: (b, i, h)),
                  pl.BlockSpec((None, S, hw), lambda b, h, i: (b, 0, H + h)),
                  pl.BlockSpec((None, nk, hw, T), lambda b, h, i: (b, 0, h, 0)),
                  pl.BlockSpec((None, 3, T, T), lambda b, h, i: (h, 0, 0, 0)),
                  _const_spec((1, DIFF_HEAD_DIM)), _const_spec((1, DIFF_HEAD_DIM)),
                  _const_spec((1, DIFF_HEAD_DIM)), _const_spec((1, DIFF_HEAD_DIM)),
                  _const_spec((hw, 1))],
        out_specs=pl.BlockSpec((None, T, hw), lambda b, h, i: (b, i, h)),
        out_shape=jax.ShapeDtypeStruct((B, S, H * hw), BF16),
        scratch_shapes=[pltpu.VMEM((2, hw, T), F32),
                        pltpu.VMEM((2, 1, T), F32),
                        pltpu.VMEM((2, 1, T), F32)],
        compiler_params=_params("parallel", "parallel", "parallel"),
        name="diff_attn",
    )(far_c, qk, qk, vt, band, vec(lq1), vec(lk1), vec(lq2), vec(lk2),
      gain.reshape(hw, 1).astype(F32))


def _pool_matrices(TM):
    r = np.arange(TM)[:, None]
    c = np.arange(TM)[None, :]
    hpos = np.concatenate([np.arange(-SUBLANES, 0), np.arange(TM, TM + SUBLANES),
                           np.full(LANES - 2 * SUBLANES, 10 ** 6)])[None, :]
    main, halo = [], []
    for w in POOL_WINDOWS:
        lo, hi = r - w // 2, r + w - w // 2
        main.append((c >= lo) & (c < hi))
        halo.append((hpos >= lo) & (hpos < hi))
    return (jnp.asarray(np.stack(main), BF16), jnp.asarray(np.stack(halo), BF16))


def _merge_kernel(x_ref, u_ref, up_ref, un_ref, o_ref, wg_ref, bg_ref, am_ref, ah_ref, wgrp_ref,
                  ps_ref, wbp_ref, wba_ref, wo_ref, lng_ref, lnb_ref, h_ref, *, TM, S, alpha):
    s_idx = pl.program_id(1)
    ns = pl.num_programs(1)
    x = x_ref[...]
    D = x.shape[1]
    xb = x.astype(BF16)
    gates = _sigmoid(_dot(xb, wg_ref[...]) + bg_ref[...])

    u = u_ref[...]
    P = u.shape[1]
    prev = jnp.where(s_idx > 0, up_ref[...], 0.0)
    nxt = jnp.where(s_idx < ns - 1, un_ref[...], 0.0)
    halo = jnp.concatenate([prev, nxt, jnp.zeros((LANES - 2 * SUBLANES, P), F32)], axis=0)
    u_hi, u_lo = _split_bf16(u)
    h_hi, h_lo = _split_bf16(halo)
    pos = s_idx * TM + lax.broadcasted_iota(jnp.int32, (TM, 1), 0)
    G = P // len(POOL_WINDOWS)
    mixed = []
    for g, w in enumerate(POOL_WINDOWS):
        sl = slice(g * G, (g + 1) * G)
        win2 = (_dot(am_ref[g], jnp.concatenate([u_hi[:, sl], u_lo[:, sl]], axis=1))
                + _dot(ah_ref[g], jnp.concatenate([h_hi[:, sl], h_lo[:, sl]], axis=1)))
        win = win2[:, :G] + win2[:, G:]
        cnt = (jnp.minimum(pos + (w - w // 2), S) - jnp.maximum(pos - w // 2, 0)).astype(F32)
        pooled = win / cnt - u[:, sl]
        mixed.append(_dot(pooled.astype(BF16), wgrp_ref[g]))
    mix = jnp.concatenate(mixed, axis=1) * ps_ref[...]
    y_pool = _dot(mix.astype(BF16), wbp_ref[...])
    y_attn = _dot(o_ref[...], wba_ref[...])
    merged = gates[:, :D] * y_pool + gates[:, D:] * y_attn
    y = _dot(merged.astype(BF16), wo_ref[...])
    h_ref[...] = _layer_norm(alpha * x + y, lng_ref[...], lnb_ref[...])


def _merge(x, u, o, w_gate, b_gate, w_grp, pool_scale, w_bp, w_ba, w_out, ln_g, ln_b, alpha):
    B, S, D = x.shape
    P = u.shape[2]
    A = o.shape[2]
    TM = MERGE_TILE
    ns = S // TM
    hb = TM // SUBLANES
    am, ah = _pool_matrices(TM)
    nw = len(POOL_WINDOWS)
    G = P // nw
    row = lambda a: a.reshape(1, -1).astype(F32)
    kern = functools.partial(_merge_kernel, TM=TM, S=S, alpha=alpha)
    return pl.pallas_call(
        kern,
        grid=(B, ns),
        in_specs=[pl.BlockSpec((None, TM, D), lambda b, s: (b, s, 0)),
                  pl.BlockSpec((None, TM, P), lambda b, s: (b, s, 0)),
                  pl.BlockSpec((None, SUBLANES, P),
                               lambda b, s: (b, jnp.maximum(s * hb - 1, 0), 0)),
                  pl.BlockSpec((None, SUBLANES, P),
                               lambda b, s: (b, jnp.minimum((s + 1) * hb, S // SUBLANES - 1), 0)),
                  pl.BlockSpec((None, TM, A), lambda b, s: (b, s, 0)),
                  _const_spec((D, 2 * D)), _const_spec((1, 2 * D)),
                  _const_spec((nw, TM, TM)), _const_spec((nw, TM, LANES)),
                  _const_spec((nw, G, G)), _const_spec((1, P)),
                  _const_spec((P, D)), _const_spec((A, D)), _const_spec((D, D)),
                  _const_spec((1, D)), _const_spec((1, D))],
        out_specs=pl.BlockSpec((None, TM, D), lambda b, s: (b, s, 0)),
        out_shape=jax.ShapeDtypeStruct((B, S, D), F32),
        compiler_params=_params("parallel", "parallel"),
        name="merge",
    )(x, u, u, u, o, w_gate, row(b_gate), am, ah, w_grp, row(pool_scale), w_bp, w_ba, w_out,
      row(ln_g), row(ln_b))


def _route_kernel(h_ref, wrh_ref, wrl_ref, rb_ref, tri_ref, ones_ref, wsg_ref, wsu_ref, wsd_ref,
                  idx_ref, wt_ref, rank_ref, cnt_ref, pre_ref, cnt_acc, *, TM, E, alpha):
    t = pl.program_id(0)

    @pl.when(t == 0)
    def _():
        cnt_acc[...] = jnp.zeros(cnt_acc.shape, F32)

    h = h_ref[...]
    hb, hl = _split_bf16(h)
    logits = (lax.dot_general(wrh_ref[...], hb, _NT, preferred_element_type=F32)
              + lax.dot_general(wrh_ref[...], hl, _NT, preferred_element_type=F32)
              + lax.dot_general(wrl_ref[...], hb, _NT, preferred_element_type=F32))
    s = _sigmoid(logits)
    biased = s + rb_ref[...]
    gsz = E // N_EXPERT_GROUPS
    sub = lax.broadcasted_iota(jnp.int32, (gsz, TM), 0).astype(F32)
    neg = -jnp.inf
    grp, gscore = [], []
    for g in range(N_EXPERT_GROUPS):
        bg = biased[g * gsz:(g + 1) * gsz, :]
        m1 = jnp.max(bg, axis=0, keepdims=True)
        first = jnp.min(jnp.where(bg == m1, sub, float(gsz)), axis=0, keepdims=True)
        m2 = jnp.max(jnp.where(sub == first, neg, bg), axis=0, keepdims=True)
        grp.append(bg)
        gscore.append(m1 + m2)
    masked = []
    for g in range(N_EXPERT_GROUPS):
        beaten = jnp.zeros((1, TM), F32)
        for g2 in range(N_EXPERT_GROUPS):
            if g2 == g:
                continue
            wins = (gscore[g2] > gscore[g]) if g2 > g else (gscore[g2] >= gscore[g])
            beaten = beaten + wins.astype(F32)
        masked.append(jnp.where(beaten < float(TOPK_GROUPS), grp[g], neg))
    masked = jnp.concatenate(masked, axis=0)

    row = lax.broadcasted_iota(jnp.int32, (E, TM), 0).astype(F32)
    sel = jnp.zeros((E, TM), F32)
    firsts, ws = [], []
    for _ in range(TOP_K):
        mx = jnp.max(masked, axis=0, keepdims=True)
        first = jnp.min(jnp.where(masked == mx, row, float(E)), axis=0, keepdims=True)
        oh = row == first
        ws.append(jnp.sum(jnp.where(oh, s, 0.0), axis=0, keepdims=True))
        masked = jnp.where(oh, neg, masked)
        sel = jnp.where(oh, 1.0, sel)
        firsts.append(first)
    wsum = ws[0]
    for w in ws[1:]:
        wsum = wsum + w

    sel_b = sel.astype(BF16)
    base = jnp.concatenate([cnt_acc[...]] * (TM // LANES), axis=1)
    rank_all = _dot(sel_b, tri_ref[...]) + base
    cnt_acc[...] = cnt_acc[...] + _dot(sel_b, ones_ref[...])
    cnt_ref[...] = cnt_acc[...]
    for k in range(TOP_K):
        oh = row == firsts[k]
        idx_ref[k:k + 1, :] = firsts[k].astype(jnp.int32)
        wt_ref[k:k + 1, :] = ws[k] / wsum * ROUTED_SCALE
        rank_ref[k:k + 1, :] = jnp.sum(jnp.where(oh, rank_all, 0.0), axis=0,
                                       keepdims=True).astype(jnp.int32)

    g_act = _dot(hb, wsg_ref[...])
    hid = g_act * _sigmoid(g_act) * _dot(hb, wsu_ref[...])
    pre_ref[...] = alpha * h + _dot(hid.astype(BF16), wsd_ref[...])


def _route(h, wr_t_hi, wr_t_lo, router_bias, w_sg, w_su, w_sd, alpha):
    N, D = h.shape
    E = wr_t_hi.shape[0]
    TM = TOKEN_TILE
    Hs = w_sg.shape[1]
    tri = jnp.asarray(np.triu(np.ones((TM, TM), np.float32), k=1), BF16)
    ones = jnp.ones((TM, LANES), BF16)
    kern = functools.partial(_route_kernel, TM=TM, E=E, alpha=alpha)
    kt = pl.BlockSpec((TOP_K, TM), lambda t: (0, t))
    return pl.pallas_call(
        kern,
        grid=(N // TM,),
        in_specs=[pl.BlockSpec((TM, D), lambda t: (t, 0)),
                  _const_spec((E, D)), _const_spec((E, D)), _const_spec((E, 1)),
                  _const_spec((TM, TM)), _const_spec((TM, LANES)),
                  _const_spec((D, Hs)), _const_spec((D, Hs)), _const_spec((Hs, D))],
        out_specs=[kt, kt, kt, _const_spec((E, LANES)), pl.BlockSpec((TM, D), lambda t: (t, 0))],
        out_shape=[jax.ShapeDtypeStruct((TOP_K, N), jnp.int32),
                   jax.ShapeDtypeStruct((TOP_K, N), F32),
                   jax.ShapeDtypeStruct((TOP_K, N), jnp.int32),
                   jax.ShapeDtypeStruct((E, LANES), F32),
                   jax.ShapeDtypeStruct((N, D), F32)],
        scratch_shapes=[pltpu.VMEM((E, LANES), F32)],
        compiler_params=_params("arbitrary"),
        name="route_shared",
    )(h, wr_t_hi, wr_t_lo, router_bias.reshape(E, 1).astype(F32), tri, ones, w_sg, w_su, w_sd)


def _row_copy(src_hbm, row, buf, slot, r, sem):
    return pltpu.make_async_copy(src_hbm.at[pl.ds(row, 1)], buf.at[slot, pl.ds(r, 1)],
                                 sem.at[slot])


def _expert_kernel(blk_e_ref, nact_ref, tokc_ref, tokn_ref, roww_ref, h_hbm, wg_ref, wu_ref,
                   wd_ref, y_ref, xbuf, wg_s, wu_s, wd_s, sem, *, BLK):
    b = pl.program_id(0)
    nact = nact_ref[0]
    slot = b % 2

    def issue(tok_ref, dst_slot):
        for r in range(BLK):
            _row_copy(h_hbm, tok_ref[0, 0, r], xbuf, dst_slot, r, sem).start()

    @pl.when(b == 0)
    def _():
        issue(tokc_ref, 0)

    @pl.when(b + 1 < nact)
    def _():
        issue(tokn_ref, 1 - slot)

    @pl.when(b < nact)
    def _():
        e = blk_e_ref[b]
        e_prev = blk_e_ref[jnp.maximum(b - 1, 0)]

        @pl.when(jnp.logical_or(b == 0, e != e_prev))
        def _():
            wg_s[...] = wg_ref[...].astype(BF16)
            wu_s[...] = wu_ref[...].astype(BF16)
            wd_s[...] = wd_ref[...].astype(BF16)

        for r in range(BLK):
            _row_copy(h_hbm, 0, xbuf, slot, r, sem).wait()
        x = xbuf[slot].astype(BF16)
        g_act = _dot(x, wg_s[...])
        hid = g_act * _sigmoid(g_act) * _dot(x, wu_s[...])
        y_ref[...] = _dot(hid.astype(BF16), wd_s[...]) * roww_ref[...]

    @pl.when(b >= nact)
    def _():
        y_ref[...] = jnp.zeros(y_ref.shape, F32)


def _experts(h, blk_e, nact, row_tok, row_w, w_eg, w_eu, w_ed):
    N, D = h.shape
    E, _, He = w_eg.shape
    BLK = EXPERT_BLOCK
    NB = row_tok.shape[0] // BLK
    tok3 = row_tok.reshape(NB, 1, BLK)
    kern = functools.partial(_expert_kernel, BLK=BLK)
    grid_spec = pltpu.PrefetchScalarGridSpec(
        num_scalar_prefetch=2,
        grid=(NB,),
        in_specs=[pl.BlockSpec((1, 1, BLK), lambda b, be, na: (b, 0, 0), memory_space=pltpu.SMEM),
                  pl.BlockSpec((1, 1, BLK), lambda b, be, na: (jnp.minimum(b + 1, NB - 1), 0, 0),
                               memory_space=pltpu.SMEM),
                  pl.BlockSpec((BLK, 1), lambda b, be, na: (b, 0)),
                  pl.BlockSpec(memory_space=pl.ANY),
                  pl.BlockSpec((None, D, He), lambda b, be, na: (be[b], 0, 0)),
                  pl.BlockSpec((None, D, He), lambda b, be, na: (be[b], 0, 0)),
                  pl.BlockSpec((None, He, D), lambda b, be, na: (be[b], 0, 0))],
        out_specs=pl.BlockSpec((BLK, D), lambda b, be, na: (b, 0)),
        scratch_shapes=[pltpu.VMEM((2, BLK, D), F32),
                        pltpu.VMEM((D, He), BF16), pltpu.VMEM((D, He), BF16),
                        pltpu.VMEM((He, D), BF16),
                        pltpu.SemaphoreType.DMA((2,))],
    )
    return pl.pallas_call(
        kern,
        grid_spec=grid_spec,
        out_shape=jax.ShapeDtypeStruct((NB * BLK, D), F32),
        compiler_params=_params("arbitrary"),
        name="routed_experts",
    )(blk_e, nact, tok3, tok3, row_w.reshape(NB * BLK, 1), h, w_eg, w_eu, w_ed)


def _combine_kernel(destc_ref, destn_ref, pre_ref, lng_ref, lnb_ref, y_hbm, o_ref, buf, sem, *, TM):
    t = pl.program_id(0)
    nt = pl.num_programs(0)
    slot = t % 2

    def issue(dest_ref, dst_slot):
        for k in range(TOP_K):
            for r in range(TM):
                _row_copy(y_hbm, dest_ref[0, k, r], buf, dst_slot, k * TM + r, sem).start()

    @pl.when(t == 0)
    def _():
        issue(destc_ref, 0)

    @pl.when(t + 1 < nt)
    def _():
        issue(destn_ref, 1 - slot)

    for r in range(TOP_K * TM):
        _row_copy(y_hbm, 0, buf, slot, r, sem).wait()
    z = pre_ref[...]
    for k in range(TOP_K):
        z = z + buf[slot, pl.ds(k * TM, TM), :]
    o_ref[...] = _layer_norm(z, lng_ref[...], lnb_ref[...])


def _combine(y_sorted, dest, pre, ln_g, ln_b):
    N, D = pre.shape
    TM = COMBINE_TILE
    nt = N // TM
    dest3 = dest.reshape(TOP_K, nt, TM).transpose(1, 0, 2)
    kern = functools.partial(_combine_kernel, TM=TM)
    row = lambda a: a.reshape(1, -1).astype(F32)
    return pl.pallas_call(
        kern,
        grid=(nt,),
        in_specs=[pl.BlockSpec((1, TOP_K, TM), lambda t: (t, 0, 0), memory_space=pltpu.SMEM),
                  pl.BlockSpec((1, TOP_K, TM), lambda t: (jnp.minimum(t + 1, nt - 1), 0, 0),
                               memory_space=pltpu.SMEM),
                  pl.BlockSpec((TM, D), lambda t: (t, 0)),
                  _const_spec((1, D)), _const_spec((1, D)),
                  pl.BlockSpec(memory_space=pl.ANY)],
        out_specs=pl.BlockSpec((TM, D), lambda t: (t, 0)),
        out_shape=jax.ShapeDtypeStruct((N, D), F32),
        scratch_shapes=[pltpu.VMEM((2, TOP_K * TM, D), F32), pltpu.SemaphoreType.DMA((2,))],
        compiler_params=_params("arbitrary"),
        name="combine",
    )(dest3, dest3, pre, row(ln_g), row(ln_b), y_sorted)


def _dispatch(idx, wts, rank, cnt, N, E):
    BLK = EXPERT_BLOCK
    NK = N * TOP_K
    NB = (NK + E * (BLK - 1)) // BLK
    counts = cnt[:, 0].astype(jnp.int32)
    nblk_e = (counts + BLK - 1) // BLK
    bend = jnp.cumsum(nblk_e)
    bstart = bend - nblk_e
    dest = bstart[idx] * BLK + rank
    tok = jnp.broadcast_to(jnp.arange(N, dtype=jnp.int32)[None, :], dest.shape)
    row_tok = jnp.zeros((NB * BLK,), jnp.int32).at[dest.reshape(-1)].set(tok.reshape(-1))
    row_w = jnp.zeros((NB * BLK,), F32).at[dest.reshape(-1)].set(wts.reshape(-1))
    blk_e = jnp.minimum(jnp.searchsorted(bend, jnp.arange(NB, dtype=jnp.int32), side='right'),
                        E - 1).astype(jnp.int32)
    nact = bend[-1:].astype(jnp.int32)
    return dest, row_tok, row_w, blk_e, nact


def kernel(x, rel_bias_table, w_in, b_gate, w_pool_grp, pool_scale, w_branch_pool, lambda_q1,
           lambda_k1, lambda_q2, lambda_k2, subln_gain, w_branch_attn, w_out, ln1_g, ln1_b,
           w_router, router_bias, w_exp_gate, w_exp_up, w_exp_down, w_sh_gate, w_sh_up,
           w_sh_down, ln2_g, ln2_b):
    B, S, D = x.shape
    L = w_in.shape[0]
    E = w_router.shape[2]
    P = pool_scale.shape[1]
    A = w_branch_attn.shape[1]
    N = B * S
    alpha = (2 * L) ** 0.25
    assert S % TOKEN_TILE == 0 and S % MERGE_TILE == 0 and N % COMBINE_TILE == 0
    assert A == N_DIFF_HEADS * 2 * DIFF_HEAD_DIM and TOKEN_TILE >= REL_MAX_DIST
    band, far_c = _bias_tables(rel_bias_table, S, TOKEN_TILE)

    h = x
    for i in range(L):
        w_uqkv = w_in[i][:, :P + 3 * A].astype(BF16)
        w_gate = w_in[i][:, P + 3 * A:].astype(BF16)
        u, qk, vt = _inproj(h, w_uqkv, P, A)
        o = _attention(qk, vt, band, far_c, lambda_q1[i], lambda_k1[i], lambda_q2[i],
                       lambda_k2[i], subln_gain[i], _lambda_init(i))
        h1 = _merge(h, u, o, w_gate, b_gate[i], w_pool_grp[i].astype(BF16), pool_scale[i],
                    w_branch_pool[i].astype(BF16), w_branch_attn[i].astype(BF16),
                    w_out[i].astype(BF16), ln1_g[i], ln1_b[i], alpha)
        t = h1.reshape(N, D)
        wr_hi, wr_lo = _split_bf16(w_router[i].astype(F32).T)
        idx, wts, rank, cnt, pre = _route(t, wr_hi, wr_lo, router_bias[i],
                                          w_sh_gate[i].astype(BF16), w_sh_up[i].astype(BF16),
                                          w_sh_down[i].astype(BF16), alpha)
        dest, row_tok, row_w, blk_e, nact = _dispatch(idx, wts, rank, cnt, N, E)
        y_sorted = _experts(t, blk_e, nact, row_tok, row_w, w_exp_gate[i], w_exp_up[i],
                            w_exp_down[i])
        h = _combine(y_sorted, dest, pre, ln2_g[i], ln2_b[i]).reshape(B, S, D)
    return h
```

```python
import functools
import math

import numpy as np
import jax
import jax.numpy as jnp
from jax import lax
from jax.experimental import pallas as pl
from jax.experimental.pallas import tpu as pltpu

F32 = jnp.float32
BF16 = jnp.bfloat16

POOL_WINDOWS = (2, 4, 8, 16)
N_DIFF_HEADS = 4
DIFF_HEAD_DIM = 64
REL_BUCKETS = 32
REL_MAX_DIST = 128
TOP_K = 8
N_EXPERT_GROUPS = 8
TOPK_GROUPS = 4
ROUTED_SCALE = 2.5
LN_EPS = 1e-5

LANES = 128
SUBLANES = 8
VMEM_LIMIT_BYTES = 56 * 1024 * 1024

TOKEN_TILE = 512
MERGE_TILE = 256
EXPERT_BLOCK = 256

_NT = (((1,), (1,)), ((), ()))


def _lambda_init(layer_idx):
    return 0.8 - 0.6 * math.exp(-0.3 * layer_idx)


def _dot(a, b):
    return jnp.dot(a, b, preferred_element_type=F32)


def _split_bf16(a):
    hi = a.astype(BF16)
    lo = (a - hi.astype(F32)).astype(BF16)
    return hi, lo


def _sigmoid(z):
    return 1.0 / (1.0 + jnp.exp(-z))


def _layer_norm(z, g, b):
    mu = jnp.mean(z, axis=-1, keepdims=True)
    zc = z - mu
    var = jnp.mean(zc * zc, axis=-1, keepdims=True)
    return zc * lax.rsqrt(var + LN_EPS) * g + b


def _params(*sem):
    return pltpu.CompilerParams(dimension_semantics=sem, vmem_limit_bytes=VMEM_LIMIT_BYTES)


def _const_spec(shape):
    nd = len(shape)
    return pl.BlockSpec(shape, lambda *_: (0,) * nd)


def _inproj_kernel(x_ref, w_ref, u_ref, qk_ref, vt_ref, *, pool_w, attn_w, q_scale):
    xb = x_ref[...].astype(BF16)
    p = _dot(xb, w_ref[...])
    u_ref[...] = p[:, :pool_w]
    q = p[:, pool_w:pool_w + attn_w] * q_scale
    k = p[:, pool_w + attn_w:pool_w + 2 * attn_w]
    qk_ref[:, :attn_w] = q.astype(BF16)
    qk_ref[:, attn_w:] = k.astype(BF16)
    v = p[:, pool_w + 2 * attn_w:pool_w + 3 * attn_w]
    vt_ref[...] = v.T.astype(BF16)


def _inproj(x, w_uqkv, pool_w, attn_w):
    B, S, D = x.shape
    T = TOKEN_TILE
    ns = S // T
    width = w_uqkv.shape[1]
    kern = functools.partial(_inproj_kernel, pool_w=pool_w, attn_w=attn_w,
                             q_scale=DIFF_HEAD_DIM ** -0.5)
    return pl.pallas_call(
        kern,
        grid=(B, ns),
        in_specs=[pl.BlockSpec((None, T, D), lambda b, s: (b, s, 0)),
                  _const_spec((D, width))],
        out_specs=[pl.BlockSpec((None, T, pool_w), lambda b, s: (b, s, 0)),
                   pl.BlockSpec((None, T, 2 * attn_w), lambda b, s: (b, s, 0)),
                   pl.BlockSpec((None, None, attn_w, T), lambda b, s: (b, s, 0, 0))],
        out_shape=[jax.ShapeDtypeStruct((B, S, pool_w), F32),
                   jax.ShapeDtypeStruct((B, S, 2 * attn_w), BF16),
                   jax.ShapeDtypeStruct((B, ns, attn_w, T), BF16)],
        compiler_params=_params("parallel", "parallel"),
        name="inproj",
    )(x, w_uqkv)


def _t5_bucket_np(rel):
    half = REL_BUCKETS // 2
    max_exact = half // 2
    ret = np.where(rel > 0, half, 0)
    n = np.abs(rel)
    nf = np.maximum(n, 1).astype(np.float32)
    large = max_exact + (np.log(nf / max_exact) / math.log(REL_MAX_DIST / max_exact)
                         * (half - max_exact)).astype(np.int32)
    large = np.minimum(large, half - 1)
    return ret + np.where(n < max_exact, n, large)


def _t5_bucket(rel):
    half = REL_BUCKETS // 2
    max_exact = half // 2
    ret = jnp.where(rel > 0, half, 0)
    n = jnp.abs(rel)
    nf = jnp.maximum(n, 1).astype(F32)
    large = max_exact + (jnp.log(nf / max_exact) / math.log(REL_MAX_DIST / max_exact)
                         * (half - max_exact)).astype(jnp.int32)
    large = jnp.minimum(large, half - 1)
    return ret + jnp.where(n < max_exact, n, large)


def _bias_tables(table, S, T):
    far = np.arange(T + 1, S)
    if far.size:
        assert np.all(_t5_bucket_np(-far) == _t5_bucket_np(-far[-1]))
        assert np.all(_t5_bucket_np(far) == _t5_bucket_np(far[-1]))
    rel = jnp.arange(-(S - 1), S, dtype=jnp.int32)
    dist_bias = table[_t5_bucket(rel)].astype(F32).T
    pad = jnp.pad(dist_bias, ((0, 0), (2 * T, 2 * T)), mode='edge')
    diag = jnp.stack([lax.slice_in_dim(pad, (d - 1) * T - (T - 1) + S - 1 + 2 * T,
                                       (d - 1) * T + T + S - 1 + 2 * T, axis=1)
                      for d in range(3)], axis=1)
    w = jnp.concatenate([diag[..., ::-1], diag[..., :1]], axis=-1)
    flat = jnp.tile(w, (1, 1, T))[..., :T * (2 * T - 1)]
    band = flat.reshape(w.shape[0], 3, T, 2 * T - 1)[..., T - 1:]
    far_c = jnp.stack([dist_bias[:, 0], dist_bias[:, 2 * S - 2]], axis=1)
    return band, far_c


def _attn_kernel(far_ref, q_ref, k_ref, vt_ref, band_ref, lq1_ref, lk1_ref, lq2_ref, lk2_ref,
                 gain_ref, o_ref, acc_ref, m_ref, l_ref, *, T, nk, lam_init):
    h = pl.program_id(1)
    i = pl.program_id(2)
    q = q_ref[...]
    lane = lax.broadcasted_iota(jnp.int32, q.shape, 1)
    zero = jnp.zeros_like(q)
    qz = (jnp.where(lane < DIFF_HEAD_DIM, q, zero), jnp.where(lane >= DIFF_HEAD_DIM, q, zero))

    m_ref[...] = jnp.full(m_ref.shape, -jnp.inf, F32)
    l_ref[...] = jnp.zeros(l_ref.shape, F32)
    acc_ref[...] = jnp.zeros(acc_ref.shape, F32)

    def tile(j, bias, const):
        kt = k_ref[pl.ds(pl.multiple_of(j * T, T), T), :]
        vt = vt_ref[j]
        for c in range(2):
            s = lax.dot_general(kt, qz[c], _NT, preferred_element_type=F32)
            if bias is not None:
                s = s + bias
            mt = jnp.max(s, axis=0, keepdims=True)
            if const is not None:
                mt = mt + const
            m_old = m_ref[c]
            m_new = jnp.maximum(m_old, mt)
            alpha = jnp.exp(m_old - m_new)
            shift = m_new if const is None else m_new - const
            e = jnp.exp(s - shift)
            l_ref[c] = alpha * l_ref[c] + jnp.sum(e, axis=0, keepdims=True)
            acc_ref[c] = alpha * acc_ref[c] + _dot(vt, e.astype(BF16))
            m_ref[c] = m_new

    def far_left(j, carry):
        tile(j, None, far_ref[h, 0])
        return carry

    def far_right(j, carry):
        tile(j, None, far_ref[h, 1])
        return carry

    lax.fori_loop(0, jnp.maximum(i - 1, 0), far_left, 0)
    for d in range(3):
        j = i - 1 + d

        @pl.when(jnp.logical_and(j >= 0, j < nk))
        def _():
            tile(jnp.clip(j, 0, nk - 1), band_ref[d], None)

    lax.fori_loop(i + 2, nk, far_right, 0)

    lam = (jnp.exp(jnp.sum(lq1_ref[...] * lk1_ref[...], keepdims=True))
           - jnp.exp(jnp.sum(lq2_ref[...] * lk2_ref[...], keepdims=True)) + lam_init)
    o = acc_ref[0] / l_ref[0] - lam * (acc_ref[1] / l_ref[1])
    ms = jnp.mean(o * o, axis=0, keepdims=True)
    y = o * lax.rsqrt(ms + LN_EPS) * gain_ref[...] * (1.0 - lam_init)
    o_ref[...] = y.T.astype(BF16)


def _attention(qk, vt, band, far_c, lq1, lk1, lq2, lk2, gain, lam_init):
    B, S, _ = qk.shape
    T = TOKEN_TILE
    nk = S // T
    H = N_DIFF_HEADS
    hw = 2 * DIFF_HEAD_DIM
    kern = functools.partial(_attn_kernel, T=T, nk=nk, lam_init=lam_init)
    vec = lambda a: a.reshape(1, DIFF_HEAD_DIM).astype(F32)
    return pl.pallas_call(
        kern,
        grid=(B, H, nk),
        in_specs=[pl.BlockSpec(memory_space=pltpu.SMEM),
                  pl.BlockSpec((None, T, hw), lambda b, h, i: (b, i, h)),
                  pl.BlockSpec((None, S, hw), lambda b, h, i: (b, 0, H + h)),
                  pl.BlockSpec((None, nk, hw, T), lambda b, h, i: (b, 0, h, 0)),
                  pl.BlockSpec((None, 3, T, T), lambda b, h, i: (h, 0, 0, 0)),
                  _const_spec((1, DIFF_HEAD_DIM)), _const_spec((1, DIFF_HEAD_DIM)),
                  _const_spec((1, DIFF_HEAD_DIM)), _const_spec((1, DIFF_HEAD_DIM)),
                  _const_spec((hw, 1))],
        out_specs=pl.BlockSpec((None, T, hw), lambda b, h, i: (b, i, h)),
        out_shape=jax.ShapeDtypeStruct((B, S, H * hw), BF16),
        scratch_shapes=[pltpu.VMEM((2, hw, T), F32),
                        pltpu.VMEM((2, 1, T), F32),
                        pltpu.VMEM((2, 1, T), F32)],
        compiler_params=_params("parallel", "parallel", "parallel"),
        name="diff_attn",
    )(far_c, qk, qk, vt, band, vec(lq1), vec(lk1), vec(lq2), vec(lk2),
      gain.reshape(hw, 1).astype(F32))


def _pool_matrices(TM):
    r = np.arange(TM)[:, None]
    c = np.arange(TM)[None, :]
    hpos = np.concatenate([np.arange(-SUBLANES, 0), np.arange(TM, TM + SUBLANES),
                           np.full(LANES - 2 * SUBLANES, 10 ** 6)])[None, :]
    main, halo = [], []
    for w in POOL_WINDOWS:
        lo, hi = r - w // 2, r + w - w // 2
        main.append((c >= lo) & (c < hi))
        halo.append((hpos >= lo) & (hpos < hi))
    return (jnp.asarray(np.stack(main), BF16), jnp.asarray(np.stack(halo), BF16))


def _merge_kernel(x_ref, u_ref, up_ref, un_ref, o_ref, wg_ref, bg_ref, am_ref, ah_ref, wgrp_ref,
                  ps_ref, wbp_ref, wba_ref, wo_ref, lng_ref, lnb_ref, h_ref, *, TM, S, alpha):
    s_idx = pl.program_id(1)
    ns = pl.num_programs(1)
    x = x_ref[...]
    D = x.shape[1]
    xb = x.astype(BF16)
    gates = _sigmoid(_dot(xb, wg_ref[...]) + bg_ref[...])

    u = u_ref[...]
    P = u.shape[1]
    prev = jnp.where(s_idx > 0, up_ref[...], 0.0)
    nxt = jnp.where(s_idx < ns - 1, un_ref[...], 0.0)
    halo = jnp.concatenate([prev, nxt, jnp.zeros((LANES - 2 * SUBLANES, P), F32)], axis=0)
    u_hi, u_lo = _split_bf16(u)
    h_hi, h_lo = _split_bf16(halo)
    pos = s_idx * TM + lax.broadcasted_iota(jnp.int32, (TM, 1), 0)
    G = P // len(POOL_WINDOWS)
    mixed = []
    for g, w in enumerate(POOL_WINDOWS):
        sl = slice(g * G, (g + 1) * G)
        win2 = (_dot(am_ref[g], jnp.concatenate([u_hi[:, sl], u_lo[:, sl]], axis=1))
                + _dot(ah_ref[g], jnp.concatenate([h_hi[:, sl], h_lo[:, sl]], axis=1)))
        win = win2[:, :G] + win2[:, G:]
        cnt = (jnp.minimum(pos + (w - w // 2), S) - jnp.maximum(pos - w // 2, 0)).astype(F32)
        pooled = win / cnt - u[:, sl]
        mixed.append(_dot(pooled.astype(BF16), wgrp_ref[g]))
    mix = jnp.concatenate(mixed, axis=1) * ps_ref[...]
    y_pool = _dot(mix.astype(BF16), wbp_ref[...])
    y_attn = _dot(o_ref[...], wba_ref[...])
    merged = gates[:, :D] * y_pool + gates[:, D:] * y_attn
    y = _dot(merged.astype(BF16), wo_ref[...])
    h_ref[...] = _layer_norm(alpha * x + y, lng_ref[...], lnb_ref[...])


def _merge(x, u, o, w_gate, b_gate, w_grp, pool_scale, w_bp, w_ba, w_out, ln_g, ln_b, alpha):
    B, S, D = x.shape
    P = u.shape[2]
    A = o.shape[2]
    TM = MERGE_TILE
    ns = S // TM
    hb = TM // SUBLANES
    am, ah = _pool_matrices(TM)
    nw = len(POOL_WINDOWS)
    G = P // nw
    row = lambda a: a.reshape(1, -1).astype(F32)
    kern = functools.partial(_merge_kernel, TM=TM, S=S, alpha=alpha)
    return pl.pallas_call(
        kern,
        grid=(B, ns),
        in_specs=[pl.BlockSpec((None, TM, D), lambda b, s: (b, s, 0)),
                  pl.BlockSpec((None, TM, P), lambda b, s: (b, s, 0)),
                  pl.BlockSpec((None, SUBLANES, P),
                               lambda b, s: (b, jnp.maximum(s * hb - 1, 0), 0)),
                  pl.BlockSpec((None, SUBLANES, P),
                               lambda b, s: (b, jnp.minimum((s + 1) * hb, S // SUBLANES - 1), 0)),
                  pl.BlockSpec((None, TM, A), lambda b, s: (b, s, 0)),
                  _const_spec((D, 2 * D)), _const_spec((1, 2 * D)),
                  _const_spec((nw, TM, TM)), _const_spec((nw, TM, LANES)),
                  _const_spec((nw, G, G)), _const_spec((1, P)),
                  _const_spec((P, D)), _const_spec((A, D)), _const_spec((D, D)),
                  _const_spec((1, D)), _const_spec((1, D))],
        out_specs=pl.BlockSpec((None, TM, D), lambda b, s: (b, s, 0)),
        out_shape=jax.ShapeDtypeStruct((B, S, D), F32),
        compiler_params=_params("parallel", "parallel"),
        name="merge",
    )(x, u, u, u, o, w_gate, row(b_gate), am, ah, w_grp, row(pool_scale), w_bp, w_ba, w_out,
      row(ln_g), row(ln_b))


def _route_kernel(h_ref, wrh_ref, wrl_ref, rb_ref, tri_ref, ones_ref, wsg_ref, wsu_ref, wsd_ref,
                  idx_ref, wt_ref, rank_ref, cnt_ref, pre_ref, cnt_acc, *, TM, E, alpha):
    t = pl.program_id(0)

    @pl.when(t == 0)
    def _():
        cnt_acc[...] = jnp.zeros(cnt_acc.shape, F32)

    h = h_ref[...]
    hb, hl = _split_bf16(h)
    logits = (lax.dot_general(wrh_ref[...], hb, _NT, preferred_element_type=F32)
              + lax.dot_general(wrh_ref[...], hl, _NT, preferred_element_type=F32)
              + lax.dot_general(wrl_ref[...], hb, _NT, preferred_element_type=F32))
    s = _sigmoid(logits)
    biased = s + rb_ref[...]
    gsz = E // N_EXPERT_GROUPS
    sub = lax.broadcasted_iota(jnp.int32, (gsz, TM), 0).astype(F32)
    neg = -jnp.inf
    grp, gscore = [], []
    for g in range(N_EXPERT_GROUPS):
        bg = biased[g * gsz:(g + 1) * gsz, :]
        m1 = jnp.max(bg, axis=0, keepdims=True)
        first = jnp.min(jnp.where(bg == m1, sub, float(gsz)), axis=0, keepdims=True)
        m2 = jnp.max(jnp.where(sub == first, neg, bg), axis=0, keepdims=True)
        grp.append(bg)
        gscore.append(m1 + m2)
    masked = []
    for g in range(N_EXPERT_GROUPS):
        beaten = jnp.zeros((1, TM), F32)
        for g2 in range(N_EXPERT_GROUPS):
            if g2 == g:
                continue
            wins = (gscore[g2] > gscore[g]) if g2 > g else (gscore[g2] >= gscore[g])
            beaten = beaten + wins.astype(F32)
        masked.append(jnp.where(beaten < float(TOPK_GROUPS), grp[g], neg))
    masked = jnp.concatenate(masked, axis=0)

    row = lax.broadcasted_iota(jnp.int32, (E, TM), 0).astype(F32)
    sel = jnp.zeros((E, TM), F32)
    firsts, ws = [], []
    for _ in range(TOP_K):
        mx = jnp.max(masked, axis=0, keepdims=True)
        first = jnp.min(jnp.where(masked == mx, row, float(E)), axis=0, keepdims=True)
        oh = row == first
        ws.append(jnp.sum(jnp.where(oh, s, 0.0), axis=0, keepdims=True))
        masked = jnp.where(oh, neg, masked)
        sel = jnp.where(oh, 1.0, sel)
        firsts.append(first)
    wsum = ws[0]
    for w in ws[1:]:
        wsum = wsum + w

    sel_b = sel.astype(BF16)
    base = jnp.concatenate([cnt_acc[...]] * (TM // LANES), axis=1)
    rank_all = _dot(sel_b, tri_ref[...]) + base
    cnt_acc[...] = cnt_acc[...] + _dot(sel_b, ones_ref[...])
    cnt_ref[...] = cnt_acc[...]
    for k in range(TOP_K):
        oh = row == firsts[k]
        idx_k = firsts[k].astype(jnp.int32)
        wt_k = ws[k] / wsum * ROUTED_SCALE
        rank_k = jnp.sum(jnp.where(oh, rank_all, 0.0), axis=0, keepdims=True).astype(jnp.int32)
        for c in range(TM // LANES):
            sl = slice(c * LANES, (c + 1) * LANES)
            idx_ref[c, k:k + 1, :] = idx_k[:, sl]
            wt_ref[c, k:k + 1, :] = wt_k[:, sl]
            rank_ref[c, k:k + 1, :] = rank_k[:, sl]

    g_act = _dot(hb, wsg_ref[...])
    hid = g_act * _sigmoid(g_act) * _dot(hb, wsu_ref[...])
    pre_ref[...] = alpha * h + _dot(hid.astype(BF16), wsd_ref[...])


def _route(h, wr_t_hi, wr_t_lo, router_bias, w_sg, w_su, w_sd, alpha):
    N, D = h.shape
    E = wr_t_hi.shape[0]
    TM = TOKEN_TILE
    Hs = w_sg.shape[1]
    tri = jnp.asarray(np.triu(np.ones((TM, TM), np.float32), k=1), BF16)
    ones = jnp.ones((TM, LANES), BF16)
    kern = functools.partial(_route_kernel, TM=TM, E=E, alpha=alpha)
    kt = pl.BlockSpec((TM // LANES, TOP_K, LANES), lambda t: (t, 0, 0))
    return pl.pallas_call(
        kern,
        grid=(N // TM,),
        in_specs=[pl.BlockSpec((TM, D), lambda t: (t, 0)),
                  _const_spec((E, D)), _const_spec((E, D)), _const_spec((E, 1)),
                  _const_spec((TM, TM)), _const_spec((TM, LANES)),
                  _const_spec((D, Hs)), _const_spec((D, Hs)), _const_spec((Hs, D))],
        out_specs=[kt, kt, kt, _const_spec((E, LANES)), pl.BlockSpec((TM, D), lambda t: (t, 0))],
        out_shape=[jax.ShapeDtypeStruct((N // LANES, TOP_K, LANES), jnp.int32),
                   jax.ShapeDtypeStruct((N // LANES, TOP_K, LANES), F32),
                   jax.ShapeDtypeStruct((N // LANES, TOP_K, LANES), jnp.int32),
                   jax.ShapeDtypeStruct((E, LANES), F32),
                   jax.ShapeDtypeStruct((N, D), F32)],
        scratch_shapes=[pltpu.VMEM((E, LANES), F32)],
        compiler_params=_params("arbitrary"),
        name="route_shared",
    )(h, wr_t_hi, wr_t_lo, router_bias.reshape(E, 1).astype(F32), tri, ones, w_sg, w_su, w_sd)


def _dispatch_kernel(rowstart_ref, cnt_ref, nact_ref, idx_ref, rank_ref, h_ref, xs_hbm,
                     stage, zeros, sem, zsem, *, TM, BLK, NB, E):
    t = pl.program_id(0)
    nt = pl.num_programs(0)
    slot = t % 2

    def zero_copy(row, n, start):
        cp = pltpu.make_async_copy(zeros.at[pl.ds(0, n)], xs_hbm.at[pl.ds(row, n)], zsem)
        cp.start() if start else cp.wait()

    def stage_copy(s):
        return pltpu.make_async_copy(stage.at[s], xs_hbm.at[pl.ds(0, TM)], sem.at[s])

    def wait_slot(s):
        for _ in range(TOP_K):
            stage_copy(s).wait()

    @pl.when(t == 0)
    def _():
        zeros[...] = jnp.zeros(zeros.shape, F32)
        nact = nact_ref[0]

        def tail(e, c, *, start):
            first = rowstart_ref[e] + cnt_ref[e]
            end = rowstart_ref[e] + (cnt_ref[e] + BLK - 1) // BLK * BLK
            mid = jnp.minimum((first + SUBLANES - 1) // SUBLANES * SUBLANES, end)

            def one(r, c2):
                zero_copy(r, 1, start)
                return c2

            def eight(i, c2):
                zero_copy(pl.multiple_of(mid + i * SUBLANES, SUBLANES), SUBLANES, start)
                return c2

            lax.fori_loop(first, mid, one, 0)
            lax.fori_loop(0, (end - mid) // SUBLANES, eight, 0)
            return c

        def idle(b, c, *, start):
            zero_copy(pl.multiple_of(b * BLK, BLK), BLK, start)
            return c

        for start in (True, False):
            lax.fori_loop(0, E, functools.partial(tail, start=start), 0)
            lax.fori_loop(nact, NB, functools.partial(idle, start=start), 0)

    @pl.when(t >= 2)
    def _():
        wait_slot(slot)

    stage[slot] = h_ref[...]
    for c in range(TM // LANES):
        def row_body(r, carry, c=c):
            for k in range(TOP_K):
                dest = rowstart_ref[idx_ref[c, k, r]] + rank_ref[c, k, r]
                pltpu.make_async_copy(stage.at[slot, pl.ds(c * LANES + r, 1)],
                                      xs_hbm.at[pl.ds(dest, 1)], sem.at[slot]).start()
            return carry

        lax.fori_loop(0, LANES, row_body, 0)

    @pl.when(t == nt - 1)
    def _():
        @pl.when(t >= 1)
        def _():
            wait_slot(1 - slot)

        wait_slot(slot)


def _dispatch(h, idx, rank, rowstart, counts, nact, NB):
    N, D = h.shape
    E = rowstart.shape[0]
    TM = MERGE_TILE
    BLK = EXPERT_BLOCK
    sb = TM // LANES
    kern = functools.partial(_dispatch_kernel, TM=TM, BLK=BLK, NB=NB, E=E)
    smem = lambda: pl.BlockSpec((sb, TOP_K, LANES), lambda t, *_: (t, 0, 0),
                                memory_space=pltpu.SMEM)
    grid_spec = pltpu.PrefetchScalarGridSpec(
        num_scalar_prefetch=3,
        grid=(N // TM,),
        in_specs=[smem(), smem(), pl.BlockSpec((TM, D), lambda t, *_: (t, 0))],
        out_specs=pl.BlockSpec(memory_space=pl.ANY),
        scratch_shapes=[pltpu.VMEM((2, TM, D), F32), pltpu.VMEM((BLK, D), F32),
                        pltpu.SemaphoreType.DMA((2,)), pltpu.SemaphoreType.DMA(())],
    )
    return pl.pallas_call(
        kern,
        grid_spec=grid_spec,
        out_shape=jax.ShapeDtypeStruct((NB * BLK, D), F32),
        compiler_params=_params("arbitrary"),
        name="dispatch",
    )(rowstart, counts, nact, idx, rank, h)


def _expert_kernel(blk_e_ref, nact_ref, x_ref, wg_ref, wu_ref, wd_ref, y_ref, wg_s, wu_s, wd_s):
    b = pl.program_id(0)
    nact = nact_ref[0]

    @pl.when(b < nact)
    def _():
        e = blk_e_ref[b]
        e_prev = blk_e_ref[jnp.maximum(b - 1, 0)]

        @pl.when(jnp.logical_or(b == 0, e != e_prev))
        def _():
            wg_s[...] = wg_ref[...].astype(BF16)
            wu_s[...] = wu_ref[...].astype(BF16)
            wd_s[...] = wd_ref[...].astype(BF16)

        x = x_ref[...].astype(BF16)
        g_act = _dot(x, wg_s[...])
        hid = g_act * _sigmoid(g_act) * _dot(x, wu_s[...])
        y_ref[...] = _dot(hid.astype(BF16), wd_s[...])

    @pl.when(b >= nact)
    def _():
        y_ref[...] = jnp.zeros(y_ref.shape, F32)


def _experts(x_sorted, blk_e, nact, NB, w_eg, w_eu, w_ed):
    D = x_sorted.shape[1]
    E, _, He = w_eg.shape
    BLK = EXPERT_BLOCK
    live = lambda b, na: jnp.minimum(b, na[0] - 1)
    grid_spec = pltpu.PrefetchScalarGridSpec(
        num_scalar_prefetch=2,
        grid=(NB,),
        in_specs=[pl.BlockSpec((BLK, D), lambda b, be, na: (live(b, na), 0)),
                  pl.BlockSpec((None, D, He), lambda b, be, na: (be[live(b, na)], 0, 0)),
                  pl.BlockSpec((None, D, He), lambda b, be, na: (be[live(b, na)], 0, 0)),
                  pl.BlockSpec((None, He, D), lambda b, be, na: (be[live(b, na)], 0, 0))],
        out_specs=pl.BlockSpec((BLK, D), lambda b, be, na: (b, 0)),
        scratch_shapes=[pltpu.VMEM((D, He), BF16), pltpu.VMEM((D, He), BF16),
                        pltpu.VMEM((He, D), BF16)],
    )
    return pl.pallas_call(
        _expert_kernel,
        grid_spec=grid_spec,
        out_shape=jax.ShapeDtypeStruct((NB * BLK, D), F32),
        compiler_params=_params("arbitrary"),
        name="routed_experts",
    )(blk_e, nact, x_sorted, w_eg, w_eu, w_ed)


def _combine_kernel(rowstart_ref, idx_ref, rank_ref, wt_ref, pre_ref, lng_ref, lnb_ref, y_hbm,
                    o_ref, buf, sem, *, TM):
    t = pl.program_id(0)
    nt = pl.num_programs(0) - 1
    slot = t % 2

    @pl.when(t < nt)
    def _():
        for k in range(TOP_K):
            for r in range(TM):
                src = rowstart_ref[idx_ref[0, k, r]] + rank_ref[0, k, r]
                pltpu.make_async_copy(y_hbm.at[pl.ds(src, 1)],
                                      buf.at[slot, pl.ds(k * TM + r, 1)], sem.at[slot]).start()

    @pl.when(t >= 1)
    def _():
        prev = 1 - slot
        for k in range(TOP_K):
            pltpu.make_async_copy(y_hbm.at[pl.ds(0, TM)], buf.at[prev, pl.ds(k * TM, TM)],
                                  sem.at[prev]).wait()
        w = jnp.concatenate([wt_ref[0], jnp.zeros((TM - TOP_K, TM), F32)], axis=0).T
        z = pre_ref[...]
        for k in range(TOP_K):
            z = z + w[:, k:k + 1] * buf[prev, pl.ds(k * TM, TM), :]
        o_ref[...] = _layer_norm(z, lng_ref[...], lnb_ref[...])


def _combine(y_sorted, idx, rank, wts, rowstart, pre, ln_g, ln_b):
    N, D = pre.shape
    TM = LANES
    nt = N // TM
    kern = functools.partial(_combine_kernel, TM=TM)
    row = lambda a: a.reshape(1, -1).astype(F32)
    ahead = lambda t, *_: (jnp.minimum(t, nt - 1), 0, 0)
    behind3 = lambda t, *_: (jnp.maximum(t - 1, 0), 0, 0)
    behind = lambda t, *_: (jnp.maximum(t - 1, 0), 0)
    grid_spec = pltpu.PrefetchScalarGridSpec(
        num_scalar_prefetch=1,
        grid=(nt + 1,),
        in_specs=[pl.BlockSpec((1, TOP_K, TM), ahead, memory_space=pltpu.SMEM),
                  pl.BlockSpec((1, TOP_K, TM), ahead, memory_space=pltpu.SMEM),
                  pl.BlockSpec((1, TOP_K, TM), behind3),
                  pl.BlockSpec((TM, D), behind),
                  pl.BlockSpec((1, D), lambda t, *_: (0, 0)),
                  pl.BlockSpec((1, D), lambda t, *_: (0, 0)),
                  pl.BlockSpec(memory_space=pl.ANY)],
        out_specs=pl.BlockSpec((TM, D), behind),
        scratch_shapes=[pltpu.VMEM((2, TOP_K * TM, D), F32), pltpu.SemaphoreType.DMA((2,))],
    )
    return pl.pallas_call(
        kern,
        grid_spec=grid_spec,
        out_shape=jax.ShapeDtypeStruct((N, D), F32),
        compiler_params=_params("arbitrary"),
        name="combine",
    )(rowstart, idx, rank, wts, pre, row(ln_g), row(ln_b), y_sorted)


def _dispatch_meta(cnt, N, E):
    BLK = EXPERT_BLOCK
    NB = (N * TOP_K + E * (BLK - 1)) // BLK
    counts = cnt[:, 0].astype(jnp.int32)
    nblk_e = (counts + BLK - 1) // BLK
    bend = jnp.cumsum(nblk_e)
    rowstart = (bend - nblk_e) * BLK
    blk_of = jnp.sum(bend[None, :] <= jnp.arange(NB, dtype=jnp.int32)[:, None], axis=1)
    blk_e = jnp.minimum(blk_of, E - 1).astype(jnp.int32)
    nact = bend[-1:].astype(jnp.int32)
    return counts, rowstart.astype(jnp.int32), blk_e, nact, NB


def kernel(x, rel_bias_table, w_in, b_gate, w_pool_grp, pool_scale, w_branch_pool, lambda_q1,
           lambda_k1, lambda_q2, lambda_k2, subln_gain, w_branch_attn, w_out, ln1_g, ln1_b,
           w_router, router_bias, w_exp_gate, w_exp_up, w_exp_down, w_sh_gate, w_sh_up,
           w_sh_down, ln2_g, ln2_b):
    B, S, D = x.shape
    L = w_in.shape[0]
    E = w_router.shape[2]
    P = pool_scale.shape[1]
    A = w_branch_attn.shape[1]
    N = B * S
    alpha = (2 * L) ** 0.25
    assert S % TOKEN_TILE == 0 and S % MERGE_TILE == 0
    assert A == N_DIFF_HEADS * 2 * DIFF_HEAD_DIM and TOKEN_TILE >= REL_MAX_DIST
    band, far_c = _bias_tables(rel_bias_table, S, TOKEN_TILE)

    h = x
    for i in range(L):
        w_uqkv = w_in[i][:, :P + 3 * A].astype(BF16)
        w_gate = w_in[i][:, P + 3 * A:].astype(BF16)
        u, qk, vt = _inproj(h, w_uqkv, P, A)
        o = _attention(qk, vt, band, far_c, lambda_q1[i], lambda_k1[i], lambda_q2[i],
                       lambda_k2[i], subln_gain[i], _lambda_init(i))
        h1 = _merge(h, u, o, w_gate, b_gate[i], w_pool_grp[i].astype(BF16), pool_scale[i],
                    w_branch_pool[i].astype(BF16), w_branch_attn[i].astype(BF16),
                    w_out[i].astype(BF16), ln1_g[i], ln1_b[i], alpha)
        t = h1.reshape(N, D)
        wr_hi, wr_lo = _split_bf16(w_router[i].astype(F32).T)
        idx, wts, rank, cnt, pre = _route(t, wr_hi, wr_lo, router_bias[i],
                                          w_sh_gate[i].astype(BF16), w_sh_up[i].astype(BF16),
                                          w_sh_down[i].astype(BF16), alpha)
        counts, rowstart, blk_e, nact, NB = _dispatch_meta(cnt, N, E)
        x_sorted = _dispatch(t, idx, rank, rowstart, counts, nact, NB)
        y_sorted = _experts(x_sorted, blk_e, nact, NB, w_exp_gate[i], w_exp_up[i], w_exp_down[i])
        h = _combine(y_sorted, idx, rank, wts, rowstart, pre, ln2_g[i], ln2_b[i]).reshape(B, S, D)
    return h
```

```python
import functools
import math

import numpy as np
import jax
import jax.numpy as jnp
from jax import lax
from jax.experimental import pallas as pl
from jax.experimental.pallas import tpu as pltpu

F32 = jnp.float32
BF16 = jnp.bfloat16

POOL_WINDOWS = (2, 4, 8, 16)
N_DIFF_HEADS = 4
DIFF_HEAD_DIM = 64
REL_BUCKETS = 32
REL_MAX_DIST = 128
TOP_K = 8
N_EXPERT_GROUPS = 8
TOPK_GROUPS = 4
ROUTED_SCALE = 2.5
LN_EPS = 1e-5
LOG2E = math.log2(math.e)

LANES = 128
SUBLANES = 8
VMEM_LIMIT_BYTES = 56 * 1024 * 1024

TOKEN_TILE = 512
MERGE_TILE = 256
EXPERT_BLOCK = 256

_NT = (((1,), (1,)), ((), ()))


def _lambda_init(layer_idx):
    return 0.8 - 0.6 * math.exp(-0.3 * layer_idx)


def _dot(a, b):
    return jnp.dot(a, b, preferred_element_type=F32)


def _split_bf16(a):
    hi = a.astype(BF16)
    lo = (a - hi.astype(F32)).astype(BF16)
    return hi, lo


def _sigmoid(z):
    return 1.0 / (1.0 + jnp.exp(-z))


def _layer_norm(z, g, b):
    mu = jnp.mean(z, axis=-1, keepdims=True)
    zc = z - mu
    var = jnp.mean(zc * zc, axis=-1, keepdims=True)
    return zc * lax.rsqrt(var + LN_EPS) * g + b


def _params(*sem):
    return pltpu.CompilerParams(dimension_semantics=sem, vmem_limit_bytes=VMEM_LIMIT_BYTES)


def _const_spec(shape):
    nd = len(shape)
    return pl.BlockSpec(shape, lambda *_: (0,) * nd)


def _inproj_kernel(x_ref, w_ref, u_ref, qk_ref, vt_ref, *, pool_w, attn_w, q_scale):
    xb = x_ref[...].astype(BF16)
    p = _dot(xb, w_ref[...])
    u_ref[...] = p[:, :pool_w]
    q = p[:, pool_w:pool_w + attn_w] * q_scale
    k = p[:, pool_w + attn_w:pool_w + 2 * attn_w]
    qk_ref[:, :attn_w] = q.astype(BF16)
    qk_ref[:, attn_w:] = k.astype(BF16)
    v = p[:, pool_w + 2 * attn_w:pool_w + 3 * attn_w]
    vt_ref[...] = v.T.astype(BF16)


def _inproj(x, w_uqkv, pool_w, attn_w):
    B, S, D = x.shape
    T = TOKEN_TILE
    ns = S // T
    width = w_uqkv.shape[1]
    kern = functools.partial(_inproj_kernel, pool_w=pool_w, attn_w=attn_w,
                             q_scale=DIFF_HEAD_DIM ** -0.5 * LOG2E)
    return pl.pallas_call(
        kern,
        grid=(B, ns),
        in_specs=[pl.BlockSpec((None, T, D), lambda b, s: (b, s, 0)),
                  _const_spec((D, width))],
        out_specs=[pl.BlockSpec((None, T, pool_w), lambda b, s: (b, s, 0)),
                   pl.BlockSpec((None, T, 2 * attn_w), lambda b, s: (b, s, 0)),
                   pl.BlockSpec((None, None, attn_w, T), lambda b, s: (b, s, 0, 0))],
        out_shape=[jax.ShapeDtypeStruct((B, S, pool_w), F32),
                   jax.ShapeDtypeStruct((B, S, 2 * attn_w), BF16),
                   jax.ShapeDtypeStruct((B, ns, attn_w, T), BF16)],
        compiler_params=_params("parallel", "parallel"),
        name="inproj",
    )(x, w_uqkv)


def _t5_bucket_np(rel):
    half = REL_BUCKETS // 2
    max_exact = half // 2
    ret = np.where(rel > 0, half, 0)
    n = np.abs(rel)
    nf = np.maximum(n, 1).astype(np.float32)
    large = max_exact + (np.log(nf / max_exact) / math.log(REL_MAX_DIST / max_exact)
                         * (half - max_exact)).astype(np.int32)
    large = np.minimum(large, half - 1)
    return ret + np.where(n < max_exact, n, large)


def _t5_bucket(rel):
    half = REL_BUCKETS // 2
    max_exact = half // 2
    ret = jnp.where(rel > 0, half, 0)
    n = jnp.abs(rel)
    nf = jnp.maximum(n, 1).astype(F32)
    large = max_exact + (jnp.log(nf / max_exact) / math.log(REL_MAX_DIST / max_exact)
                         * (half - max_exact)).astype(jnp.int32)
    large = jnp.minimum(large, half - 1)
    return ret + jnp.where(n < max_exact, n, large)


def _bias_tables(table, S, T):
    far = np.arange(T + 1, S)
    if far.size:
        assert np.all(_t5_bucket_np(-far) == _t5_bucket_np(-far[-1]))
        assert np.all(_t5_bucket_np(far) == _t5_bucket_np(far[-1]))
    rel = jnp.arange(-(S - 1), S, dtype=jnp.int32)
    dist_bias = table[_t5_bucket(rel)].astype(F32).T
    pad = jnp.pad(dist_bias, ((0, 0), (2 * T, 2 * T)), mode='edge')
    diag = jnp.stack([lax.slice_in_dim(pad, (d - 1) * T - (T - 1) + S - 1 + 2 * T,
                                       (d - 1) * T + T + S - 1 + 2 * T, axis=1)
                      for d in range(3)], axis=1)
    w = jnp.concatenate([diag[..., ::-1], diag[..., :1]], axis=-1)
    flat = jnp.tile(w, (1, 1, T))[..., :T * (2 * T - 1)]
    band = flat.reshape(w.shape[0], 3, T, 2 * T - 1)[..., T - 1:]
    far_c = jnp.stack([dist_bias[:, 0], dist_bias[:, 2 * S - 2]], axis=1)
    return band * LOG2E, far_c * LOG2E


def _attn_kernel(far_ref, q_ref, k_ref, vt_ref, band_ref, lq1_ref, lk1_ref, lq2_ref, lk2_ref,
                 gain_ref, o_ref, qz_ref, s00, s01, s10, s11, e00, e01, e10, e11, acc0, acc1,
                 mt_ref, al_ref, m_ref, l_ref, *, T, nk, lam_init):
    h = pl.program_id(1)
    i = pl.program_id(2)
    s_buf = ((s00, s01), (s10, s11))
    e_buf = ((e00, e01), (e10, e11))
    acc = (acc0, acc1)
    c_left = far_ref[h, 0]
    c_right = far_ref[h, 1]

    q = q_ref[...]
    lane = lax.broadcasted_iota(jnp.int32, q.shape, 1)
    zero = jnp.zeros_like(q)
    qz_ref[0] = jnp.where(lane < DIFF_HEAD_DIM, q, zero)
    qz_ref[1] = jnp.where(lane >= DIFF_HEAD_DIM, q, zero)
    m_ref[...] = jnp.full(m_ref.shape, -jnp.inf, F32)
    l_ref[...] = jnp.zeros(l_ref.shape, F32)
    acc0[...] = jnp.zeros(acc0.shape, F32)
    acc1[...] = jnp.zeros(acc1.shape, F32)

    has_prev = i >= 1
    has_next = i <= nk - 2
    n_left = jnp.maximum(i - 1, 0) - jnp.where(has_next, 0, 1)
    right0 = i + 2 + jnp.where(has_prev, 0, 1)

    def tile_of(p):
        if isinstance(p, int) and p == 0:
            return i, band_ref[1], None
        if isinstance(p, int) and p == 1:
            return (jnp.where(has_prev, i - 1, i + 2),
                    jnp.where(has_prev, band_ref[0], c_right), None)
        if isinstance(p, int) and p == 2:
            return (jnp.where(has_next, i + 1, i - 2),
                    jnp.where(has_next, band_ref[2], c_left), None)
        f = p - 3
        is_left = f < n_left
        return (jnp.where(is_left, f, f - n_left + right0), None,
                jnp.where(is_left, c_left, c_right))

    def stage_a(p, x):
        j, bias, const = tile_of(p)
        kt = k_ref[pl.ds(pl.multiple_of(j * T, T), T), :]
        for c in range(2):
            s = lax.dot_general(kt, qz_ref[c], _NT, preferred_element_type=F32)
            if bias is not None:
                s = s + bias
            s_buf[x][c][...] = s
            mt = jnp.max(s, axis=0, keepdims=True)
            mt_ref[2 * x + c] = mt if const is None else mt + const

    def stage_b(p, x):
        _, _, const = tile_of(p)
        for c in range(2):
            m_old = m_ref[c]
            m_new = jnp.maximum(m_old, mt_ref[2 * x + c])
            alpha = jnp.exp2(m_old - m_new)
            shift = m_new if const is None else m_new - const
            e = jnp.exp2(s_buf[x][c][...] - shift)
            l_ref[c] = alpha * l_ref[c] + jnp.sum(e, axis=0, keepdims=True)
            e_buf[x][c][...] = e.astype(BF16)
            al_ref[2 * x + c] = alpha
            m_ref[c] = m_new

    def stage_c(p, x):
        j, _, _ = tile_of(p)
        vt = vt_ref[j]
        for c in range(2):
            acc[c][...] = al_ref[2 * x + c] * acc[c][...] + _dot(vt, e_buf[x][c][...])

    def iteration(n, parity):
        static = isinstance(n, int)
        if not static or n < nk:
            stage_a(n, parity)
        if not static or 1 <= n <= nk:
            stage_b(n - 1, 1 - parity)
        if not static or 2 <= n <= nk + 1:
            stage_c(n - 2, parity)

    head = 5
    for n in range(head):
        iteration(n, n % 2)

    def pair(it, carry):
        n = head + 2 * it
        iteration(n, head % 2)
        iteration(n + 1, 1 - head % 2)
        return carry

    n_pairs = (nk - 1 - head) // 2
    lax.fori_loop(0, n_pairs, pair, 0)
    for n in range(head + 2 * n_pairs, nk + 2):
        iteration(n, n % 2)

    lam = (jnp.exp(jnp.sum(lq1_ref[...] * lk1_ref[...], keepdims=True))
           - jnp.exp(jnp.sum(lq2_ref[...] * lk2_ref[...], keepdims=True)) + lam_init)
    o = acc0[...] / l_ref[0] - lam * (acc1[...] / l_ref[1])
    ms = jnp.mean(o * o, axis=0, keepdims=True)
    y = o * lax.rsqrt(ms + LN_EPS) * gain_ref[...] * (1.0 - lam_init)
    o_ref[...] = y.T.astype(BF16)


def _attention(qk, vt, band, far_c, lq1, lk1, lq2, lk2, gain, lam_init):
    B, S, _ = qk.shape
    T = TOKEN_TILE
    nk = S // T
    assert nk >= 6, "the pipeline prologue assumes at least three far key tiles"
    H = N_DIFF_HEADS
    hw = 2 * DIFF_HEAD_DIM
    kern = functools.partial(_attn_kernel, T=T, nk=nk, lam_init=lam_init)
    vec = lambda a: a.reshape(1, DIFF_HEAD_DIM).astype(F32)
    return pl.pallas_call(
        kern,
        grid=(B, H, nk),
        in_specs=[pl.BlockSpec(memory_space=pltpu.SMEM),
                  pl.BlockSpec((None, T, hw), lambda b, h, i: (b, i, h)),
                  pl.BlockSpec((None, S, hw), lambda b, h, i: (b, 0, H + h)),
                  pl.BlockSpec((None, nk, hw, T), lambda b, h, i: (b, 0, h, 0)),
                  pl.BlockSpec((None, 3, T, T), lambda b, h, i: (h, 0, 0, 0)),
                  _const_spec((1, DIFF_HEAD_DIM)), _const_spec((1, DIFF_HEAD_DIM)),
                  _const_spec((1, DIFF_HEAD_DIM)), _const_spec((1, DIFF_HEAD_DIM)),
                  _const_spec((hw, 1))],
        out_specs=pl.BlockSpec((None, T, hw), lambda b, h, i: (b, i, h)),
        out_shape=jax.ShapeDtypeStruct((B, S, H * hw), BF16),
        scratch_shapes=([pltpu.VMEM((2, T, hw), BF16)]
                        + [pltpu.VMEM((T, T), F32)] * 4 + [pltpu.VMEM((T, T), BF16)] * 4
                        + [pltpu.VMEM((hw, T), F32)] * 2
                        + [pltpu.VMEM((4, 1, T), F32), pltpu.VMEM((4, 1, T), F32),
                           pltpu.VMEM((2, 1, T), F32), pltpu.VMEM((2, 1, T), F32)]),
        compiler_params=_params("parallel", "parallel", "parallel"),
        name="diff_attn",
    )(far_c, qk, qk, vt, band, vec(lq1), vec(lk1), vec(lq2), vec(lk2),
      gain.reshape(hw, 1).astype(F32))


def _pool_matrices(TM):
    r = np.arange(TM)[:, None]
    c = np.arange(TM)[None, :]
    hpos = np.concatenate([np.arange(-SUBLANES, 0), np.arange(TM, TM + SUBLANES),
                           np.full(LANES - 2 * SUBLANES, 10 ** 6)])[None, :]
    main, halo = [], []
    for w in POOL_WINDOWS:
        lo, hi = r - w // 2, r + w - w // 2
        main.append((c >= lo) & (c < hi))
        halo.append((hpos >= lo) & (hpos < hi))
    return (jnp.asarray(np.stack(main), BF16), jnp.asarray(np.stack(halo), BF16))


def _merge_kernel(x_ref, u_ref, up_ref, un_ref, o_ref, wg_ref, bg_ref, am_ref, ah_ref, wgrp_ref,
                  ps_ref, wbp_ref, wba_ref, wo_ref, lng_ref, lnb_ref, h_ref, *, TM, S, alpha):
    s_idx = pl.program_id(1)
    ns = pl.num_programs(1)
    x = x_ref[...]
    D = x.shape[1]
    xb = x.astype(BF16)
    gates = _sigmoid(_dot(xb, wg_ref[...]) + bg_ref[...])

    u = u_ref[...]
    P = u.shape[1]
    prev = jnp.where(s_idx > 0, up_ref[...], 0.0)
    nxt = jnp.where(s_idx < ns - 1, un_ref[...], 0.0)
    halo = jnp.concatenate([prev, nxt, jnp.zeros((LANES - 2 * SUBLANES, P), F32)], axis=0)
    u_hi, u_lo = _split_bf16(u)
    h_hi, h_lo = _split_bf16(halo)
    pos = s_idx * TM + lax.broadcasted_iota(jnp.int32, (TM, 1), 0)
    G = P // len(POOL_WINDOWS)
    mixed = []
    for g, w in enumerate(POOL_WINDOWS):
        sl = slice(g * G, (g + 1) * G)
        win2 = (_dot(am_ref[g], jnp.concatenate([u_hi[:, sl], u_lo[:, sl]], axis=1))
                + _dot(ah_ref[g], jnp.concatenate([h_hi[:, sl], h_lo[:, sl]], axis=1)))
        win = win2[:, :G] + win2[:, G:]
        cnt = (jnp.minimum(pos + (w - w // 2), S) - jnp.maximum(pos - w // 2, 0)).astype(F32)
        pooled = win / cnt - u[:, sl]
        mixed.append(_dot(pooled.astype(BF16), wgrp_ref[g]))
    mix = jnp.concatenate(mixed, axis=1) * ps_ref[...]
    y_pool = _dot(mix.astype(BF16), wbp_ref[...])
    y_attn = _dot(o_ref[...], wba_ref[...])
    merged = gates[:, :D] * y_pool + gates[:, D:] * y_attn
    y = _dot(merged.astype(BF16), wo_ref[...])
    h_ref[...] = _layer_norm(alpha * x + y, lng_ref[...], lnb_ref[...])


def _merge(x, u, o, w_gate, b_gate, w_grp, pool_scale, w_bp, w_ba, w_out, ln_g, ln_b, alpha):
    B, S, D = x.shape
    P = u.shape[2]
    A = o.shape[2]
    TM = MERGE_TILE
    ns = S // TM
    hb = TM // SUBLANES
    am, ah = _pool_matrices(TM)
    nw = len(POOL_WINDOWS)
    G = P // nw
    row = lambda a: a.reshape(1, -1).astype(F32)
    kern = functools.partial(_merge_kernel, TM=TM, S=S, alpha=alpha)
    return pl.pallas_call(
        kern,
        grid=(B, ns),
        in_specs=[pl.BlockSpec((None, TM, D), lambda b, s: (b, s, 0)),
                  pl.BlockSpec((None, TM, P), lambda b, s: (b, s, 0)),
                  pl.BlockSpec((None, SUBLANES, P),
                               lambda b, s: (b, jnp.maximum(s * hb - 1, 0), 0)),
                  pl.BlockSpec((None, SUBLANES, P),
                               lambda b, s: (b, jnp.minimum((s + 1) * hb, S // SUBLANES - 1), 0)),
                  pl.BlockSpec((None, TM, A), lambda b, s: (b, s, 0)),
                  _const_spec((D, 2 * D)), _const_spec((1, 2 * D)),
                  _const_spec((nw, TM, TM)), _const_spec((nw, TM, LANES)),
                  _const_spec((nw, G, G)), _const_spec((1, P)),
                  _const_spec((P, D)), _const_spec((A, D)), _const_spec((D, D)),
                  _const_spec((1, D)), _const_spec((1, D))],
        out_specs=pl.BlockSpec((None, TM, D), lambda b, s: (b, s, 0)),
        out_shape=jax.ShapeDtypeStruct((B, S, D), F32),
        compiler_params=_params("parallel", "parallel"),
        name="merge",
    )(x, u, u, u, o, w_gate, row(b_gate), am, ah, w_grp, row(pool_scale), w_bp, w_ba, w_out,
      row(ln_g), row(ln_b))


def _route_kernel(h_ref, wrh_ref, wrl_ref, rb_ref, tri_ref, ones_ref, wsg_ref, wsu_ref, wsd_ref,
                  idx_ref, wt_ref, rank_ref, cnt_ref, pre_ref, cnt_acc, *, TM, E, alpha):
    t = pl.program_id(0)

    @pl.when(t == 0)
    def _():
        cnt_acc[...] = jnp.zeros(cnt_acc.shape, F32)

    h = h_ref[...]
    hb, hl = _split_bf16(h)
    logits = (lax.dot_general(wrh_ref[...], hb, _NT, preferred_element_type=F32)
              + lax.dot_general(wrh_ref[...], hl, _NT, preferred_element_type=F32)
              + lax.dot_general(wrl_ref[...], hb, _NT, preferred_element_type=F32))
    s = _sigmoid(logits)
    biased = s + rb_ref[...]
    gsz = E // N_EXPERT_GROUPS
    sub = lax.broadcasted_iota(jnp.int32, (gsz, TM), 0).astype(F32)
    neg = -jnp.inf
    grp, gscore = [], []
    for g in range(N_EXPERT_GROUPS):
        bg = biased[g * gsz:(g + 1) * gsz, :]
        m1 = jnp.max(bg, axis=0, keepdims=True)
        first = jnp.min(jnp.where(bg == m1, sub, float(gsz)), axis=0, keepdims=True)
        m2 = jnp.max(jnp.where(sub == first, neg, bg), axis=0, keepdims=True)
        grp.append(bg)
        gscore.append(m1 + m2)
    masked = []
    for g in range(N_EXPERT_GROUPS):
        beaten = jnp.zeros((1, TM), F32)
        for g2 in range(N_EXPERT_GROUPS):
            if g2 == g:
                continue
            wins = (gscore[g2] > gscore[g]) if g2 > g else (gscore[g2] >= gscore[g])
            beaten = beaten + wins.astype(F32)
        masked.append(jnp.where(beaten < float(TOPK_GROUPS), grp[g], neg))
    masked = jnp.concatenate(masked, axis=0)

    row = lax.broadcasted_iota(jnp.int32, (E, TM), 0).astype(F32)
    sel = jnp.zeros((E, TM), F32)
    firsts, ws = [], []
    for _ in range(TOP_K):
        mx = jnp.max(masked, axis=0, keepdims=True)
        first = jnp.min(jnp.where(masked == mx, row, float(E)), axis=0, keepdims=True)
        oh = row == first
        ws.append(jnp.sum(jnp.where(oh, s, 0.0), axis=0, keepdims=True))
        masked = jnp.where(oh, neg, masked)
        sel = jnp.where(oh, 1.0, sel)
        firsts.append(first)
    wsum = ws[0]
    for w in ws[1:]:
        wsum = wsum + w

    sel_b = sel.astype(BF16)
    base = jnp.concatenate([cnt_acc[...]] * (TM // LANES), axis=1)
    rank_all = _dot(sel_b, tri_ref[...]) + base
    cnt_acc[...] = cnt_acc[...] + _dot(sel_b, ones_ref[...])
    cnt_ref[...] = cnt_acc[...]
    for k in range(TOP_K):
        oh = row == firsts[k]
        idx_k = firsts[k].astype(jnp.int32)
        wt_k = ws[k] / wsum * ROUTED_SCALE
        rank_k = jnp.sum(jnp.where(oh, rank_all, 0.0), axis=0, keepdims=True).astype(jnp.int32)
        for c in range(TM // LANES):
            sl = slice(c * LANES, (c + 1) * LANES)
            idx_ref[c, k:k + 1, :] = idx_k[:, sl]
            wt_ref[c, k:k + 1, :] = wt_k[:, sl]
            rank_ref[c, k:k + 1, :] = rank_k[:, sl]

    g_act = _dot(hb, wsg_ref[...])
    hid = g_act * _sigmoid(g_act) * _dot(hb, wsu_ref[...])
    pre_ref[...] = alpha * h + _dot(hid.astype(BF16), wsd_ref[...])


def _route(h, wr_t_hi, wr_t_lo, router_bias, w_sg, w_su, w_sd, alpha):
    N, D = h.shape
    E = wr_t_hi.shape[0]
    TM = TOKEN_TILE
    Hs = w_sg.shape[1]
    tri = jnp.asarray(np.triu(np.ones((TM, TM), np.float32), k=1), BF16)
    ones = jnp.ones((TM, LANES), BF16)
    kern = functools.partial(_route_kernel, TM=TM, E=E, alpha=alpha)
    kt = pl.BlockSpec((TM // LANES, TOP_K, LANES), lambda t: (t, 0, 0))
    return pl.pallas_call(
        kern,
        grid=(N // TM,),
        in_specs=[pl.BlockSpec((TM, D), lambda t: (t, 0)),
                  _const_spec((E, D)), _const_spec((E, D)), _const_spec((E, 1)),
                  _const_spec((TM, TM)), _const_spec((TM, LANES)),
                  _const_spec((D, Hs)), _const_spec((D, Hs)), _const_spec((Hs, D))],
        out_specs=[kt, kt, kt, _const_spec((E, LANES)), pl.BlockSpec((TM, D), lambda t: (t, 0))],
        out_shape=[jax.ShapeDtypeStruct((N // LANES, TOP_K, LANES), jnp.int32),
                   jax.ShapeDtypeStruct((N // LANES, TOP_K, LANES), F32),
                   jax.ShapeDtypeStruct((N // LANES, TOP_K, LANES), jnp.int32),
                   jax.ShapeDtypeStruct((E, LANES), F32),
                   jax.ShapeDtypeStruct((N, D), F32)],
        scratch_shapes=[pltpu.VMEM((E, LANES), F32)],
        compiler_params=_params("arbitrary"),
        name="route_shared",
    )(h, wr_t_hi, wr_t_lo, router_bias.reshape(E, 1).astype(F32), tri, ones, w_sg, w_su, w_sd)


def _dispatch_kernel(rowstart_ref, cnt_ref, nact_ref, idx_ref, rank_ref, h_ref, xs_hbm,
                     stage, zeros, sem, zsem, *, TM, BLK, NB, E):
    t = pl.program_id(0)
    nt = pl.num_programs(0)
    slot = t % 2

    def zero_copy(row, n, start):
        cp = pltpu.make_async_copy(zeros.at[pl.ds(0, n)], xs_hbm.at[pl.ds(row, n)], zsem)
        cp.start() if start else cp.wait()

    def stage_copy(s):
        return pltpu.make_async_copy(stage.at[s], xs_hbm.at[pl.ds(0, TM)], sem.at[s])

    def wait_slot(s):
        for _ in range(TOP_K):
            stage_copy(s).wait()

    @pl.when(t == 0)
    def _():
        zeros[...] = jnp.zeros(zeros.shape, F32)
        nact = nact_ref[0]

        def tail(e, c, *, start):
            first = rowstart_ref[e] + cnt_ref[e]
            end = rowstart_ref[e] + (cnt_ref[e] + BLK - 1) // BLK * BLK
            mid = jnp.minimum((first + SUBLANES - 1) // SUBLANES * SUBLANES, end)

            def one(r, c2):
                zero_copy(r, 1, start)
                return c2

            def eight(i, c2):
                zero_copy(pl.multiple_of(mid + i * SUBLANES, SUBLANES), SUBLANES, start)
                return c2

            lax.fori_loop(first, mid, one, 0)
            lax.fori_loop(0, (end - mid) // SUBLANES, eight, 0)
            return c

        def idle(b, c, *, start):
            zero_copy(pl.multiple_of(b * BLK, BLK), BLK, start)
            return c

        for start in (True, False):
            lax.fori_loop(0, E, functools.partial(tail, start=start), 0)
            lax.fori_loop(nact, NB, functools.partial(idle, start=start), 0)

    @pl.when(t >= 2)
    def _():
        wait_slot(slot)

    stage[slot] = h_ref[...]
    for c in range(TM // LANES):
        def row_body(r, carry, c=c):
            for k in range(TOP_K):
                dest = rowstart_ref[idx_ref[c, k, r]] + rank_ref[c, k, r]
                pltpu.make_async_copy(stage.at[slot, pl.ds(c * LANES + r, 1)],
                                      xs_hbm.at[pl.ds(dest, 1)], sem.at[slot]).start()
            return carry

        lax.fori_loop(0, LANES, row_body, 0)

    @pl.when(t == nt - 1)
    def _():
        @pl.when(t >= 1)
        def _():
            wait_slot(1 - slot)

        wait_slot(slot)


def _dispatch(h, idx, rank, rowstart, counts, nact, NB):
    N, D = h.shape
    E = rowstart.shape[0]
    TM = MERGE_TILE
    BLK = EXPERT_BLOCK
    sb = TM // LANES
    kern = functools.partial(_dispatch_kernel, TM=TM, BLK=BLK, NB=NB, E=E)
    smem = lambda: pl.BlockSpec((sb, TOP_K, LANES), lambda t, *_: (t, 0, 0),
                                memory_space=pltpu.SMEM)
    grid_spec = pltpu.PrefetchScalarGridSpec(
        num_scalar_prefetch=3,
        grid=(N // TM,),
        in_specs=[smem(), smem(), pl.BlockSpec((TM, D), lambda t, *_: (t, 0))],
        out_specs=pl.BlockSpec(memory_space=pl.ANY),
        scratch_shapes=[pltpu.VMEM((2, TM, D), F32), pltpu.VMEM((BLK, D), F32),
                        pltpu.SemaphoreType.DMA((2,)), pltpu.SemaphoreType.DMA(())],
    )
    return pl.pallas_call(
        kern,
        grid_spec=grid_spec,
        out_shape=jax.ShapeDtypeStruct((NB * BLK, D), F32),
        compiler_params=_params("arbitrary"),
        name="dispatch",
    )(rowstart, counts, nact, idx, rank, h)


def _expert_kernel(blk_e_ref, nact_ref, x_ref, wg_ref, wu_ref, wd_ref, y_ref, wg_s, wu_s, wd_s):
    b = pl.program_id(0)
    nact = nact_ref[0]

    @pl.when(b < nact)
    def _():
        e = blk_e_ref[b]
        e_prev = blk_e_ref[jnp.maximum(b - 1, 0)]

        @pl.when(jnp.logical_or(b == 0, e != e_prev))
        def _():
            wg_s[...] = wg_ref[...].astype(BF16)
            wu_s[...] = wu_ref[...].astype(BF16)
            wd_s[...] = wd_ref[...].astype(BF16)

        x = x_ref[...].astype(BF16)
        g_act = _dot(x, wg_s[...])
        hid = g_act * _sigmoid(g_act) * _dot(x, wu_s[...])
        y_ref[...] = _dot(hid.astype(BF16), wd_s[...])

    @pl.when(b >= nact)
    def _():
        y_ref[...] = jnp.zeros(y_ref.shape, F32)


def _experts(x_sorted, blk_e, nact, NB, w_eg, w_eu, w_ed):
    D = x_sorted.shape[1]
    E, _, He = w_eg.shape
    BLK = EXPERT_BLOCK
    live = lambda b, na: jnp.minimum(b, na[0] - 1)
    grid_spec = pltpu.PrefetchScalarGridSpec(
        num_scalar_prefetch=2,
        grid=(NB,),
        in_specs=[pl.BlockSpec((BLK, D), lambda b, be, na: (live(b, na), 0)),
                  pl.BlockSpec((None, D, He), lambda b, be, na: (be[live(b, na)], 0, 0)),
                  pl.BlockSpec((None, D, He), lambda b, be, na: (be[live(b, na)], 0, 0)),
                  pl.BlockSpec((None, He, D), lambda b, be, na: (be[live(b, na)], 0, 0))],
        out_specs=pl.BlockSpec((BLK, D), lambda b, be, na: (b, 0)),
        scratch_shapes=[pltpu.VMEM((D, He), BF16), pltpu.VMEM((D, He), BF16),
                        pltpu.VMEM((He, D), BF16)],
    )
    return pl.pallas_call(
        _expert_kernel,
        grid_spec=grid_spec,
        out_shape=jax.ShapeDtypeStruct((NB * BLK, D), F32),
        compiler_params=_params("arbitrary"),
        name="routed_experts",
    )(blk_e, nact, x_sorted, w_eg, w_eu, w_ed)


def _combine_kernel(rowstart_ref, idx_ref, rank_ref, wt_ref, pre_ref, lng_ref, lnb_ref, y_hbm,
                    o_ref, buf, sem, *, TM):
    t = pl.program_id(0)
    nt = pl.num_programs(0) - 1
    slot = t % 2

    @pl.when(t < nt)
    def _():
        for k in range(TOP_K):
            for r in range(TM):
                src = rowstart_ref[idx_ref[0, k, r]] + rank_ref[0, k, r]
                pltpu.make_async_copy(y_hbm.at[pl.ds(src, 1)],
                                      buf.at[slot, pl.ds(k * TM + r, 1)], sem.at[slot]).start()

    @pl.when(t >= 1)
    def _():
        prev = 1 - slot
        for k in range(TOP_K):
            pltpu.make_async_copy(y_hbm.at[pl.ds(0, TM)], buf.at[prev, pl.ds(k * TM, TM)],
                                  sem.at[prev]).wait()
        w = jnp.concatenate([wt_ref[0], jnp.zeros((TM - TOP_K, TM), F32)], axis=0).T
        z = pre_ref[...]
        for k in range(TOP_K):
            z = z + w[:, k:k + 1] * buf[prev, pl.ds(k * TM, TM), :]
        o_ref[...] = _layer_norm(z, lng_ref[...], lnb_ref[...])


def _combine(y_sorted, idx, rank, wts, rowstart, pre, ln_g, ln_b):
    N, D = pre.shape
    TM = LANES
    nt = N // TM
    kern = functools.partial(_combine_kernel, TM=TM)
    row = lambda a: a.reshape(1, -1).astype(F32)
    ahead = lambda t, *_: (jnp.minimum(t, nt - 1), 0, 0)
    behind3 = lambda t, *_: (jnp.maximum(t - 1, 0), 0, 0)
    behind = lambda t, *_: (jnp.maximum(t - 1, 0), 0)
    grid_spec = pltpu.PrefetchScalarGridSpec(
        num_scalar_prefetch=1,
        grid=(nt + 1,),
        in_specs=[pl.BlockSpec((1, TOP_K, TM), ahead, memory_space=pltpu.SMEM),
                  pl.BlockSpec((1, TOP_K, TM), ahead, memory_space=pltpu.SMEM),
                  pl.BlockSpec((1, TOP_K, TM), behind3),
                  pl.BlockSpec((TM, D), behind),
                  pl.BlockSpec((1, D), lambda t, *_: (0, 0)),
                  pl.BlockSpec((1, D), lambda t, *_: (0, 0)),
                  pl.BlockSpec(memory_space=pl.ANY)],
        out_specs=pl.BlockSpec((TM, D), behind),
        scratch_shapes=[pltpu.VMEM((2, TOP_K * TM, D), F32), pltpu.SemaphoreType.DMA((2,))],
    )
    return pl.pallas_call(
        kern,
        grid_spec=grid_spec,
        out_shape=jax.ShapeDtypeStruct((N, D), F32),
        compiler_params=_params("arbitrary"),
        name="combine",
    )(rowstart, idx, rank, wts, pre, row(ln_g), row(ln_b), y_sorted)


def _dispatch_meta(cnt, N, E):
    BLK = EXPERT_BLOCK
    NB = (N * TOP_K + E * (BLK - 1)) // BLK
    counts = cnt[:, 0].astype(jnp.int32)
    nblk_e = (counts + BLK - 1) // BLK
    bend = jnp.cumsum(nblk_e)
    rowstart = (bend - nblk_e) * BLK
    blk_of = jnp.sum(bend[None, :] <= jnp.arange(NB, dtype=jnp.int32)[:, None], axis=1)
    blk_e = jnp.minimum(blk_of, E - 1).astype(jnp.int32)
    nact = bend[-1:].astype(jnp.int32)
    return counts, rowstart.astype(jnp.int32), blk_e, nact, NB


def kernel(x, rel_bias_table, w_in, b_gate, w_pool_grp, pool_scale, w_branch_pool, lambda_q1,
           lambda_k1, lambda_q2, lambda_k2, subln_gain, w_branch_attn, w_out, ln1_g, ln1_b,
           w_router, router_bias, w_exp_gate, w_exp_up, w_exp_down, w_sh_gate, w_sh_up,
           w_sh_down, ln2_g, ln2_b):
    B, S, D = x.shape
    L = w_in.shape[0]
    E = w_router.shape[2]
    P = pool_scale.shape[1]
    A = w_branch_attn.shape[1]
    N = B * S
    alpha = (2 * L) ** 0.25
    assert S % TOKEN_TILE == 0 and S % MERGE_TILE == 0
    assert A == N_DIFF_HEADS * 2 * DIFF_HEAD_DIM and TOKEN_TILE >= REL_MAX_DIST
    band, far_c = _bias_tables(rel_bias_table, S, TOKEN_TILE)

    h = x
    for i in range(L):
        w_uqkv = w_in[i][:, :P + 3 * A].astype(BF16)
        w_gate = w_in[i][:, P + 3 * A:].astype(BF16)
        u, qk, vt = _inproj(h, w_uqkv, P, A)
        o = _attention(qk, vt, band, far_c, lambda_q1[i], lambda_k1[i], lambda_q2[i],
                       lambda_k2[i], subln_gain[i], _lambda_init(i))
        h1 = _merge(h, u, o, w_gate, b_gate[i], w_pool_grp[i].astype(BF16), pool_scale[i],
                    w_branch_pool[i].astype(BF16), w_branch_attn[i].astype(BF16),
                    w_out[i].astype(BF16), ln1_g[i], ln1_b[i], alpha)
        t = h1.reshape(N, D)
        wr_hi, wr_lo = _split_bf16(w_router[i].astype(F32).T)
        idx, wts, rank, cnt, pre = _route(t, wr_hi, wr_lo, router_bias[i],
                                          w_sh_gate[i].astype(BF16), w_sh_up[i].astype(BF16),
                                          w_sh_down[i].astype(BF16), alpha)
        counts, rowstart, blk_e, nact, NB = _dispatch_meta(cnt, N, E)
        x_sorted = _dispatch(t, idx, rank, rowstart, counts, nact, NB)
        y_sorted = _experts(x_sorted, blk_e, nact, NB, w_exp_gate[i], w_exp_up[i], w_exp_down[i])
        h = _combine(y_sorted, idx, rank, wts, rowstart, pre, ln2_g[i], ln2_b[i]).reshape(B, S, D)
    return h
```

```python
import functools
import math

import numpy as np
import jax
import jax.numpy as jnp
from jax import lax
from jax.experimental import pallas as pl
from jax.experimental.pallas import tpu as pltpu

F32 = jnp.float32
BF16 = jnp.bfloat16

POOL_WINDOWS = (2, 4, 8, 16)
N_DIFF_HEADS = 4
DIFF_HEAD_DIM = 64
REL_BUCKETS = 32
REL_MAX_DIST = 128
TOP_K = 8
N_EXPERT_GROUPS = 8
TOPK_GROUPS = 4
ROUTED_SCALE = 2.5
LN_EPS = 1e-5
LOG2E = math.log2(math.e)

LANES = 128
SUBLANES = 8
VMEM_LIMIT_BYTES = 56 * 1024 * 1024

TOKEN_TILE = 512
MERGE_TILE = 256
EXPERT_BLOCK = 256

_NT = (((1,), (1,)), ((), ()))


def _lambda_init(layer_idx):
    return 0.8 - 0.6 * math.exp(-0.3 * layer_idx)


def _dot(a, b):
    return jnp.dot(a, b, preferred_element_type=F32)


def _split_bf16(a):
    hi = a.astype(BF16)
    lo = (a - hi.astype(F32)).astype(BF16)
    return hi, lo


def _sigmoid(z):
    return 1.0 / (1.0 + jnp.exp(-z))


def _layer_norm(z, g, b):
    mu = jnp.mean(z, axis=-1, keepdims=True)
    zc = z - mu
    var = jnp.mean(zc * zc, axis=-1, keepdims=True)
    return zc * lax.rsqrt(var + LN_EPS) * g + b


def _params(*sem):
    return pltpu.CompilerParams(dimension_semantics=sem, vmem_limit_bytes=VMEM_LIMIT_BYTES)


def _const_spec(shape):
    nd = len(shape)
    return pl.BlockSpec(shape, lambda *_: (0,) * nd)


def _to_row_tiles(ref, x):
    rows = x.shape[0]
    for s in range(SUBLANES):
        ref[pl.ds(s, rows, stride=SUBLANES), :] = x[:, s * LANES:(s + 1) * LANES]


def _from_row_tiles(ref, start, rows):
    return jnp.concatenate([ref[pl.ds(start + s, rows, stride=SUBLANES), :]
                            for s in range(SUBLANES)], axis=1)


def _inproj_kernel(x_ref, w_ref, u_ref, qk_ref, vt_ref, *, pool_w, attn_w, q_scale):
    xb = x_ref[...].astype(BF16)
    p = _dot(xb, w_ref[...])
    u_ref[...] = p[:, :pool_w]
    q = p[:, pool_w:pool_w + attn_w] * q_scale
    k = p[:, pool_w + attn_w:pool_w + 2 * attn_w]
    qk_ref[:, :attn_w] = q.astype(BF16)
    qk_ref[:, attn_w:] = k.astype(BF16)
    v = p[:, pool_w + 2 * attn_w:pool_w + 3 * attn_w]
    vt_ref[...] = v.T.astype(BF16)


def _inproj(x, w_uqkv, pool_w, attn_w):
    B, S, D = x.shape
    T = TOKEN_TILE
    ns = S // T
    width = w_uqkv.shape[1]
    kern = functools.partial(_inproj_kernel, pool_w=pool_w, attn_w=attn_w,
                             q_scale=DIFF_HEAD_DIM ** -0.5 * LOG2E)
    return pl.pallas_call(
        kern,
        grid=(B, ns),
        in_specs=[pl.BlockSpec((None, T, D), lambda b, s: (b, s, 0)),
                  _const_spec((D, width))],
        out_specs=[pl.BlockSpec((None, T, pool_w), lambda b, s: (b, s, 0)),
                   pl.BlockSpec((None, T, 2 * attn_w), lambda b, s: (b, s, 0)),
                   pl.BlockSpec((None, None, attn_w, T), lambda b, s: (b, s, 0, 0))],
        out_shape=[jax.ShapeDtypeStruct((B, S, pool_w), F32),
                   jax.ShapeDtypeStruct((B, S, 2 * attn_w), BF16),
                   jax.ShapeDtypeStruct((B, ns, attn_w, T), BF16)],
        compiler_params=_params("parallel", "parallel"),
        name="inproj",
    )(x, w_uqkv)


def _t5_bucket_np(rel):
    half = REL_BUCKETS // 2
    max_exact = half // 2
    ret = np.where(rel > 0, half, 0)
    n = np.abs(rel)
    nf = np.maximum(n, 1).astype(np.float32)
    large = max_exact + (np.log(nf / max_exact) / math.log(REL_MAX_DIST / max_exact)
                         * (half - max_exact)).astype(np.int32)
    large = np.minimum(large, half - 1)
    return ret + np.where(n < max_exact, n, large)


def _t5_bucket(rel):
    half = REL_BUCKETS // 2
    max_exact = half // 2
    ret = jnp.where(rel > 0, half, 0)
    n = jnp.abs(rel)
    nf = jnp.maximum(n, 1).astype(F32)
    large = max_exact + (jnp.log(nf / max_exact) / math.log(REL_MAX_DIST / max_exact)
                         * (half - max_exact)).astype(jnp.int32)
    large = jnp.minimum(large, half - 1)
    return ret + jnp.where(n < max_exact, n, large)


def _bias_tables(table, S, T):
    far = np.arange(T + 1, S)
    if far.size:
        assert np.all(_t5_bucket_np(-far) == _t5_bucket_np(-far[-1]))
        assert np.all(_t5_bucket_np(far) == _t5_bucket_np(far[-1]))
    rel = jnp.arange(-(S - 1), S, dtype=jnp.int32)
    dist_bias = table[_t5_bucket(rel)].astype(F32).T
    pad = jnp.pad(dist_bias, ((0, 0), (2 * T, 2 * T)), mode='edge')
    diag = jnp.stack([lax.slice_in_dim(pad, (d - 1) * T - (T - 1) + S - 1 + 2 * T,
                                       (d - 1) * T + T + S - 1 + 2 * T, axis=1)
                      for d in range(3)], axis=1)
    w = jnp.concatenate([diag[..., ::-1], diag[..., :1]], axis=-1)
    flat = jnp.tile(w, (1, 1, T))[..., :T * (2 * T - 1)]
    band = flat.reshape(w.shape[0], 3, T, 2 * T - 1)[..., T - 1:]
    far_c = jnp.stack([dist_bias[:, 0], dist_bias[:, 2 * S - 2]], axis=1)
    return band * LOG2E, far_c * LOG2E


def _attn_kernel(far_ref, q_ref, k_ref, vt_ref, band_ref, lq1_ref, lk1_ref, lq2_ref, lk2_ref,
                 gain_ref, o_ref, qz_ref, s00, s01, s10, s11, e00, e01, e10, e11, acc0, acc1,
                 mt_ref, al_ref, m_ref, l_ref, *, T, nk, lam_init):
    h = pl.program_id(1)
    i = pl.program_id(2)
    s_buf = ((s00, s01), (s10, s11))
    e_buf = ((e00, e01), (e10, e11))
    acc = (acc0, acc1)
    c_left = far_ref[h, 0]
    c_right = far_ref[h, 1]

    q = q_ref[...]
    lane = lax.broadcasted_iota(jnp.int32, q.shape, 1)
    zero = jnp.zeros_like(q)
    qz_ref[0] = jnp.where(lane < DIFF_HEAD_DIM, q, zero)
    qz_ref[1] = jnp.where(lane >= DIFF_HEAD_DIM, q, zero)
    m_ref[...] = jnp.full(m_ref.shape, -jnp.inf, F32)
    l_ref[...] = jnp.zeros(l_ref.shape, F32)
    acc0[...] = jnp.zeros(acc0.shape, F32)
    acc1[...] = jnp.zeros(acc1.shape, F32)

    has_prev = i >= 1
    has_next = i <= nk - 2
    n_left = jnp.maximum(i - 1, 0) - jnp.where(has_next, 0, 1)
    right0 = i + 2 + jnp.where(has_prev, 0, 1)

    def tile_of(p):
        if isinstance(p, int) and p == 0:
            return i, band_ref[1], None
        if isinstance(p, int) and p == 1:
            return (jnp.where(has_prev, i - 1, i + 2),
                    jnp.where(has_prev, band_ref[0], c_right), None)
        if isinstance(p, int) and p == 2:
            return (jnp.where(has_next, i + 1, i - 2),
                    jnp.where(has_next, band_ref[2], c_left), None)
        f = p - 3
        is_left = f < n_left
        return (jnp.where(is_left, f, f - n_left + right0), None,
                jnp.where(is_left, c_left, c_right))

    def stage_a(p, x):
        j, bias, const = tile_of(p)
        kt = k_ref[pl.ds(pl.multiple_of(j * T, T), T), :]
        for c in range(2):
            s = lax.dot_general(kt, qz_ref[c], _NT, preferred_element_type=F32)
            if bias is not None:
                s = s + bias
            s_buf[x][c][...] = s
            mt = jnp.max(s, axis=0, keepdims=True)
            mt_ref[2 * x + c] = mt if const is None else mt + const

    def stage_b(p, x):
        _, _, const = tile_of(p)
        for c in range(2):
            m_old = m_ref[c]
            m_new = jnp.maximum(m_old, mt_ref[2 * x + c])
            alpha = jnp.exp2(m_old - m_new)
            shift = m_new if const is None else m_new - const
            e = jnp.exp2(s_buf[x][c][...] - shift)
            l_ref[c] = alpha * l_ref[c] + jnp.sum(e, axis=0, keepdims=True)
            e_buf[x][c][...] = e.astype(BF16)
            al_ref[2 * x + c] = alpha
            m_ref[c] = m_new

    def stage_c(p, x):
        j, _, _ = tile_of(p)
        vt = vt_ref[j]
        for c in range(2):
            acc[c][...] = al_ref[2 * x + c] * acc[c][...] + _dot(vt, e_buf[x][c][...])

    def iteration(n, parity):
        static = isinstance(n, int)
        if not static or n < nk:
            stage_a(n, parity)
        if not static or 1 <= n <= nk:
            stage_b(n - 1, 1 - parity)
        if not static or 2 <= n <= nk + 1:
            stage_c(n - 2, parity)

    head = 5
    for n in range(head):
        iteration(n, n % 2)

    def pair(it, carry):
        n = head + 2 * it
        iteration(n, head % 2)
        iteration(n + 1, 1 - head % 2)
        return carry

    n_pairs = (nk - 1 - head) // 2
    lax.fori_loop(0, n_pairs, pair, 0)
    for n in range(head + 2 * n_pairs, nk + 2):
        iteration(n, n % 2)

    lam = (jnp.exp(jnp.sum(lq1_ref[...] * lk1_ref[...], keepdims=True))
           - jnp.exp(jnp.sum(lq2_ref[...] * lk2_ref[...], keepdims=True)) + lam_init)
    o = acc0[...] / l_ref[0] - lam * (acc1[...] / l_ref[1])
    ms = jnp.mean(o * o, axis=0, keepdims=True)
    y = o * lax.rsqrt(ms + LN_EPS) * gain_ref[...] * (1.0 - lam_init)
    o_ref[...] = y.T.astype(BF16)


def _attention(qk, vt, band, far_c, lq1, lk1, lq2, lk2, gain, lam_init):
    B, S, _ = qk.shape
    T = TOKEN_TILE
    nk = S // T
    assert nk >= 6, "the pipeline prologue assumes at least three far key tiles"
    H = N_DIFF_HEADS
    hw = 2 * DIFF_HEAD_DIM
    kern = functools.partial(_attn_kernel, T=T, nk=nk, lam_init=lam_init)
    vec = lambda a: a.reshape(1, DIFF_HEAD_DIM).astype(F32)
    return pl.pallas_call(
        kern,
        grid=(B, H, nk),
        in_specs=[pl.BlockSpec(memory_space=pltpu.SMEM),
                  pl.BlockSpec((None, T, hw), lambda b, h, i: (b, i, h)),
                  pl.BlockSpec((None, S, hw), lambda b, h, i: (b, 0, H + h)),
                  pl.BlockSpec((None, nk, hw, T), lambda b, h, i: (b, 0, h, 0)),
                  pl.BlockSpec((None, 3, T, T), lambda b, h, i: (h, 0, 0, 0)),
                  _const_spec((1, DIFF_HEAD_DIM)), _const_spec((1, DIFF_HEAD_DIM)),
                  _const_spec((1, DIFF_HEAD_DIM)), _const_spec((1, DIFF_HEAD_DIM)),
                  _const_spec((hw, 1))],
        out_specs=pl.BlockSpec((None, T, hw), lambda b, h, i: (b, i, h)),
        out_shape=jax.ShapeDtypeStruct((B, S, H * hw), BF16),
        scratch_shapes=([pltpu.VMEM((2, T, hw), BF16)]
                        + [pltpu.VMEM((T, T), F32)] * 4 + [pltpu.VMEM((T, T), BF16)] * 4
                        + [pltpu.VMEM((hw, T), F32)] * 2
                        + [pltpu.VMEM((4, 1, T), F32), pltpu.VMEM((4, 1, T), F32),
                           pltpu.VMEM((2, 1, T), F32), pltpu.VMEM((2, 1, T), F32)]),
        compiler_params=_params("parallel", "parallel", "parallel"),
        name="diff_attn",
    )(far_c, qk, qk, vt, band, vec(lq1), vec(lk1), vec(lq2), vec(lk2),
      gain.reshape(hw, 1).astype(F32))


def _pool_matrices(TM):
    r = np.arange(TM)[:, None]
    c = np.arange(TM)[None, :]
    hpos = np.concatenate([np.arange(-SUBLANES, 0), np.arange(TM, TM + SUBLANES),
                           np.full(LANES - 2 * SUBLANES, 10 ** 6)])[None, :]
    main, halo = [], []
    for w in POOL_WINDOWS:
        lo, hi = r - w // 2, r + w - w // 2
        main.append((c >= lo) & (c < hi))
        halo.append((hpos >= lo) & (hpos < hi))
    return (jnp.asarray(np.stack(main), BF16), jnp.asarray(np.stack(halo), BF16))


def _merge_kernel(x_ref, u_ref, up_ref, un_ref, o_ref, wg_ref, bg_ref, am_ref, ah_ref, wgrp_ref,
                  ps_ref, wbp_ref, wba_ref, wo_ref, lng_ref, lnb_ref, h_ref, ht_ref, *, TM, S, alpha):
    s_idx = pl.program_id(1)
    ns = pl.num_programs(1)
    x = x_ref[...]
    D = x.shape[1]
    xb = x.astype(BF16)
    gates = _sigmoid(_dot(xb, wg_ref[...]) + bg_ref[...])

    u = u_ref[...]
    P = u.shape[1]
    prev = jnp.where(s_idx > 0, up_ref[...], 0.0)
    nxt = jnp.where(s_idx < ns - 1, un_ref[...], 0.0)
    halo = jnp.concatenate([prev, nxt, jnp.zeros((LANES - 2 * SUBLANES, P), F32)], axis=0)
    u_hi, u_lo = _split_bf16(u)
    h_hi, h_lo = _split_bf16(halo)
    pos = s_idx * TM + lax.broadcasted_iota(jnp.int32, (TM, 1), 0)
    G = P // len(POOL_WINDOWS)
    mixed = []
    for g, w in enumerate(POOL_WINDOWS):
        sl = slice(g * G, (g + 1) * G)
        win2 = (_dot(am_ref[g], jnp.concatenate([u_hi[:, sl], u_lo[:, sl]], axis=1))
                + _dot(ah_ref[g], jnp.concatenate([h_hi[:, sl], h_lo[:, sl]], axis=1)))
        win = win2[:, :G] + win2[:, G:]
        cnt = (jnp.minimum(pos + (w - w // 2), S) - jnp.maximum(pos - w // 2, 0)).astype(F32)
        pooled = win / cnt - u[:, sl]
        mixed.append(_dot(pooled.astype(BF16), wgrp_ref[g]))
    mix = jnp.concatenate(mixed, axis=1) * ps_ref[...]
    y_pool = _dot(mix.astype(BF16), wbp_ref[...])
    y_attn = _dot(o_ref[...], wba_ref[...])
    merged = gates[:, :D] * y_pool + gates[:, D:] * y_attn
    y = _dot(merged.astype(BF16), wo_ref[...])
    h = _layer_norm(alpha * x + y, lng_ref[...], lnb_ref[...])
    h_ref[...] = h
    _to_row_tiles(ht_ref, h)


def _merge(x, u, o, w_gate, b_gate, w_grp, pool_scale, w_bp, w_ba, w_out, ln_g, ln_b, alpha):
    B, S, D = x.shape
    P = u.shape[2]
    A = o.shape[2]
    TM = MERGE_TILE
    ns = S // TM
    hb = TM // SUBLANES
    am, ah = _pool_matrices(TM)
    nw = len(POOL_WINDOWS)
    G = P // nw
    row = lambda a: a.reshape(1, -1).astype(F32)
    kern = functools.partial(_merge_kernel, TM=TM, S=S, alpha=alpha)
    return pl.pallas_call(
        kern,
        grid=(B, ns),
        in_specs=[pl.BlockSpec((None, TM, D), lambda b, s: (b, s, 0)),
                  pl.BlockSpec((None, TM, P), lambda b, s: (b, s, 0)),
                  pl.BlockSpec((None, SUBLANES, P),
                               lambda b, s: (b, jnp.maximum(s * hb - 1, 0), 0)),
                  pl.BlockSpec((None, SUBLANES, P),
                               lambda b, s: (b, jnp.minimum((s + 1) * hb, S // SUBLANES - 1), 0)),
                  pl.BlockSpec((None, TM, A), lambda b, s: (b, s, 0)),
                  _const_spec((D, 2 * D)), _const_spec((1, 2 * D)),
                  _const_spec((nw, TM, TM)), _const_spec((nw, TM, LANES)),
                  _const_spec((nw, G, G)), _const_spec((1, P)),
                  _const_spec((P, D)), _const_spec((A, D)), _const_spec((D, D)),
                  _const_spec((1, D)), _const_spec((1, D))],
        out_specs=[pl.BlockSpec((None, TM, D), lambda b, s: (b, s, 0)),
                   pl.BlockSpec((None, TM * SUBLANES, LANES), lambda b, s: (b, s, 0))],
        out_shape=[jax.ShapeDtypeStruct((B, S, D), F32),
                   jax.ShapeDtypeStruct((B, S * SUBLANES, LANES), F32)],
        compiler_params=_params("parallel", "parallel"),
        name="merge",
    )(x, u, u, u, o, w_gate, row(b_gate), am, ah, w_grp, row(pool_scale), w_bp, w_ba, w_out,
      row(ln_g), row(ln_b))


def _route_kernel(h_ref, wrh_ref, wrl_ref, rb_ref, tri_ref, ones_ref, wsg_ref, wsu_ref, wsd_ref,
                  idx_ref, wt_ref, rank_ref, cnt_ref, pre_ref, cnt_acc, *, TM, E, alpha):
    t = pl.program_id(0)

    @pl.when(t == 0)
    def _():
        cnt_acc[...] = jnp.zeros(cnt_acc.shape, F32)

    h = h_ref[...]
    hb, hl = _split_bf16(h)
    logits = (lax.dot_general(wrh_ref[...], hb, _NT, preferred_element_type=F32)
              + lax.dot_general(wrh_ref[...], hl, _NT, preferred_element_type=F32)
              + lax.dot_general(wrl_ref[...], hb, _NT, preferred_element_type=F32))
    s = _sigmoid(logits)
    biased = s + rb_ref[...]
    gsz = E // N_EXPERT_GROUPS
    sub = lax.broadcasted_iota(jnp.int32, (gsz, TM), 0).astype(F32)
    neg = -jnp.inf
    grp, gscore = [], []
    for g in range(N_EXPERT_GROUPS):
        bg = biased[g * gsz:(g + 1) * gsz, :]
        m1 = jnp.max(bg, axis=0, keepdims=True)
        first = jnp.min(jnp.where(bg == m1, sub, float(gsz)), axis=0, keepdims=True)
        m2 = jnp.max(jnp.where(sub == first, neg, bg), axis=0, keepdims=True)
        grp.append(bg)
        gscore.append(m1 + m2)
    masked = []
    for g in range(N_EXPERT_GROUPS):
        beaten = jnp.zeros((1, TM), F32)
        for g2 in range(N_EXPERT_GROUPS):
            if g2 == g:
                continue
            wins = (gscore[g2] > gscore[g]) if g2 > g else (gscore[g2] >= gscore[g])
            beaten = beaten + wins.astype(F32)
        masked.append(jnp.where(beaten < float(TOPK_GROUPS), grp[g], neg))
    masked = jnp.concatenate(masked, axis=0)

    row = lax.broadcasted_iota(jnp.int32, (E, TM), 0).astype(F32)
    sel = jnp.zeros((E, TM), F32)
    firsts, ws = [], []
    for _ in range(TOP_K):
        mx = jnp.max(masked, axis=0, keepdims=True)
        first = jnp.min(jnp.where(masked == mx, row, float(E)), axis=0, keepdims=True)
        oh = row == first
        ws.append(jnp.sum(jnp.where(oh, s, 0.0), axis=0, keepdims=True))
        masked = jnp.where(oh, neg, masked)
        sel = jnp.where(oh, 1.0, sel)
        firsts.append(first)
    wsum = ws[0]
    for w in ws[1:]:
        wsum = wsum + w

    sel_b = sel.astype(BF16)
    base = jnp.concatenate([cnt_acc[...]] * (TM // LANES), axis=1)
    rank_all = _dot(sel_b, tri_ref[...]) + base
    cnt_acc[...] = cnt_acc[...] + _dot(sel_b, ones_ref[...])
    cnt_ref[...] = cnt_acc[...]
    for k in range(TOP_K):
        oh = row == firsts[k]
        idx_k = firsts[k].astype(jnp.int32)
        wt_k = ws[k] / wsum * ROUTED_SCALE
        rank_k = jnp.sum(jnp.where(oh, rank_all, 0.0), axis=0, keepdims=True).astype(jnp.int32)
        for c in range(TM // LANES):
            sl = slice(c * LANES, (c + 1) * LANES)
            idx_ref[c, k:k + 1, :] = idx_k[:, sl]
            wt_ref[c, k:k + 1, :] = wt_k[:, sl]
            rank_ref[c, k:k + 1, :] = rank_k[:, sl]

    g_act = _dot(hb, wsg_ref[...])
    hid = g_act * _sigmoid(g_act) * _dot(hb, wsu_ref[...])
    pre_ref[...] = alpha * h + _dot(hid.astype(BF16), wsd_ref[...])


def _route(h, wr_t_hi, wr_t_lo, router_bias, w_sg, w_su, w_sd, alpha):
    N, D = h.shape
    E = wr_t_hi.shape[0]
    TM = TOKEN_TILE
    Hs = w_sg.shape[1]
    tri = jnp.asarray(np.triu(np.ones((TM, TM), np.float32), k=1), BF16)
    ones = jnp.ones((TM, LANES), BF16)
    kern = functools.partial(_route_kernel, TM=TM, E=E, alpha=alpha)
    kt = pl.BlockSpec((TM // LANES, TOP_K, LANES), lambda t: (t, 0, 0))
    return pl.pallas_call(
        kern,
        grid=(N // TM,),
        in_specs=[pl.BlockSpec((TM, D), lambda t: (t, 0)),
                  _const_spec((E, D)), _const_spec((E, D)), _const_spec((E, 1)),
                  _const_spec((TM, TM)), _const_spec((TM, LANES)),
                  _const_spec((D, Hs)), _const_spec((D, Hs)), _const_spec((Hs, D))],
        out_specs=[kt, kt, kt, _const_spec((E, LANES)), pl.BlockSpec((TM, D), lambda t: (t, 0))],
        out_shape=[jax.ShapeDtypeStruct((N // LANES, TOP_K, LANES), jnp.int32),
                   jax.ShapeDtypeStruct((N // LANES, TOP_K, LANES), F32),
                   jax.ShapeDtypeStruct((N // LANES, TOP_K, LANES), jnp.int32),
                   jax.ShapeDtypeStruct((E, LANES), F32),
                   jax.ShapeDtypeStruct((N, D), F32)],
        scratch_shapes=[pltpu.VMEM((E, LANES), F32)],
        compiler_params=_params("arbitrary"),
        name="route_shared",
    )(h, wr_t_hi, wr_t_lo, router_bias.reshape(E, 1).astype(F32), tri, ones, w_sg, w_su, w_sd)


def _slots_kernel(idx_ref, rank_ref, rowstart_ref, dest_ref, *, E):
    expert = lax.broadcasted_iota(jnp.int32, (E, LANES), 0)
    rowstart = rowstart_ref[...]
    for c in range(idx_ref.shape[0]):
        idx = idx_ref[c]
        rank = rank_ref[c]
        for k in range(TOP_K):
            base = jnp.sum(jnp.where(expert == idx[k:k + 1, :], rowstart, 0.0), axis=0,
                           keepdims=True)
            dest_ref[c, k:k + 1, :] = base.astype(jnp.int32) + rank[k:k + 1, :]


def _slots(idx, rank, rowstart):
    nb = idx.shape[0]
    E = rowstart.shape[0]
    step = SUBLANES
    spec = pl.BlockSpec((step, TOP_K, LANES), lambda t: (t, 0, 0))
    return pl.pallas_call(
        functools.partial(_slots_kernel, E=E),
        grid=(nb // step,),
        in_specs=[spec, spec, _const_spec((E, 1))],
        out_specs=spec,
        out_shape=jax.ShapeDtypeStruct(idx.shape, jnp.int32),
        compiler_params=_params("parallel"),
        name="slots",
    )(idx, rank, rowstart.astype(F32).reshape(E, 1))


def _dispatch_kernel(rowstart_ref, cnt_ref, nact_ref, dest_ref, h_ref, xs_hbm,
                     stage, zeros, sem, zsem, *, TM, BLK, NB, E):
    t = pl.program_id(0)
    nt = pl.num_programs(0)
    slot = t % 2
    R = SUBLANES

    def zero_copy(first, n, start):
        cp = pltpu.make_async_copy(zeros.at[pl.ds(0, n * R)],
                                   xs_hbm.at[pl.ds(pl.multiple_of(first * R, R), n * R)], zsem)
        cp.start() if start else cp.wait()

    def wait_slot(s):
        for _ in range(TOP_K):
            pltpu.make_async_copy(stage.at[s], xs_hbm.at[pl.ds(0, TM * R)], sem.at[s]).wait()

    @pl.when(t == 0)
    def _():
        zeros[...] = jnp.zeros(zeros.shape, F32)
        nact = nact_ref[0]

        def tail(e, c, *, start):
            first = rowstart_ref[e] + cnt_ref[e]
            n_pad = (cnt_ref[e] + BLK - 1) // BLK * BLK - cnt_ref[e]
            done = jnp.int32(0)
            size = BLK // 2
            while size >= 1:
                piece = n_pad & size

                @pl.when(piece != 0)
                def _(size=size, done=done):
                    zero_copy(first + done, size, start)

                done = done + piece
                size //= 2
            return c

        def idle(b, c, *, start):
            zero_copy(b * BLK, BLK, start)
            return c

        for start in (True, False):
            lax.fori_loop(0, E, functools.partial(tail, start=start), 0)
            lax.fori_loop(nact, NB, functools.partial(idle, start=start), 0)

    @pl.when(t >= 2)
    def _():
        wait_slot(slot)

    stage[slot] = h_ref[...]
    for c in range(TM // LANES):
        def row_body(r, carry, c=c):
            src = pl.multiple_of((c * LANES + r) * R, R)
            for k in range(TOP_K):
                dst = pl.multiple_of(dest_ref[c, k, r] * R, R)
                pltpu.make_async_copy(stage.at[slot, pl.ds(src, R)], xs_hbm.at[pl.ds(dst, R)],
                                      sem.at[slot]).start()
            return carry

        lax.fori_loop(0, LANES, row_body, 0)

    @pl.when(t == nt - 1)
    def _():
        @pl.when(t >= 1)
        def _():
            wait_slot(1 - slot)

        wait_slot(slot)


def _dispatch(h_tiles, dest, rowstart, counts, nact, NB):
    N = h_tiles.shape[0] // SUBLANES
    E = rowstart.shape[0]
    TM = MERGE_TILE
    BLK = EXPERT_BLOCK
    R = SUBLANES
    kern = functools.partial(_dispatch_kernel, TM=TM, BLK=BLK, NB=NB, E=E)
    grid_spec = pltpu.PrefetchScalarGridSpec(
        num_scalar_prefetch=3,
        grid=(N // TM,),
        in_specs=[pl.BlockSpec((TM // LANES, TOP_K, LANES), lambda t, *_: (t, 0, 0),
                               memory_space=pltpu.SMEM),
                  pl.BlockSpec((TM * R, LANES), lambda t, *_: (t, 0))],
        out_specs=pl.BlockSpec(memory_space=pl.ANY),
        scratch_shapes=[pltpu.VMEM((2, TM * R, LANES), F32), pltpu.VMEM((BLK * R, LANES), F32),
                        pltpu.SemaphoreType.DMA((2,)), pltpu.SemaphoreType.DMA(())],
    )
    return pl.pallas_call(
        kern,
        grid_spec=grid_spec,
        out_shape=jax.ShapeDtypeStruct((NB * BLK * R, LANES), F32),
        compiler_params=_params("arbitrary"),
        name="dispatch",
    )(rowstart, counts, nact, dest, h_tiles)


def _expert_kernel(blk_e_ref, nact_ref, x_ref, wg_ref, wu_ref, wd_ref, y_ref, wg_s, wu_s, wd_s,
                   *, BLK):
    b = pl.program_id(0)
    nact = nact_ref[0]

    @pl.when(b < nact)
    def _():
        e = blk_e_ref[b]
        e_prev = blk_e_ref[jnp.maximum(b - 1, 0)]

        @pl.when(jnp.logical_or(b == 0, e != e_prev))
        def _():
            wg_s[...] = wg_ref[...].astype(BF16)
            wu_s[...] = wu_ref[...].astype(BF16)
            wd_s[...] = wd_ref[...].astype(BF16)

        x = _from_row_tiles(x_ref, 0, BLK).astype(BF16)
        g_act = _dot(x, wg_s[...])
        hid = g_act * _sigmoid(g_act) * _dot(x, wu_s[...])
        _to_row_tiles(y_ref, _dot(hid.astype(BF16), wd_s[...]))

    @pl.when(b >= nact)
    def _():
        y_ref[...] = jnp.zeros(y_ref.shape, F32)


def _experts(x_sorted, blk_e, nact, NB, w_eg, w_eu, w_ed):
    E, D, He = w_eg.shape
    BLK = EXPERT_BLOCK
    R = SUBLANES
    live = lambda b, na: jnp.minimum(b, na[0] - 1)
    grid_spec = pltpu.PrefetchScalarGridSpec(
        num_scalar_prefetch=2,
        grid=(NB,),
        in_specs=[pl.BlockSpec((BLK * R, LANES), lambda b, be, na: (live(b, na), 0)),
                  pl.BlockSpec((None, D, He), lambda b, be, na: (be[live(b, na)], 0, 0)),
                  pl.BlockSpec((None, D, He), lambda b, be, na: (be[live(b, na)], 0, 0)),
                  pl.BlockSpec((None, He, D), lambda b, be, na: (be[live(b, na)], 0, 0))],
        out_specs=pl.BlockSpec((BLK * R, LANES), lambda b, be, na: (b, 0)),
        scratch_shapes=[pltpu.VMEM((D, He), BF16), pltpu.VMEM((D, He), BF16),
                        pltpu.VMEM((He, D), BF16)],
    )
    return pl.pallas_call(
        functools.partial(_expert_kernel, BLK=BLK),
        grid_spec=grid_spec,
        out_shape=jax.ShapeDtypeStruct((NB * BLK * R, LANES), F32),
        compiler_params=_params("arbitrary"),
        name="routed_experts",
    )(blk_e, nact, x_sorted, w_eg, w_eu, w_ed)


def _combine_kernel(dest_ref, wt_ref, pre_ref, lng_ref, lnb_ref, y_hbm, o_ref, buf, sem, *, TM):
    t = pl.program_id(0)
    nt = pl.num_programs(0) - 1
    slot = t % 2
    R = SUBLANES

    @pl.when(t < nt)
    def _():
        for k in range(TOP_K):
            for r in range(TM):
                src = pl.multiple_of(dest_ref[0, k, r] * R, R)
                pltpu.make_async_copy(y_hbm.at[pl.ds(src, R)],
                                      buf.at[slot, pl.ds((k * TM + r) * R, R)],
                                      sem.at[slot]).start()

    @pl.when(t >= 1)
    def _():
        prev = 1 - slot
        for k in range(TOP_K):
            pltpu.make_async_copy(y_hbm.at[pl.ds(0, TM * R)],
                                  buf.at[prev, pl.ds(k * TM * R, TM * R)], sem.at[prev]).wait()
        w = jnp.concatenate([wt_ref[0], jnp.zeros((TM - TOP_K, TM), F32)], axis=0).T
        z = pre_ref[...]
        rows = buf.at[prev]
        for k in range(TOP_K):
            z = z + w[:, k:k + 1] * _from_row_tiles(rows, k * TM * R, TM)
        o_ref[...] = _layer_norm(z, lng_ref[...], lnb_ref[...])


def _combine(y_sorted, dest, wts, pre, ln_g, ln_b):
    N, D = pre.shape
    TM = LANES
    nt = N // TM
    R = SUBLANES
    kern = functools.partial(_combine_kernel, TM=TM)
    row = lambda a: a.reshape(1, -1).astype(F32)
    ahead = lambda t: (jnp.minimum(t, nt - 1), 0, 0)
    behind3 = lambda t: (jnp.maximum(t - 1, 0), 0, 0)
    behind = lambda t: (jnp.maximum(t - 1, 0), 0)
    return pl.pallas_call(
        kern,
        grid=(nt + 1,),
        in_specs=[pl.BlockSpec((1, TOP_K, TM), ahead, memory_space=pltpu.SMEM),
                  pl.BlockSpec((1, TOP_K, TM), behind3),
                  pl.BlockSpec((TM, D), behind),
                  _const_spec((1, D)), _const_spec((1, D)),
                  pl.BlockSpec(memory_space=pl.ANY)],
        out_specs=pl.BlockSpec((TM, D), behind),
        out_shape=jax.ShapeDtypeStruct((N, D), F32),
        scratch_shapes=[pltpu.VMEM((2, TOP_K * TM * R, LANES), F32),
                        pltpu.SemaphoreType.DMA((2,))],
        compiler_params=_params("arbitrary"),
        name="combine",
    )(dest, wts, pre, row(ln_g), row(ln_b), y_sorted)


def _dispatch_meta(cnt, N, E):
    BLK = EXPERT_BLOCK
    NB = (N * TOP_K + E * (BLK - 1)) // BLK
    counts = cnt[:, 0].astype(jnp.int32)
    nblk_e = (counts + BLK - 1) // BLK
    bend = jnp.cumsum(nblk_e)
    rowstart = (bend - nblk_e) * BLK
    blk_of = jnp.sum(bend[None, :] <= jnp.arange(NB, dtype=jnp.int32)[:, None], axis=1)
    blk_e = jnp.minimum(blk_of, E - 1).astype(jnp.int32)
    nact = bend[-1:].astype(jnp.int32)
    return counts, rowstart.astype(jnp.int32), blk_e, nact, NB


def kernel(x, rel_bias_table, w_in, b_gate, w_pool_grp, pool_scale, w_branch_pool, lambda_q1,
           lambda_k1, lambda_q2, lambda_k2, subln_gain, w_branch_attn, w_out, ln1_g, ln1_b,
           w_router, router_bias, w_exp_gate, w_exp_up, w_exp_down, w_sh_gate, w_sh_up,
           w_sh_down, ln2_g, ln2_b):
    B, S, D = x.shape
    L = w_in.shape[0]
    E = w_router.shape[2]
    P = pool_scale.shape[1]
    A = w_branch_attn.shape[1]
    N = B * S
    alpha = (2 * L) ** 0.25
    assert S % TOKEN_TILE == 0 and S % MERGE_TILE == 0
    assert A == N_DIFF_HEADS * 2 * DIFF_HEAD_DIM and TOKEN_TILE >= REL_MAX_DIST
    assert D == SUBLANES * LANES, "a row tile holds exactly one (SUBLANES, LANES) tile per row"
    band, far_c = _bias_tables(rel_bias_table, S, TOKEN_TILE)

    h = x
    for i in range(L):
        w_uqkv = w_in[i][:, :P + 3 * A].astype(BF16)
        w_gate = w_in[i][:, P + 3 * A:].astype(BF16)
        u, qk, vt = _inproj(h, w_uqkv, P, A)
        o = _attention(qk, vt, band, far_c, lambda_q1[i], lambda_k1[i], lambda_q2[i],
                       lambda_k2[i], subln_gain[i], _lambda_init(i))
        h1, h1_tiles = _merge(h, u, o, w_gate, b_gate[i], w_pool_grp[i].astype(BF16), pool_scale[i],
                    w_branch_pool[i].astype(BF16), w_branch_attn[i].astype(BF16),
                    w_out[i].astype(BF16), ln1_g[i], ln1_b[i], alpha)
        t = h1.reshape(N, D)
        wr_hi, wr_lo = _split_bf16(w_router[i].astype(F32).T)
        idx, wts, rank, cnt, pre = _route(t, wr_hi, wr_lo, router_bias[i],
                                          w_sh_gate[i].astype(BF16), w_sh_up[i].astype(BF16),
                                          w_sh_down[i].astype(BF16), alpha)
        counts, rowstart, blk_e, nact, NB = _dispatch_meta(cnt, N, E)
        dest = _slots(idx, rank, rowstart)
        x_sorted = _dispatch(h1_tiles.reshape(N * SUBLANES, LANES), dest, rowstart, counts, nact, NB)
        y_sorted = _experts(x_sorted, blk_e, nact, NB, w_exp_gate[i], w_exp_up[i], w_exp_down[i])
        h = _combine(y_sorted, dest, wts, pre, ln2_g[i], ln2_b[i]).reshape(B, S, D)
    return h
```

```python
import functools
import math

import numpy as np
import jax
import jax.numpy as jnp
from jax import lax
from jax.experimental import pallas as pl
from jax.experimental.pallas import tpu as pltpu

F32 = jnp.float32
BF16 = jnp.bfloat16

POOL_WINDOWS = (2, 4, 8, 16)
N_DIFF_HEADS = 4
DIFF_HEAD_DIM = 64
REL_BUCKETS = 32
REL_MAX_DIST = 128
TOP_K = 8
N_EXPERT_GROUPS = 8
TOPK_GROUPS = 4
ROUTED_SCALE = 2.5
LN_EPS = 1e-5
LOG2E = math.log2(math.e)

LANES = 128
SUBLANES = 8
VMEM_LIMIT_BYTES = 56 * 1024 * 1024

TOKEN_TILE = 512
MERGE_TILE = 256
EXPERT_BLOCK = 256

_NT = (((1,), (1,)), ((), ()))


def _lambda_init(layer_idx):
    return 0.8 - 0.6 * math.exp(-0.3 * layer_idx)


def _dot(a, b):
    return jnp.dot(a, b, preferred_element_type=F32)


def _split_bf16(a):
    hi = a.astype(BF16)
    lo = (a - hi.astype(F32)).astype(BF16)
    return hi, lo


def _sigmoid(z):
    return 1.0 / (1.0 + jnp.exp(-z))


def _layer_norm(z, g, b):
    mu = jnp.mean(z, axis=-1, keepdims=True)
    zc = z - mu
    var = jnp.mean(zc * zc, axis=-1, keepdims=True)
    return zc * lax.rsqrt(var + LN_EPS) * g + b


def _params(*sem):
    return pltpu.CompilerParams(dimension_semantics=sem, vmem_limit_bytes=VMEM_LIMIT_BYTES)


def _const_spec(shape):
    nd = len(shape)
    return pl.BlockSpec(shape, lambda *_: (0,) * nd)


def _to_row_tiles(ref, x):
    rows = x.shape[0]
    for s in range(SUBLANES):
        ref[pl.ds(s, rows, stride=SUBLANES), :] = x[:, s * LANES:(s + 1) * LANES]


def _from_row_tiles(ref, start, rows):
    return jnp.concatenate([ref[pl.ds(start + s, rows, stride=SUBLANES), :]
                            for s in range(SUBLANES)], axis=1)


def _inproj_kernel(x_ref, w_ref, u_ref, qk_ref, vt_ref, *, pool_w, attn_w, q_scale):
    xb = x_ref[...].astype(BF16)
    p = _dot(xb, w_ref[...])
    u_ref[...] = p[:, :pool_w]
    q = p[:, pool_w:pool_w + attn_w] * q_scale
    k = p[:, pool_w + attn_w:pool_w + 2 * attn_w]
    qk_ref[:, :attn_w] = q.astype(BF16)
    qk_ref[:, attn_w:] = k.astype(BF16)
    v = p[:, pool_w + 2 * attn_w:pool_w + 3 * attn_w]
    vt_ref[...] = v.T.astype(BF16)


def _inproj(x, w_uqkv, pool_w, attn_w):
    B, S, D = x.shape
    T = TOKEN_TILE
    ns = S // T
    width = w_uqkv.shape[1]
    kern = functools.partial(_inproj_kernel, pool_w=pool_w, attn_w=attn_w,
                             q_scale=DIFF_HEAD_DIM ** -0.5 * LOG2E)
    return pl.pallas_call(
        kern,
        grid=(B, ns),
        in_specs=[pl.BlockSpec((None, T, D), lambda b, s: (b, s, 0)),
                  _const_spec((D, width))],
        out_specs=[pl.BlockSpec((None, T, pool_w), lambda b, s: (b, s, 0)),
                   pl.BlockSpec((None, T, 2 * attn_w), lambda b, s: (b, s, 0)),
                   pl.BlockSpec((None, None, attn_w, T), lambda b, s: (b, s, 0, 0))],
        out_shape=[jax.ShapeDtypeStruct((B, S, pool_w), F32),
                   jax.ShapeDtypeStruct((B, S, 2 * attn_w), BF16),
                   jax.ShapeDtypeStruct((B, ns, attn_w, T), BF16)],
        compiler_params=_params("parallel", "parallel"),
        name="inproj",
    )(x, w_uqkv)


def _t5_bucket_np(rel):
    half = REL_BUCKETS // 2
    max_exact = half // 2
    ret = np.where(rel > 0, half, 0)
    n = np.abs(rel)
    nf = np.maximum(n, 1).astype(np.float32)
    large = max_exact + (np.log(nf / max_exact) / math.log(REL_MAX_DIST / max_exact)
                         * (half - max_exact)).astype(np.int32)
    large = np.minimum(large, half - 1)
    return ret + np.where(n < max_exact, n, large)


def _t5_bucket(rel):
    half = REL_BUCKETS // 2
    max_exact = half // 2
    ret = jnp.where(rel > 0, half, 0)
    n = jnp.abs(rel)
    nf = jnp.maximum(n, 1).astype(F32)
    large = max_exact + (jnp.log(nf / max_exact) / math.log(REL_MAX_DIST / max_exact)
                         * (half - max_exact)).astype(jnp.int32)
    large = jnp.minimum(large, half - 1)
    return ret + jnp.where(n < max_exact, n, large)


def _bias_tables(table, S, T):
    far = np.arange(T + 1, S)
    if far.size:
        assert np.all(_t5_bucket_np(-far) == _t5_bucket_np(-far[-1]))
        assert np.all(_t5_bucket_np(far) == _t5_bucket_np(far[-1]))
    rel = jnp.arange(-(S - 1), S, dtype=jnp.int32)
    dist_bias = table[_t5_bucket(rel)].astype(F32).T
    pad = jnp.pad(dist_bias, ((0, 0), (2 * T, 2 * T)), mode='edge')
    diag = jnp.stack([lax.slice_in_dim(pad, (d - 1) * T - (T - 1) + S - 1 + 2 * T,
                                       (d - 1) * T + T + S - 1 + 2 * T, axis=1)
                      for d in range(3)], axis=1)
    w = jnp.concatenate([diag[..., ::-1], diag[..., :1]], axis=-1)
    flat = jnp.tile(w, (1, 1, T))[..., :T * (2 * T - 1)]
    band = flat.reshape(w.shape[0], 3, T, 2 * T - 1)[..., T - 1:]
    far_c = jnp.stack([dist_bias[:, 0], dist_bias[:, 2 * S - 2]], axis=1)
    return band * LOG2E, far_c * LOG2E


def _attn_kernel(far_ref, q_ref, k_ref, vt_ref, band_ref, lq1_ref, lk1_ref, lq2_ref, lk2_ref,
                 gain_ref, o_ref, qz_ref, s00, s01, s10, s11, e00, e01, e10, e11, acc0, acc1,
                 mt_ref, al_ref, m_ref, l_ref, *, T, nk, lam_init):
    h = pl.program_id(1)
    i = pl.program_id(2)
    s_buf = ((s00, s01), (s10, s11))
    e_buf = ((e00, e01), (e10, e11))
    acc = (acc0, acc1)
    c_left = far_ref[h, 0]
    c_right = far_ref[h, 1]

    q = q_ref[...]
    lane = lax.broadcasted_iota(jnp.int32, q.shape, 1)
    zero = jnp.zeros_like(q)
    qz_ref[0] = jnp.where(lane < DIFF_HEAD_DIM, q, zero)
    qz_ref[1] = jnp.where(lane >= DIFF_HEAD_DIM, q, zero)
    m_ref[...] = jnp.full(m_ref.shape, -jnp.inf, F32)
    l_ref[...] = jnp.zeros(l_ref.shape, F32)
    acc0[...] = jnp.zeros(acc0.shape, F32)
    acc1[...] = jnp.zeros(acc1.shape, F32)

    has_prev = i >= 1
    has_next = i <= nk - 2
    n_left = jnp.maximum(i - 1, 0) - jnp.where(has_next, 0, 1)
    right0 = i + 2 + jnp.where(has_prev, 0, 1)

    def tile_of(p):
        if isinstance(p, int) and p == 0:
            return i, band_ref[1], None
        if isinstance(p, int) and p == 1:
            return (jnp.where(has_prev, i - 1, i + 2),
                    jnp.where(has_prev, band_ref[0], c_right), None)
        if isinstance(p, int) and p == 2:
            return (jnp.where(has_next, i + 1, i - 2),
                    jnp.where(has_next, band_ref[2], c_left), None)
        f = p - 3
        is_left = f < n_left
        return (jnp.where(is_left, f, f - n_left + right0), None,
                jnp.where(is_left, c_left, c_right))

    def stage_a(p, x):
        j, bias, const = tile_of(p)
        kt = k_ref[pl.ds(pl.multiple_of(j * T, T), T), :]
        for c in range(2):
            s = lax.dot_general(kt, qz_ref[c], _NT, preferred_element_type=F32)
            if bias is not None:
                s = s + bias
            s_buf[x][c][...] = s
            mt = jnp.max(s, axis=0, keepdims=True)
            mt_ref[2 * x + c] = mt if const is None else mt + const

    def stage_b(p, x):
        _, _, const = tile_of(p)
        for c in range(2):
            m_old = m_ref[c]
            m_new = jnp.maximum(m_old, mt_ref[2 * x + c])
            alpha = jnp.exp2(m_old - m_new)
            shift = m_new if const is None else m_new - const
            e = jnp.exp2(s_buf[x][c][...] - shift)
            l_ref[c] = alpha * l_ref[c] + jnp.sum(e, axis=0, keepdims=True)
            e_buf[x][c][...] = e.astype(BF16)
            al_ref[2 * x + c] = alpha
            m_ref[c] = m_new

    def stage_c(p, x):
        j, _, _ = tile_of(p)
        vt = vt_ref[j]
        for c in range(2):
            acc[c][...] = al_ref[2 * x + c] * acc[c][...] + _dot(vt, e_buf[x][c][...])

    def iteration(n, parity):
        static = isinstance(n, int)
        if not static or n < nk:
            stage_a(n, parity)
        if not static or 1 <= n <= nk:
            stage_b(n - 1, 1 - parity)
        if not static or 2 <= n <= nk + 1:
            stage_c(n - 2, parity)

    head = 5
    for n in range(head):
        iteration(n, n % 2)

    def pair(it, carry):
        n = head + 2 * it
        iteration(n, head % 2)
        iteration(n + 1, 1 - head % 2)
        return carry

    n_pairs = (nk - 1 - head) // 2
    lax.fori_loop(0, n_pairs, pair, 0)
    for n in range(head + 2 * n_pairs, nk + 2):
        iteration(n, n % 2)

    lam = (jnp.exp(jnp.sum(lq1_ref[...] * lk1_ref[...], keepdims=True))
           - jnp.exp(jnp.sum(lq2_ref[...] * lk2_ref[...], keepdims=True)) + lam_init)
    o = acc0[...] / l_ref[0] - lam * (acc1[...] / l_ref[1])
    ms = jnp.mean(o * o, axis=0, keepdims=True)
    y = o * lax.rsqrt(ms + LN_EPS) * gain_ref[...] * (1.0 - lam_init)
    o_ref[...] = y.T.astype(BF16)


def _attention(qk, vt, band, far_c, lq1, lk1, lq2, lk2, gain, lam_init):
    B, S, _ = qk.shape
    T = TOKEN_TILE
    nk = S // T
    assert nk >= 6, "the pipeline prologue assumes at least three far key tiles"
    H = N_DIFF_HEADS
    hw = 2 * DIFF_HEAD_DIM
    kern = functools.partial(_attn_kernel, T=T, nk=nk, lam_init=lam_init)
    vec = lambda a: a.reshape(1, DIFF_HEAD_DIM).astype(F32)
    return pl.pallas_call(
        kern,
        grid=(B, H, nk),
        in_specs=[pl.BlockSpec(memory_space=pltpu.SMEM),
                  pl.BlockSpec((None, T, hw), lambda b, h, i: (b, i, h)),
                  pl.BlockSpec((None, S, hw), lambda b, h, i: (b, 0, H + h)),
                  pl.BlockSpec((None, nk, hw, T), lambda b, h, i: (b, 0, h, 0)),
                  pl.BlockSpec((None, 3, T, T), lambda b, h, i: (h, 0, 0, 0)),
                  _const_spec((1, DIFF_HEAD_DIM)), _const_spec((1, DIFF_HEAD_DIM)),
                  _const_spec((1, DIFF_HEAD_DIM)), _const_spec((1, DIFF_HEAD_DIM)),
                  _const_spec((hw, 1))],
        out_specs=pl.BlockSpec((None, T, hw), lambda b, h, i: (b, i, h)),
        out_shape=jax.ShapeDtypeStruct((B, S, H * hw), BF16),
        scratch_shapes=([pltpu.VMEM((2, T, hw), BF16)]
                        + [pltpu.VMEM((T, T), F32)] * 4 + [pltpu.VMEM((T, T), BF16)] * 4
                        + [pltpu.VMEM((hw, T), F32)] * 2
                        + [pltpu.VMEM((4, 1, T), F32), pltpu.VMEM((4, 1, T), F32),
                           pltpu.VMEM((2, 1, T), F32), pltpu.VMEM((2, 1, T), F32)]),
        compiler_params=_params("parallel", "parallel", "parallel"),
        name="diff_attn",
    )(far_c, qk, qk, vt, band, vec(lq1), vec(lk1), vec(lq2), vec(lk2),
      gain.reshape(hw, 1).astype(F32))


def _pool_matrices(TM):
    r = np.arange(TM)[:, None]
    c = np.arange(TM)[None, :]
    hpos = np.concatenate([np.arange(-SUBLANES, 0), np.arange(TM, TM + SUBLANES),
                           np.full(LANES - 2 * SUBLANES, 10 ** 6)])[None, :]
    main, halo = [], []
    for w in POOL_WINDOWS:
        lo, hi = r - w // 2, r + w - w // 2
        main.append((c >= lo) & (c < hi))
        halo.append((hpos >= lo) & (hpos < hi))
    return (jnp.asarray(np.stack(main), BF16), jnp.asarray(np.stack(halo), BF16))


def _merge_kernel(x_ref, u_ref, up_ref, un_ref, o_ref, wg_ref, bg_ref, am_ref, ah_ref, wgrp_ref,
                  ps_ref, wbp_ref, wba_ref, wo_ref, lng_ref, lnb_ref, h_ref, ht_ref, *, TM, S, alpha):
    s_idx = pl.program_id(1)
    ns = pl.num_programs(1)
    x = x_ref[...]
    D = x.shape[1]
    xb = x.astype(BF16)
    gates = _sigmoid(_dot(xb, wg_ref[...]) + bg_ref[...])

    u = u_ref[...]
    P = u.shape[1]
    prev = jnp.where(s_idx > 0, up_ref[...], 0.0)
    nxt = jnp.where(s_idx < ns - 1, un_ref[...], 0.0)
    halo = jnp.concatenate([prev, nxt, jnp.zeros((LANES - 2 * SUBLANES, P), F32)], axis=0)
    u_hi, u_lo = _split_bf16(u)
    h_hi, h_lo = _split_bf16(halo)
    pos = s_idx * TM + lax.broadcasted_iota(jnp.int32, (TM, 1), 0)
    G = P // len(POOL_WINDOWS)
    mixed = []
    for g, w in enumerate(POOL_WINDOWS):
        sl = slice(g * G, (g + 1) * G)
        win2 = (_dot(am_ref[g], jnp.concatenate([u_hi[:, sl], u_lo[:, sl]], axis=1))
                + _dot(ah_ref[g], jnp.concatenate([h_hi[:, sl], h_lo[:, sl]], axis=1)))
        win = win2[:, :G] + win2[:, G:]
        cnt = (jnp.minimum(pos + (w - w // 2), S) - jnp.maximum(pos - w // 2, 0)).astype(F32)
        pooled = win / cnt - u[:, sl]
        mixed.append(_dot(pooled.astype(BF16), wgrp_ref[g]))
    mix = jnp.concatenate(mixed, axis=1) * ps_ref[...]
    y_pool = _dot(mix.astype(BF16), wbp_ref[...])
    y_attn = _dot(o_ref[...], wba_ref[...])
    merged = gates[:, :D] * y_pool + gates[:, D:] * y_attn
    y = _dot(merged.astype(BF16), wo_ref[...])
    h = _layer_norm(alpha * x + y, lng_ref[...], lnb_ref[...])
    h_ref[...] = h
    _to_row_tiles(ht_ref, h)


def _merge(x, u, o, w_gate, b_gate, w_grp, pool_scale, w_bp, w_ba, w_out, ln_g, ln_b, alpha):
    B, S, D = x.shape
    P = u.shape[2]
    A = o.shape[2]
    TM = MERGE_TILE
    ns = S // TM
    hb = TM // SUBLANES
    am, ah = _pool_matrices(TM)
    nw = len(POOL_WINDOWS)
    G = P // nw
    row = lambda a: a.reshape(1, -1).astype(F32)
    kern = functools.partial(_merge_kernel, TM=TM, S=S, alpha=alpha)
    return pl.pallas_call(
        kern,
        grid=(B, ns),
        in_specs=[pl.BlockSpec((None, TM, D), lambda b, s: (b, s, 0)),
                  pl.BlockSpec((None, TM, P), lambda b, s: (b, s, 0)),
                  pl.BlockSpec((None, SUBLANES, P),
                               lambda b, s: (b, jnp.maximum(s * hb - 1, 0), 0)),
                  pl.BlockSpec((None, SUBLANES, P),
                               lambda b, s: (b, jnp.minimum((s + 1) * hb, S // SUBLANES - 1), 0)),
                  pl.BlockSpec((None, TM, A), lambda b, s: (b, s, 0)),
                  _const_spec((D, 2 * D)), _const_spec((1, 2 * D)),
                  _const_spec((nw, TM, TM)), _const_spec((nw, TM, LANES)),
                  _const_spec((nw, G, G)), _const_spec((1, P)),
                  _const_spec((P, D)), _const_spec((A, D)), _const_spec((D, D)),
                  _const_spec((1, D)), _const_spec((1, D))],
        out_specs=[pl.BlockSpec((None, TM, D), lambda b, s: (b, s, 0)),
                   pl.BlockSpec((None, TM * SUBLANES, LANES), lambda b, s: (b, s, 0))],
        out_shape=[jax.ShapeDtypeStruct((B, S, D), F32),
                   jax.ShapeDtypeStruct((B, S * SUBLANES, LANES), F32)],
        compiler_params=_params("parallel", "parallel"),
        name="merge",
    )(x, u, u, u, o, w_gate, row(b_gate), am, ah, w_grp, row(pool_scale), w_bp, w_ba, w_out,
      row(ln_g), row(ln_b))


def _route_kernel(h_ref, wrh_ref, wrl_ref, rb_ref, tri_ref, ones_ref, wsg_ref, wsu_ref, wsd_ref,
                  idx_ref, wt_ref, rank_ref, cnt_ref, pre_ref, cnt_acc, *, TM, E, alpha):
    t = pl.program_id(0)

    @pl.when(t == 0)
    def _():
        cnt_acc[...] = jnp.zeros(cnt_acc.shape, F32)

    h = h_ref[...]
    hb, hl = _split_bf16(h)
    logits = (lax.dot_general(wrh_ref[...], hb, _NT, preferred_element_type=F32)
              + lax.dot_general(wrh_ref[...], hl, _NT, preferred_element_type=F32)
              + lax.dot_general(wrl_ref[...], hb, _NT, preferred_element_type=F32))
    s = _sigmoid(logits)
    biased = s + rb_ref[...]
    gsz = E // N_EXPERT_GROUPS
    sub = lax.broadcasted_iota(jnp.int32, (gsz, TM), 0).astype(F32)
    neg = -jnp.inf
    grp, gscore = [], []
    for g in range(N_EXPERT_GROUPS):
        bg = biased[g * gsz:(g + 1) * gsz, :]
        m1 = jnp.max(bg, axis=0, keepdims=True)
        first = jnp.min(jnp.where(bg == m1, sub, float(gsz)), axis=0, keepdims=True)
        m2 = jnp.max(jnp.where(sub == first, neg, bg), axis=0, keepdims=True)
        grp.append(bg)
        gscore.append(m1 + m2)
    masked = []
    for g in range(N_EXPERT_GROUPS):
        beaten = jnp.zeros((1, TM), F32)
        for g2 in range(N_EXPERT_GROUPS):
            if g2 == g:
                continue
            wins = (gscore[g2] > gscore[g]) if g2 > g else (gscore[g2] >= gscore[g])
            beaten = beaten + wins.astype(F32)
        masked.append(jnp.where(beaten < float(TOPK_GROUPS), grp[g], neg))
    masked = jnp.concatenate(masked, axis=0)

    row = lax.broadcasted_iota(jnp.int32, (E, TM), 0).astype(F32)
    sel = jnp.zeros((E, TM), F32)
    firsts, ws = [], []
    for _ in range(TOP_K):
        mx = jnp.max(masked, axis=0, keepdims=True)
        first = jnp.min(jnp.where(masked == mx, row, float(E)), axis=0, keepdims=True)
        oh = row == first
        ws.append(jnp.sum(jnp.where(oh, s, 0.0), axis=0, keepdims=True))
        masked = jnp.where(oh, neg, masked)
        sel = jnp.where(oh, 1.0, sel)
        firsts.append(first)
    wsum = ws[0]
    for w in ws[1:]:
        wsum = wsum + w

    sel_b = sel.astype(BF16)
    base = jnp.concatenate([cnt_acc[...]] * (TM // LANES), axis=1)
    rank_all = _dot(sel_b, tri_ref[...]) + base
    cnt_acc[...] = cnt_acc[...] + _dot(sel_b, ones_ref[...])
    cnt_ref[...] = cnt_acc[...]
    for k in range(TOP_K):
        oh = row == firsts[k]
        idx_k = firsts[k].astype(jnp.int32)
        wt_k = ws[k] / wsum * ROUTED_SCALE
        rank_k = jnp.sum(jnp.where(oh, rank_all, 0.0), axis=0, keepdims=True).astype(jnp.int32)
        for c in range(TM // LANES):
            sl = slice(c * LANES, (c + 1) * LANES)
            idx_ref[c, k:k + 1, :] = idx_k[:, sl]
            wt_ref[c, k:k + 1, :] = wt_k[:, sl]
            rank_ref[c, k:k + 1, :] = rank_k[:, sl]

    g_act = _dot(hb, wsg_ref[...])
    hid = g_act * _sigmoid(g_act) * _dot(hb, wsu_ref[...])
    pre_ref[...] = alpha * h + _dot(hid.astype(BF16), wsd_ref[...])


def _route(h, wr_t_hi, wr_t_lo, router_bias, w_sg, w_su, w_sd, alpha):
    N, D = h.shape
    E = wr_t_hi.shape[0]
    TM = TOKEN_TILE
    Hs = w_sg.shape[1]
    tri = jnp.asarray(np.triu(np.ones((TM, TM), np.float32), k=1), BF16)
    ones = jnp.ones((TM, LANES), BF16)
    kern = functools.partial(_route_kernel, TM=TM, E=E, alpha=alpha)
    kt = pl.BlockSpec((TM // LANES, TOP_K, LANES), lambda t: (t, 0, 0))
    return pl.pallas_call(
        kern,
        grid=(N // TM,),
        in_specs=[pl.BlockSpec((TM, D), lambda t: (t, 0)),
                  _const_spec((E, D)), _const_spec((E, D)), _const_spec((E, 1)),
                  _const_spec((TM, TM)), _const_spec((TM, LANES)),
                  _const_spec((D, Hs)), _const_spec((D, Hs)), _const_spec((Hs, D))],
        out_specs=[kt, kt, kt, _const_spec((E, LANES)), pl.BlockSpec((TM, D), lambda t: (t, 0))],
        out_shape=[jax.ShapeDtypeStruct((N // LANES, TOP_K, LANES), jnp.int32),
                   jax.ShapeDtypeStruct((N // LANES, TOP_K, LANES), F32),
                   jax.ShapeDtypeStruct((N // LANES, TOP_K, LANES), jnp.int32),
                   jax.ShapeDtypeStruct((E, LANES), F32),
                   jax.ShapeDtypeStruct((N, D), F32)],
        scratch_shapes=[pltpu.VMEM((E, LANES), F32)],
        compiler_params=_params("arbitrary"),
        name="route_shared",
    )(h, wr_t_hi, wr_t_lo, router_bias.reshape(E, 1).astype(F32), tri, ones, w_sg, w_su, w_sd)


def _slots_kernel(idx_ref, rank_ref, rowstart_ref, dest_ref, *, E):
    expert = lax.broadcasted_iota(jnp.int32, (E, LANES), 0)
    rowstart = rowstart_ref[...]
    for c in range(idx_ref.shape[0]):
        idx = idx_ref[c]
        rank = rank_ref[c]
        for k in range(TOP_K):
            base = jnp.sum(jnp.where(expert == idx[k:k + 1, :], rowstart, 0.0), axis=0,
                           keepdims=True)
            dest_ref[c, k:k + 1, :] = base.astype(jnp.int32) + rank[k:k + 1, :]


def _slots(idx, rank, rowstart):
    nb = idx.shape[0]
    E = rowstart.shape[0]
    step = SUBLANES
    spec = pl.BlockSpec((step, TOP_K, LANES), lambda t: (t, 0, 0))
    return pl.pallas_call(
        functools.partial(_slots_kernel, E=E),
        grid=(nb // step,),
        in_specs=[spec, spec, _const_spec((E, 1))],
        out_specs=spec,
        out_shape=jax.ShapeDtypeStruct(idx.shape, jnp.int32),
        compiler_params=_params("parallel"),
        name="slots",
    )(idx, rank, rowstart.astype(F32).reshape(E, 1))


def _dispatch_kernel(rowstart_ref, cnt_ref, nact_ref, dest_ref, h_ref, xs_hbm,
                     stage, zeros, sem, zsem, *, TM, BLK, NB, E):
    t = pl.program_id(0)
    nt = pl.num_programs(0)
    slot = t % 2
    R = SUBLANES

    def zero_copy(first, n, start):
        cp = pltpu.make_async_copy(zeros.at[pl.ds(0, n * R)],
                                   xs_hbm.at[pl.ds(pl.multiple_of(first * R, R), n * R)], zsem)
        cp.start() if start else cp.wait()

    def wait_slot(s):
        for _ in range(TOP_K):
            pltpu.make_async_copy(stage.at[s], xs_hbm.at[pl.ds(0, TM * R)], sem.at[s]).wait()

    @pl.when(t == 0)
    def _():
        zeros[...] = jnp.zeros(zeros.shape, F32)
        nact = nact_ref[0]

        def tail(e, c, *, start):
            first = rowstart_ref[e] + cnt_ref[e]
            n_pad = (cnt_ref[e] + BLK - 1) // BLK * BLK - cnt_ref[e]
            done = jnp.int32(0)
            size = BLK // 2
            while size >= 1:
                piece = n_pad & size

                @pl.when(piece != 0)
                def _(size=size, done=done):
                    zero_copy(first + done, size, start)

                done = done + piece
                size //= 2
            return c

        def idle(b, c, *, start):
            zero_copy(b * BLK, BLK, start)
            return c

        for start in (True, False):
            lax.fori_loop(0, E, functools.partial(tail, start=start), 0)
            lax.fori_loop(nact, NB, functools.partial(idle, start=start), 0)

    @pl.when(t >= 2)
    def _():
        wait_slot(slot)

    stage[slot] = h_ref[...]
    for c in range(TM // LANES):
        def row_body(r, carry, c=c):
            src = pl.multiple_of((c * LANES + r) * R, R)
            for k in range(TOP_K):
                dst = pl.multiple_of(dest_ref[c, k, r] * R, R)
                pltpu.make_async_copy(stage.at[slot, pl.ds(src, R)], xs_hbm.at[pl.ds(dst, R)],
                                      sem.at[slot]).start(priority=k % 2)
            return carry

        lax.fori_loop(0, LANES, row_body, 0)

    @pl.when(t == nt - 1)
    def _():
        @pl.when(t >= 1)
        def _():
            wait_slot(1 - slot)

        wait_slot(slot)


def _dispatch(h_tiles, dest, rowstart, counts, nact, NB):
    N = h_tiles.shape[0] // SUBLANES
    E = rowstart.shape[0]
    TM = MERGE_TILE
    BLK = EXPERT_BLOCK
    R = SUBLANES
    kern = functools.partial(_dispatch_kernel, TM=TM, BLK=BLK, NB=NB, E=E)
    grid_spec = pltpu.PrefetchScalarGridSpec(
        num_scalar_prefetch=3,
        grid=(N // TM,),
        in_specs=[pl.BlockSpec((TM // LANES, TOP_K, LANES), lambda t, *_: (t, 0, 0),
                               memory_space=pltpu.SMEM),
                  pl.BlockSpec((TM * R, LANES), lambda t, *_: (t, 0))],
        out_specs=pl.BlockSpec(memory_space=pl.ANY),
        scratch_shapes=[pltpu.VMEM((2, TM * R, LANES), F32), pltpu.VMEM((BLK * R, LANES), F32),
                        pltpu.SemaphoreType.DMA((2,)), pltpu.SemaphoreType.DMA(())],
    )
    return pl.pallas_call(
        kern,
        grid_spec=grid_spec,
        out_shape=jax.ShapeDtypeStruct((NB * BLK * R, LANES), F32),
        compiler_params=_params("arbitrary"),
        name="dispatch",
    )(rowstart, counts, nact, dest, h_tiles)


def _expert_kernel(blk_e_ref, nact_ref, x_ref, wg_ref, wu_ref, wd_ref, y_ref, wg_s, wu_s, wd_s,
                   *, BLK):
    b = pl.program_id(0)
    nact = nact_ref[0]

    @pl.when(b < nact)
    def _():
        e = blk_e_ref[b]
        e_prev = blk_e_ref[jnp.maximum(b - 1, 0)]

        @pl.when(jnp.logical_or(b == 0, e != e_prev))
        def _():
            wg_s[...] = wg_ref[...].astype(BF16)
            wu_s[...] = wu_ref[...].astype(BF16)
            wd_s[...] = wd_ref[...].astype(BF16)

        x = _from_row_tiles(x_ref, 0, BLK).astype(BF16)
        g_act = _dot(x, wg_s[...])
        hid = g_act * _sigmoid(g_act) * _dot(x, wu_s[...])
        _to_row_tiles(y_ref, _dot(hid.astype(BF16), wd_s[...]))

    @pl.when(b >= nact)
    def _():
        y_ref[...] = jnp.zeros(y_ref.shape, F32)


def _experts(x_sorted, blk_e, nact, NB, w_eg, w_eu, w_ed):
    E, D, He = w_eg.shape
    BLK = EXPERT_BLOCK
    R = SUBLANES
    live = lambda b, na: jnp.minimum(b, na[0] - 1)
    grid_spec = pltpu.PrefetchScalarGridSpec(
        num_scalar_prefetch=2,
        grid=(NB,),
        in_specs=[pl.BlockSpec((BLK * R, LANES), lambda b, be, na: (live(b, na), 0)),
                  pl.BlockSpec((None, D, He), lambda b, be, na: (be[live(b, na)], 0, 0)),
                  pl.BlockSpec((None, D, He), lambda b, be, na: (be[live(b, na)], 0, 0)),
                  pl.BlockSpec((None, He, D), lambda b, be, na: (be[live(b, na)], 0, 0))],
        out_specs=pl.BlockSpec((BLK * R, LANES), lambda b, be, na: (b, 0)),
        scratch_shapes=[pltpu.VMEM((D, He), BF16), pltpu.VMEM((D, He), BF16),
                        pltpu.VMEM((He, D), BF16)],
    )
    return pl.pallas_call(
        functools.partial(_expert_kernel, BLK=BLK),
        grid_spec=grid_spec,
        out_shape=jax.ShapeDtypeStruct((NB * BLK * R, LANES), F32),
        compiler_params=_params("arbitrary"),
        name="routed_experts",
    )(blk_e, nact, x_sorted, w_eg, w_eu, w_ed)


def _combine_kernel(dest_ref, wt_ref, pre_ref, lng_ref, lnb_ref, y_hbm, o_ref, buf, sem, *, TM):
    t = pl.program_id(0)
    nt = pl.num_programs(0) - 1
    slot = t % 2
    R = SUBLANES

    @pl.when(t < nt)
    def _():
        for k in range(TOP_K):
            for r in range(TM):
                src = pl.multiple_of(dest_ref[0, k, r] * R, R)
                pltpu.make_async_copy(y_hbm.at[pl.ds(src, R)],
                                      buf.at[slot, pl.ds((k * TM + r) * R, R)],
                                      sem.at[slot]).start(priority=r % 2)

    @pl.when(t >= 1)
    def _():
        prev = 1 - slot
        for k in range(TOP_K):
            pltpu.make_async_copy(y_hbm.at[pl.ds(0, TM * R)],
                                  buf.at[prev, pl.ds(k * TM * R, TM * R)], sem.at[prev]).wait()
        w = jnp.concatenate([wt_ref[0], jnp.zeros((TM - TOP_K, TM), F32)], axis=0).T
        z = pre_ref[...]
        rows = buf.at[prev]
        for k in range(TOP_K):
            z = z + w[:, k:k + 1] * _from_row_tiles(rows, k * TM * R, TM)
        o_ref[...] = _layer_norm(z, lng_ref[...], lnb_ref[...])


def _combine(y_sorted, dest, wts, pre, ln_g, ln_b):
    N, D = pre.shape
    TM = LANES
    nt = N // TM
    R = SUBLANES
    kern = functools.partial(_combine_kernel, TM=TM)
    row = lambda a: a.reshape(1, -1).astype(F32)
    ahead = lambda t: (jnp.minimum(t, nt - 1), 0, 0)
    behind3 = lambda t: (jnp.maximum(t - 1, 0), 0, 0)
    behind = lambda t: (jnp.maximum(t - 1, 0), 0)
    return pl.pallas_call(
        kern,
        grid=(nt + 1,),
        in_specs=[pl.BlockSpec((1, TOP_K, TM), ahead, memory_space=pltpu.SMEM),
                  pl.BlockSpec((1, TOP_K, TM), behind3),
                  pl.BlockSpec((TM, D), behind),
                  _const_spec((1, D)), _const_spec((1, D)),
                  pl.BlockSpec(memory_space=pl.ANY)],
        out_specs=pl.BlockSpec((TM, D), behind),
        out_shape=jax.ShapeDtypeStruct((N, D), F32),
        scratch_shapes=[pltpu.VMEM((2, TOP_K * TM * R, LANES), F32),
                        pltpu.SemaphoreType.DMA((2,))],
        compiler_params=_params("arbitrary"),
        name="combine",
    )(dest, wts, pre, row(ln_g), row(ln_b), y_sorted)


def _dispatch_meta(cnt, N, E):
    BLK = EXPERT_BLOCK
    NB = (N * TOP_K + E * (BLK - 1)) // BLK
    counts = cnt[:, 0].astype(jnp.int32)
    nblk_e = (counts + BLK - 1) // BLK
    bend = jnp.cumsum(nblk_e)
    rowstart = (bend - nblk_e) * BLK
    blk_of = jnp.sum(bend[None, :] <= jnp.arange(NB, dtype=jnp.int32)[:, None], axis=1)
    blk_e = jnp.minimum(blk_of, E - 1).astype(jnp.int32)
    nact = bend[-1:].astype(jnp.int32)
    return counts, rowstart.astype(jnp.int32), blk_e, nact, NB


def kernel(x, rel_bias_table, w_in, b_gate, w_pool_grp, pool_scale, w_branch_pool, lambda_q1,
           lambda_k1, lambda_q2, lambda_k2, subln_gain, w_branch_attn, w_out, ln1_g, ln1_b,
           w_router, router_bias, w_exp_gate, w_exp_up, w_exp_down, w_sh_gate, w_sh_up,
           w_sh_down, ln2_g, ln2_b):
    B, S, D = x.shape
    L = w_in.shape[0]
    E = w_router.shape[2]
    P = pool_scale.shape[1]
    A = w_branch_attn.shape[1]
    N = B * S
    alpha = (2 * L) ** 0.25
    assert S % TOKEN_TILE == 0 and S % MERGE_TILE == 0
    assert A == N_DIFF_HEADS * 2 * DIFF_HEAD_DIM and TOKEN_TILE >= REL_MAX_DIST
    assert D == SUBLANES * LANES, "a row tile holds exactly one (SUBLANES, LANES) tile per row"
    band, far_c = _bias_tables(rel_bias_table, S, TOKEN_TILE)

    h = x
    for i in range(L):
        w_uqkv = w_in[i][:, :P + 3 * A].astype(BF16)
        w_gate = w_in[i][:, P + 3 * A:].astype(BF16)
        u, qk, vt = _inproj(h, w_uqkv, P, A)
        o = _attention(qk, vt, band, far_c, lambda_q1[i], lambda_k1[i], lambda_q2[i],
                       lambda_k2[i], subln_gain[i], _lambda_init(i))
        h1, h1_tiles = _merge(h, u, o, w_gate, b_gate[i], w_pool_grp[i].astype(BF16), pool_scale[i],
                    w_branch_pool[i].astype(BF16), w_branch_attn[i].astype(BF16),
                    w_out[i].astype(BF16), ln1_g[i], ln1_b[i], alpha)
        t = h1.reshape(N, D)
        wr_hi, wr_lo = _split_bf16(w_router[i].astype(F32).T)
        idx, wts, rank, cnt, pre = _route(t, wr_hi, wr_lo, router_bias[i],
                                          w_sh_gate[i].astype(BF16), w_sh_up[i].astype(BF16),
                                          w_sh_down[i].astype(BF16), alpha)
        counts, rowstart, blk_e, nact, NB = _dispatch_meta(cnt, N, E)
        dest = _slots(idx, rank, rowstart)
        x_sorted = _dispatch(h1_tiles.reshape(N * SUBLANES, LANES), dest, rowstart, counts, nact, NB)
        y_sorted = _experts(x_sorted, blk_e, nact, NB, w_exp_gate[i], w_exp_up[i], w_exp_down[i])
        h = _combine(y_sorted, dest, wts, pre, ln2_g[i], ln2_b[i]).reshape(B, S, D)
    return h
```

```python
import functools
import math

import numpy as np
import jax
import jax.numpy as jnp
from jax import lax
from jax.experimental import pallas as pl
from jax.experimental.pallas import tpu as pltpu

F32 = jnp.float32
BF16 = jnp.bfloat16

POOL_WINDOWS = (2, 4, 8, 16)
N_DIFF_HEADS = 4
DIFF_HEAD_DIM = 64
REL_BUCKETS = 32
REL_MAX_DIST = 128
TOP_K = 8
N_EXPERT_GROUPS = 8
TOPK_GROUPS = 4
ROUTED_SCALE = 2.5
LN_EPS = 1e-5
LOG2E = math.log2(math.e)

LANES = 128
SUBLANES = 8
VMEM_LIMIT_BYTES = 56 * 1024 * 1024

TOKEN_TILE = 512
MERGE_TILE = 256
EXPERT_BLOCK = 256

_NT = (((1,), (1,)), ((), ()))


def _lambda_init(layer_idx):
    return 0.8 - 0.6 * math.exp(-0.3 * layer_idx)


def _dot(a, b):
    return jnp.dot(a, b, preferred_element_type=F32)


def _split_bf16(a):
    hi = a.astype(BF16)
    lo = (a - hi.astype(F32)).astype(BF16)
    return hi, lo


def _sigmoid(z):
    return 1.0 / (1.0 + jnp.exp(-z))


def _layer_norm(z, g, b):
    mu = jnp.mean(z, axis=-1, keepdims=True)
    zc = z - mu
    var = jnp.mean(zc * zc, axis=-1, keepdims=True)
    return zc * lax.rsqrt(var + LN_EPS) * g + b


def _params(*sem):
    return pltpu.CompilerParams(dimension_semantics=sem, vmem_limit_bytes=VMEM_LIMIT_BYTES)


def _const_spec(shape):
    nd = len(shape)
    return pl.BlockSpec(shape, lambda *_: (0,) * nd)


def _to_row_tiles(ref, x):
    rows = x.shape[0]
    for s in range(SUBLANES):
        ref[pl.ds(s, rows, stride=SUBLANES), :] = x[:, s * LANES:(s + 1) * LANES]


def _from_row_tiles(ref, start, rows):
    return jnp.concatenate([ref[pl.ds(start + s, rows, stride=SUBLANES), :]
                            for s in range(SUBLANES)], axis=1)


def _inproj_kernel(x_ref, w_ref, u_ref, qk_ref, vt_ref, *, pool_w, attn_w, q_scale):
    xb = x_ref[...].astype(BF16)
    p = _dot(xb, w_ref[...])
    u_ref[...] = p[:, :pool_w]
    q = p[:, pool_w:pool_w + attn_w] * q_scale
    k = p[:, pool_w + attn_w:pool_w + 2 * attn_w]
    qk_ref[:, :attn_w] = q.astype(BF16)
    qk_ref[:, attn_w:] = k.astype(BF16)
    v = p[:, pool_w + 2 * attn_w:pool_w + 3 * attn_w]
    vt_ref[...] = v.T.astype(BF16)


def _inproj(x, w_uqkv, pool_w, attn_w):
    B, S, D = x.shape
    T = TOKEN_TILE
    ns = S // T
    width = w_uqkv.shape[1]
    kern = functools.partial(_inproj_kernel, pool_w=pool_w, attn_w=attn_w,
                             q_scale=DIFF_HEAD_DIM ** -0.5 * LOG2E)
    return pl.pallas_call(
        kern,
        grid=(B, ns),
        in_specs=[pl.BlockSpec((None, T, D), lambda b, s: (b, s, 0)),
                  _const_spec((D, width))],
        out_specs=[pl.BlockSpec((None, T, pool_w), lambda b, s: (b, s, 0)),
                   pl.BlockSpec((None, T, 2 * attn_w), lambda b, s: (b, s, 0)),
                   pl.BlockSpec((None, None, attn_w, T), lambda b, s: (b, s, 0, 0))],
        out_shape=[jax.ShapeDtypeStruct((B, S, pool_w), F32),
                   jax.ShapeDtypeStruct((B, S, 2 * attn_w), BF16),
                   jax.ShapeDtypeStruct((B, ns, attn_w, T), BF16)],
        compiler_params=_params("parallel", "parallel"),
        name="inproj",
    )(x, w_uqkv)


def _t5_bucket_np(rel):
    half = REL_BUCKETS // 2
    max_exact = half // 2
    ret = np.where(rel > 0, half, 0)
    n = np.abs(rel)
    nf = np.maximum(n, 1).astype(np.float32)
    large = max_exact + (np.log(nf / max_exact) / math.log(REL_MAX_DIST / max_exact)
                         * (half - max_exact)).astype(np.int32)
    large = np.minimum(large, half - 1)
    return ret + np.where(n < max_exact, n, large)


def _t5_bucket(rel):
    half = REL_BUCKETS // 2
    max_exact = half // 2
    ret = jnp.where(rel > 0, half, 0)
    n = jnp.abs(rel)
    nf = jnp.maximum(n, 1).astype(F32)
    large = max_exact + (jnp.log(nf / max_exact) / math.log(REL_MAX_DIST / max_exact)
                         * (half - max_exact)).astype(jnp.int32)
    large = jnp.minimum(large, half - 1)
    return ret + jnp.where(n < max_exact, n, large)


def _bias_tables(table, S, T):
    far = np.arange(T + 1, S)
    if far.size:
        assert np.all(_t5_bucket_np(-far) == _t5_bucket_np(-far[-1]))
        assert np.all(_t5_bucket_np(far) == _t5_bucket_np(far[-1]))
    rel = jnp.arange(-(S - 1), S, dtype=jnp.int32)
    dist_bias = table[_t5_bucket(rel)].astype(F32).T
    pad = jnp.pad(dist_bias, ((0, 0), (2 * T, 2 * T)), mode='edge')
    diag = jnp.stack([lax.slice_in_dim(pad, (d - 1) * T - (T - 1) + S - 1 + 2 * T,
                                       (d - 1) * T + T + S - 1 + 2 * T, axis=1)
                      for d in range(3)], axis=1)
    w = jnp.concatenate([diag[..., ::-1], diag[..., :1]], axis=-1)
    flat = jnp.tile(w, (1, 1, T))[..., :T * (2 * T - 1)]
    band = flat.reshape(w.shape[0], 3, T, 2 * T - 1)[..., T - 1:]
    far_c = jnp.stack([dist_bias[:, 0], dist_bias[:, 2 * S - 2]], axis=1)
    return band * LOG2E, far_c * LOG2E


def _attn_kernel(far_ref, q_ref, k_ref, vt_ref, band_ref, lq1_ref, lk1_ref, lq2_ref, lk2_ref,
                 gain_ref, o_ref, qz_ref, s00, s01, s10, s11, e00, e01, e10, e11, acc0, acc1,
                 mt_ref, al_ref, m_ref, l_ref, *, T, nk, lam_init):
    h = pl.program_id(1)
    i = pl.program_id(2)
    s_buf = ((s00, s01), (s10, s11))
    e_buf = ((e00, e01), (e10, e11))
    acc = (acc0, acc1)
    c_left = far_ref[h, 0]
    c_right = far_ref[h, 1]

    q = q_ref[...]
    lane = lax.broadcasted_iota(jnp.int32, q.shape, 1)
    zero = jnp.zeros_like(q)
    qz_ref[0] = jnp.where(lane < DIFF_HEAD_DIM, q, zero)
    qz_ref[1] = jnp.where(lane >= DIFF_HEAD_DIM, q, zero)
    m_ref[...] = jnp.full(m_ref.shape, -jnp.inf, F32)
    l_ref[...] = jnp.zeros(l_ref.shape, F32)
    acc0[...] = jnp.zeros(acc0.shape, F32)
    acc1[...] = jnp.zeros(acc1.shape, F32)

    has_prev = i >= 1
    has_next = i <= nk - 2
    n_left = jnp.maximum(i - 1, 0) - jnp.where(has_next, 0, 1)
    right0 = i + 2 + jnp.where(has_prev, 0, 1)

    def tile_of(p):
        if isinstance(p, int) and p == 0:
            return i, band_ref[1], None
        if isinstance(p, int) and p == 1:
            return (jnp.where(has_prev, i - 1, i + 2),
                    jnp.where(has_prev, band_ref[0], c_right), None)
        if isinstance(p, int) and p == 2:
            return (jnp.where(has_next, i + 1, i - 2),
                    jnp.where(has_next, band_ref[2], c_left), None)
        f = p - 3
        is_left = f < n_left
        return (jnp.where(is_left, f, f - n_left + right0), None,
                jnp.where(is_left, c_left, c_right))

    def stage_a(p, x):
        j, bias, const = tile_of(p)
        kt = k_ref[pl.ds(pl.multiple_of(j * T, T), T), :]
        for c in range(2):
            s = lax.dot_general(kt, qz_ref[c], _NT, preferred_element_type=F32)
            if bias is not None:
                s = s + bias
            s_buf[x][c][...] = s
            mt = jnp.max(s, axis=0, keepdims=True)
            mt_ref[2 * x + c] = mt if const is None else mt + const

    def stage_b(p, x):
        _, _, const = tile_of(p)
        for c in range(2):
            m_old = m_ref[c]
            m_new = jnp.maximum(m_old, mt_ref[2 * x + c])
            alpha = jnp.exp2(m_old - m_new)
            shift = m_new if const is None else m_new - const
            e = jnp.exp2(s_buf[x][c][...] - shift)
            l_ref[c] = alpha * l_ref[c] + jnp.sum(e, axis=0, keepdims=True)
            e_buf[x][c][...] = e.astype(BF16)
            al_ref[2 * x + c] = alpha
            m_ref[c] = m_new

    def stage_c(p, x):
        j, _, _ = tile_of(p)
        vt = vt_ref[j]
        for c in range(2):
            acc[c][...] = al_ref[2 * x + c] * acc[c][...] + _dot(vt, e_buf[x][c][...])

    def iteration(n, parity):
        static = isinstance(n, int)
        if not static or n < nk:
            stage_a(n, parity)
        if not static or 1 <= n <= nk:
            stage_b(n - 1, 1 - parity)
        if not static or 2 <= n <= nk + 1:
            stage_c(n - 2, parity)

    head = 5
    for n in range(head):
        iteration(n, n % 2)

    def pair(it, carry):
        n = head + 2 * it
        iteration(n, head % 2)
        iteration(n + 1, 1 - head % 2)
        return carry

    n_pairs = (nk - 1 - head) // 2
    lax.fori_loop(0, n_pairs, pair, 0)
    for n in range(head + 2 * n_pairs, nk + 2):
        iteration(n, n % 2)

    lam = (jnp.exp(jnp.sum(lq1_ref[...] * lk1_ref[...], keepdims=True))
           - jnp.exp(jnp.sum(lq2_ref[...] * lk2_ref[...], keepdims=True)) + lam_init)
    o = acc0[...] / l_ref[0] - lam * (acc1[...] / l_ref[1])
    ms = jnp.mean(o * o, axis=0, keepdims=True)
    y = o * lax.rsqrt(ms + LN_EPS) * gain_ref[...] * (1.0 - lam_init)
    o_ref[...] = y.T.astype(BF16)


def _attention(qk, vt, band, far_c, lq1, lk1, lq2, lk2, gain, lam_init):
    B, S, _ = qk.shape
    T = TOKEN_TILE
    nk = S // T
    assert nk >= 6, "the pipeline prologue assumes at least three far key tiles"
    H = N_DIFF_HEADS
    hw = 2 * DIFF_HEAD_DIM
    kern = functools.partial(_attn_kernel, T=T, nk=nk, lam_init=lam_init)
    vec = lambda a: a.reshape(1, DIFF_HEAD_DIM).astype(F32)
    return pl.pallas_call(
        kern,
        grid=(B, H, nk),
        in_specs=[pl.BlockSpec(memory_space=pltpu.SMEM),
                  pl.BlockSpec((None, T, hw), lambda b, h, i: (b, i, h)),
                  pl.BlockSpec((None, S, hw), lambda b, h, i: (b, 0, H + h)),
                  pl.BlockSpec((None, nk, hw, T), lambda b, h, i: (b, 0, h, 0)),
                  pl.BlockSpec((None, 3, T, T), lambda b, h, i: (h, 0, 0, 0)),
                  _const_spec((1, DIFF_HEAD_DIM)), _const_spec((1, DIFF_HEAD_DIM)),
                  _const_spec((1, DIFF_HEAD_DIM)), _const_spec((1, DIFF_HEAD_DIM)),
                  _const_spec((hw, 1))],
        out_specs=pl.BlockSpec((None, T, hw), lambda b, h, i: (b, i, h)),
        out_shape=jax.ShapeDtypeStruct((B, S, H * hw), BF16),
        scratch_shapes=([pltpu.VMEM((2, T, hw), BF16)]
                        + [pltpu.VMEM((T, T), F32)] * 4 + [pltpu.VMEM((T, T), BF16)] * 4
                        + [pltpu.VMEM((hw, T), F32)] * 2
                        + [pltpu.VMEM((4, 1, T), F32), pltpu.VMEM((4, 1, T), F32),
                           pltpu.VMEM((2, 1, T), F32), pltpu.VMEM((2, 1, T), F32)]),
        compiler_params=_params("parallel", "parallel", "parallel"),
        name="diff_attn",
    )(far_c, qk, qk, vt, band, vec(lq1), vec(lk1), vec(lq2), vec(lk2),
      gain.reshape(hw, 1).astype(F32))


def _pool_matrices(TM):
    r = np.arange(TM)[:, None]
    c = np.arange(TM)[None, :]
    hpos = np.concatenate([np.arange(-SUBLANES, 0), np.arange(TM, TM + SUBLANES),
                           np.full(LANES - 2 * SUBLANES, 10 ** 6)])[None, :]
    main, halo = [], []
    for w in POOL_WINDOWS:
        lo, hi = r - w // 2, r + w - w // 2
        main.append((c >= lo) & (c < hi))
        halo.append((hpos >= lo) & (hpos < hi))
    return (jnp.asarray(np.stack(main), BF16), jnp.asarray(np.stack(halo), BF16))


def _merge_kernel(x_ref, u_ref, up_ref, un_ref, o_ref, wg_ref, bg_ref, am_ref, ah_ref, wgrp_ref,
                  ps_ref, wbp_ref, wba_ref, wo_ref, lng_ref, lnb_ref, h_ref, ht_ref, *, TM, S, alpha):
    s_idx = pl.program_id(1)
    ns = pl.num_programs(1)
    x = x_ref[...]
    D = x.shape[1]
    xb = x.astype(BF16)
    gates = _sigmoid(_dot(xb, wg_ref[...]) + bg_ref[...])

    u = u_ref[...]
    P = u.shape[1]
    prev = jnp.where(s_idx > 0, up_ref[...], 0.0)
    nxt = jnp.where(s_idx < ns - 1, un_ref[...], 0.0)
    halo = jnp.concatenate([prev, nxt, jnp.zeros((LANES - 2 * SUBLANES, P), F32)], axis=0)
    u_hi, u_lo = _split_bf16(u)
    h_hi, h_lo = _split_bf16(halo)
    pos = s_idx * TM + lax.broadcasted_iota(jnp.int32, (TM, 1), 0)
    G = P // len(POOL_WINDOWS)
    mixed = []
    for g, w in enumerate(POOL_WINDOWS):
        sl = slice(g * G, (g + 1) * G)
        win2 = (_dot(am_ref[g], jnp.concatenate([u_hi[:, sl], u_lo[:, sl]], axis=1))
                + _dot(ah_ref[g], jnp.concatenate([h_hi[:, sl], h_lo[:, sl]], axis=1)))
        win = win2[:, :G] + win2[:, G:]
        cnt = (jnp.minimum(pos + (w - w // 2), S) - jnp.maximum(pos - w // 2, 0)).astype(F32)
        pooled = win / cnt - u[:, sl]
        mixed.append(_dot(pooled.astype(BF16), wgrp_ref[g]))
    mix = jnp.concatenate(mixed, axis=1) * ps_ref[...]
    y_pool = _dot(mix.astype(BF16), wbp_ref[...])
    y_attn = _dot(o_ref[...], wba_ref[...])
    merged = gates[:, :D] * y_pool + gates[:, D:] * y_attn
    y = _dot(merged.astype(BF16), wo_ref[...])
    h = _layer_norm(alpha * x + y, lng_ref[...], lnb_ref[...])
    h_ref[...] = h
    _to_row_tiles(ht_ref, h)


def _merge(x, u, o, w_gate, b_gate, w_grp, pool_scale, w_bp, w_ba, w_out, ln_g, ln_b, alpha):
    B, S, D = x.shape
    P = u.shape[2]
    A = o.shape[2]
    TM = MERGE_TILE
    ns = S // TM
    hb = TM // SUBLANES
    am, ah = _pool_matrices(TM)
    nw = len(POOL_WINDOWS)
    G = P // nw
    row = lambda a: a.reshape(1, -1).astype(F32)
    kern = functools.partial(_merge_kernel, TM=TM, S=S, alpha=alpha)
    return pl.pallas_call(
        kern,
        grid=(B, ns),
        in_specs=[pl.BlockSpec((None, TM, D), lambda b, s: (b, s, 0)),
                  pl.BlockSpec((None, TM, P), lambda b, s: (b, s, 0)),
                  pl.BlockSpec((None, SUBLANES, P),
                               lambda b, s: (b, jnp.maximum(s * hb - 1, 0), 0)),
                  pl.BlockSpec((None, SUBLANES, P),
                               lambda b, s: (b, jnp.minimum((s + 1) * hb, S // SUBLANES - 1), 0)),
                  pl.BlockSpec((None, TM, A), lambda b, s: (b, s, 0)),
                  _const_spec((D, 2 * D)), _const_spec((1, 2 * D)),
                  _const_spec((nw, TM, TM)), _const_spec((nw, TM, LANES)),
                  _const_spec((nw, G, G)), _const_spec((1, P)),
                  _const_spec((P, D)), _const_spec((A, D)), _const_spec((D, D)),
                  _const_spec((1, D)), _const_spec((1, D))],
        out_specs=[pl.BlockSpec((None, TM, D), lambda b, s: (b, s, 0)),
                   pl.BlockSpec((None, TM * SUBLANES, LANES), lambda b, s: (b, s, 0))],
        out_shape=[jax.ShapeDtypeStruct((B, S, D), F32),
                   jax.ShapeDtypeStruct((B, S * SUBLANES, LANES), F32)],
        compiler_params=_params("parallel", "parallel"),
        name="merge",
    )(x, u, u, u, o, w_gate, row(b_gate), am, ah, w_grp, row(pool_scale), w_bp, w_ba, w_out,
      row(ln_g), row(ln_b))


def _route_kernel(h_ref, wrh_ref, wrl_ref, rb_ref, tri_ref, ones_ref, wsg_ref, wsu_ref, wsd_ref,
                  idx_ref, wt_ref, rank_ref, cnt_ref, pre_ref, cnt_acc, *, TM, E, alpha):
    t = pl.program_id(0)

    @pl.when(t == 0)
    def _():
        cnt_acc[...] = jnp.zeros(cnt_acc.shape, F32)

    h = h_ref[...]
    hb, hl = _split_bf16(h)
    logits = (lax.dot_general(wrh_ref[...], hb, _NT, preferred_element_type=F32)
              + lax.dot_general(wrh_ref[...], hl, _NT, preferred_element_type=F32)
              + lax.dot_general(wrl_ref[...], hb, _NT, preferred_element_type=F32))
    s = _sigmoid(logits)
    biased = s + rb_ref[...]
    gsz = E // N_EXPERT_GROUPS
    sub = lax.broadcasted_iota(jnp.int32, (gsz, TM), 0).astype(F32)
    neg = -jnp.inf
    grp, gscore = [], []
    for g in range(N_EXPERT_GROUPS):
        bg = biased[g * gsz:(g + 1) * gsz, :]
        m1 = jnp.max(bg, axis=0, keepdims=True)
        first = jnp.min(jnp.where(bg == m1, sub, float(gsz)), axis=0, keepdims=True)
        m2 = jnp.max(jnp.where(sub == first, neg, bg), axis=0, keepdims=True)
        grp.append(bg)
        gscore.append(m1 + m2)
    masked = []
    for g in range(N_EXPERT_GROUPS):
        beaten = jnp.zeros((1, TM), F32)
        for g2 in range(N_EXPERT_GROUPS):
            if g2 == g:
                continue
            wins = (gscore[g2] > gscore[g]) if g2 > g else (gscore[g2] >= gscore[g])
            beaten = beaten + wins.astype(F32)
        masked.append(jnp.where(beaten < float(TOPK_GROUPS), grp[g], neg))
    masked = jnp.concatenate(masked, axis=0)

    row = lax.broadcasted_iota(jnp.int32, (E, TM), 0).astype(F32)
    sel = jnp.zeros((E, TM), F32)
    firsts, ws = [], []
    for _ in range(TOP_K):
        mx = jnp.max(masked, axis=0, keepdims=True)
        first = jnp.min(jnp.where(masked == mx, row, float(E)), axis=0, keepdims=True)
        oh = row == first
        ws.append(jnp.sum(jnp.where(oh, s, 0.0), axis=0, keepdims=True))
        masked = jnp.where(oh, neg, masked)
        sel = jnp.where(oh, 1.0, sel)
        firsts.append(first)
    wsum = ws[0]
    for w in ws[1:]:
        wsum = wsum + w

    sel_b = sel.astype(BF16)
    base = jnp.concatenate([cnt_acc[...]] * (TM // LANES), axis=1)
    rank_all = _dot(sel_b, tri_ref[...]) + base
    cnt_acc[...] = cnt_acc[...] + _dot(sel_b, ones_ref[...])
    cnt_ref[...] = cnt_acc[...]
    for k in range(TOP_K):
        oh = row == firsts[k]
        idx_k = firsts[k].astype(jnp.int32)
        wt_k = ws[k] / wsum * ROUTED_SCALE
        rank_k = jnp.sum(jnp.where(oh, rank_all, 0.0), axis=0, keepdims=True).astype(jnp.int32)
        for c in range(TM // LANES):
            sl = slice(c * LANES, (c + 1) * LANES)
            idx_ref[c, k:k + 1, :] = idx_k[:, sl]
            wt_ref[c, k:k + 1, :] = wt_k[:, sl]
            rank_ref[c, k:k + 1, :] = rank_k[:, sl]

    g_act = _dot(hb, wsg_ref[...])
    hid = g_act * _sigmoid(g_act) * _dot(hb, wsu_ref[...])
    pre_ref[...] = alpha * h + _dot(hid.astype(BF16), wsd_ref[...])


def _route(h, wr_t_hi, wr_t_lo, router_bias, w_sg, w_su, w_sd, alpha):
    N, D = h.shape
    E = wr_t_hi.shape[0]
    TM = TOKEN_TILE
    Hs = w_sg.shape[1]
    tri = jnp.asarray(np.triu(np.ones((TM, TM), np.float32), k=1), BF16)
    ones = jnp.ones((TM, LANES), BF16)
    kern = functools.partial(_route_kernel, TM=TM, E=E, alpha=alpha)
    kt = pl.BlockSpec((TM // LANES, TOP_K, LANES), lambda t: (t, 0, 0))
    return pl.pallas_call(
        kern,
        grid=(N // TM,),
        in_specs=[pl.BlockSpec((TM, D), lambda t: (t, 0)),
                  _const_spec((E, D)), _const_spec((E, D)), _const_spec((E, 1)),
                  _const_spec((TM, TM)), _const_spec((TM, LANES)),
                  _const_spec((D, Hs)), _const_spec((D, Hs)), _const_spec((Hs, D))],
        out_specs=[kt, kt, kt, _const_spec((E, LANES)), pl.BlockSpec((TM, D), lambda t: (t, 0))],
        out_shape=[jax.ShapeDtypeStruct((N // LANES, TOP_K, LANES), jnp.int32),
                   jax.ShapeDtypeStruct((N // LANES, TOP_K, LANES), F32),
                   jax.ShapeDtypeStruct((N // LANES, TOP_K, LANES), jnp.int32),
                   jax.ShapeDtypeStruct((E, LANES), F32),
                   jax.ShapeDtypeStruct((N, D), F32)],
        scratch_shapes=[pltpu.VMEM((E, LANES), F32)],
        compiler_params=_params("arbitrary"),
        name="route_shared",
    )(h, wr_t_hi, wr_t_lo, router_bias.reshape(E, 1).astype(F32), tri, ones, w_sg, w_su, w_sd)


def _slots_kernel(idx_ref, rank_ref, rowstart_ref, dest_ref, *, E):
    expert = lax.broadcasted_iota(jnp.int32, (E, LANES), 0)
    rowstart = rowstart_ref[...]
    for c in range(idx_ref.shape[0]):
        idx = idx_ref[c]
        rank = rank_ref[c]
        for k in range(TOP_K):
            base = jnp.sum(jnp.where(expert == idx[k:k + 1, :], rowstart, 0.0), axis=0,
                           keepdims=True)
            dest_ref[c, k:k + 1, :] = base.astype(jnp.int32) + rank[k:k + 1, :]


def _slots(idx, rank, rowstart):
    nb = idx.shape[0]
    E = rowstart.shape[0]
    step = SUBLANES
    spec = pl.BlockSpec((step, TOP_K, LANES), lambda t: (t, 0, 0))
    return pl.pallas_call(
        functools.partial(_slots_kernel, E=E),
        grid=(nb // step,),
        in_specs=[spec, spec, _const_spec((E, 1))],
        out_specs=spec,
        out_shape=jax.ShapeDtypeStruct(idx.shape, jnp.int32),
        compiler_params=_params("parallel"),
        name="slots",
    )(idx, rank, rowstart.astype(F32).reshape(E, 1))


def _dispatch_kernel(rowstart_ref, cnt_ref, nact_ref, dest_ref, h_ref, xs_hbm,
                     stage, zeros, sem, zsem, *, TM, BLK, NB, E):
    t = pl.program_id(0)
    nt = pl.num_programs(0)
    slot = t % 2
    R = SUBLANES

    def zero_copy(first, n, start):
        cp = pltpu.make_async_copy(zeros.at[pl.ds(0, n * R)],
                                   xs_hbm.at[pl.ds(pl.multiple_of(first * R, R), n * R)], zsem)
        cp.start() if start else cp.wait()

    def wait_slot(s):
        for _ in range(TOP_K):
            pltpu.make_async_copy(stage.at[s], xs_hbm.at[pl.ds(0, TM * R)], sem.at[s]).wait()

    @pl.when(t == 0)
    def _():
        zeros[...] = jnp.zeros(zeros.shape, F32)
        nact = nact_ref[0]

        def tail(e, c, *, start):
            first = rowstart_ref[e] + cnt_ref[e]
            n_pad = (cnt_ref[e] + BLK - 1) // BLK * BLK - cnt_ref[e]
            done = jnp.int32(0)
            size = BLK // 2
            while size >= 1:
                piece = n_pad & size

                @pl.when(piece != 0)
                def _(size=size, done=done):
                    zero_copy(first + done, size, start)

                done = done + piece
                size //= 2
            return c

        def idle(b, c, *, start):
            zero_copy(b * BLK, BLK, start)
            return c

        for start in (True, False):
            lax.fori_loop(0, E, functools.partial(tail, start=start), 0)
            lax.fori_loop(nact, NB, functools.partial(idle, start=start), 0)

    @pl.when(t >= 2)
    def _():
        wait_slot(slot)

    stage[slot] = h_ref[...]
    for c in range(TM // LANES):
        def row_body(r, carry, c=c):
            src = pl.multiple_of((c * LANES + r) * R, R)
            for k in range(TOP_K):
                dst = pl.multiple_of(dest_ref[c, k, r] * R, R)
                pltpu.make_async_copy(stage.at[slot, pl.ds(src, R)], xs_hbm.at[pl.ds(dst, R)],
                                      sem.at[slot]).start(priority=k % 2)
            return carry

        lax.fori_loop(0, LANES, row_body, 0)

    @pl.when(t == nt - 1)
    def _():
        @pl.when(t >= 1)
        def _():
            wait_slot(1 - slot)

        wait_slot(slot)


def _dispatch(h_tiles, dest, rowstart, counts, nact, NB):
    N = h_tiles.shape[0] // SUBLANES
    E = rowstart.shape[0]
    TM = MERGE_TILE
    BLK = EXPERT_BLOCK
    R = SUBLANES
    kern = functools.partial(_dispatch_kernel, TM=TM, BLK=BLK, NB=NB, E=E)
    grid_spec = pltpu.PrefetchScalarGridSpec(
        num_scalar_prefetch=3,
        grid=(N // TM,),
        in_specs=[pl.BlockSpec((TM // LANES, TOP_K, LANES), lambda t, *_: (t, 0, 0),
                               memory_space=pltpu.SMEM),
                  pl.BlockSpec((TM * R, LANES), lambda t, *_: (t, 0))],
        out_specs=pl.BlockSpec(memory_space=pl.ANY),
        scratch_shapes=[pltpu.VMEM((2, TM * R, LANES), F32), pltpu.VMEM((BLK * R, LANES), F32),
                        pltpu.SemaphoreType.DMA((2,)), pltpu.SemaphoreType.DMA(())],
    )
    return pl.pallas_call(
        kern,
        grid_spec=grid_spec,
        out_shape=jax.ShapeDtypeStruct((NB * BLK * R, LANES), F32),
        compiler_params=_params("arbitrary"),
        name="dispatch",
    )(rowstart, counts, nact, dest, h_tiles)


def _expert_kernel(first_ref, nblk_ref, nact_ref, wg_ref, wu_ref, wd_ref, x_hbm, y_hbm,
                   xbuf, ybuf, wg_s, wu_s, wd_s, xsem, ysem, *, BLK, NB):
    e = pl.program_id(0)
    nact = nact_ref[0]
    rows = BLK * SUBLANES

    def block(ref, g):
        return ref.at[pl.ds(pl.multiple_of(g * rows, rows), rows)]

    def x_copy(g, slot):
        return pltpu.make_async_copy(block(x_hbm, g), xbuf.at[slot], xsem.at[slot])

    def y_copy(g, slot):
        return pltpu.make_async_copy(ybuf.at[slot], block(y_hbm, g), ysem.at[slot])

    @pl.when(jnp.logical_and(e == 0, nact > 0))
    def _():
        x_copy(0, 0).start()

    @pl.when(nblk_ref[e] > 0)
    def _():
        wg_s[...] = wg_ref[...].astype(BF16)
        wu_s[...] = wu_ref[...].astype(BF16)
        wd_s[...] = wd_ref[...].astype(BF16)

    def body(j, carry):
        g = first_ref[e] + j
        slot = g % 2
        x_copy(g, slot).wait()

        @pl.when(g + 1 < nact)
        def _():
            x_copy(g + 1, 1 - slot).start()

        @pl.when(g >= 2)
        def _():
            y_copy(g - 2, slot).wait()

        x = _from_row_tiles(xbuf.at[slot], 0, BLK).astype(BF16)
        g_act = _dot(x, wg_s[...])
        hid = g_act * _sigmoid(g_act) * _dot(x, wu_s[...])
        _to_row_tiles(ybuf.at[slot], _dot(hid.astype(BF16), wd_s[...]))
        y_copy(g, slot).start()
        return carry

    lax.fori_loop(0, nblk_ref[e], body, 0)

    @pl.when(e == pl.num_programs(0) - 1)
    def _():
        for back in (2, 1):
            @pl.when(nact >= back)
            def _(back=back):
                y_copy(nact - back, (nact - back) % 2).wait()

        ybuf[0] = jnp.zeros(ybuf.shape[1:], F32)

        def idle(g, carry, *, start):
            cp = y_copy(g, 0)
            cp.start() if start else cp.wait()
            return carry

        for start in (True, False):
            lax.fori_loop(nact, NB, functools.partial(idle, start=start), 0)


def _experts(x_sorted, first_blk, nblk_e, nact, NB, w_eg, w_eu, w_ed):
    E, D, He = w_eg.shape
    BLK = EXPERT_BLOCK
    R = SUBLANES
    per_expert = lambda shape: pl.BlockSpec((None,) + shape, lambda e, *_: (e, 0, 0))
    grid_spec = pltpu.PrefetchScalarGridSpec(
        num_scalar_prefetch=3,
        grid=(E,),
        in_specs=[per_expert((D, He)), per_expert((D, He)), per_expert((He, D)),
                  pl.BlockSpec(memory_space=pl.ANY)],
        out_specs=pl.BlockSpec(memory_space=pl.ANY),
        scratch_shapes=[pltpu.VMEM((2, BLK * R, LANES), F32), pltpu.VMEM((2, BLK * R, LANES), F32),
                        pltpu.VMEM((D, He), BF16), pltpu.VMEM((D, He), BF16),
                        pltpu.VMEM((He, D), BF16),
                        pltpu.SemaphoreType.DMA((2,)), pltpu.SemaphoreType.DMA((2,))],
    )
    return pl.pallas_call(
        functools.partial(_expert_kernel, BLK=BLK, NB=NB),
        grid_spec=grid_spec,
        out_shape=jax.ShapeDtypeStruct((NB * BLK * R, LANES), F32),
        compiler_params=_params("arbitrary"),
        name="routed_experts",
    )(first_blk, nblk_e, nact, w_eg, w_eu, w_ed, x_sorted)


def _combine_kernel(dest_ref, wt_ref, pre_ref, lng_ref, lnb_ref, y_hbm, o_ref, buf, sem, *, TM):
    t = pl.program_id(0)
    nt = pl.num_programs(0) - 1
    slot = t % 2
    R = SUBLANES

    @pl.when(t < nt)
    def _():
        for k in range(TOP_K):
            for r in range(TM):
                src = pl.multiple_of(dest_ref[0, k, r] * R, R)
                pltpu.make_async_copy(y_hbm.at[pl.ds(src, R)],
                                      buf.at[slot, pl.ds((k * TM + r) * R, R)],
                                      sem.at[slot]).start(priority=r % 2)

    @pl.when(t >= 1)
    def _():
        prev = 1 - slot
        for k in range(TOP_K):
            pltpu.make_async_copy(y_hbm.at[pl.ds(0, TM * R)],
                                  buf.at[prev, pl.ds(k * TM * R, TM * R)], sem.at[prev]).wait()
        w = jnp.concatenate([wt_ref[0], jnp.zeros((TM - TOP_K, TM), F32)], axis=0).T
        z = pre_ref[...]
        rows = buf.at[prev]
        for k in range(TOP_K):
            z = z + w[:, k:k + 1] * _from_row_tiles(rows, k * TM * R, TM)
        o_ref[...] = _layer_norm(z, lng_ref[...], lnb_ref[...])


def _combine(y_sorted, dest, wts, pre, ln_g, ln_b):
    N, D = pre.shape
    TM = LANES
    nt = N // TM
    R = SUBLANES
    kern = functools.partial(_combine_kernel, TM=TM)
    row = lambda a: a.reshape(1, -1).astype(F32)
    ahead = lambda t: (jnp.minimum(t, nt - 1), 0, 0)
    behind3 = lambda t: (jnp.maximum(t - 1, 0), 0, 0)
    behind = lambda t: (jnp.maximum(t - 1, 0), 0)
    return pl.pallas_call(
        kern,
        grid=(nt + 1,),
        in_specs=[pl.BlockSpec((1, TOP_K, TM), ahead, memory_space=pltpu.SMEM),
                  pl.BlockSpec((1, TOP_K, TM), behind3),
                  pl.BlockSpec((TM, D), behind),
                  _const_spec((1, D)), _const_spec((1, D)),
                  pl.BlockSpec(memory_space=pl.ANY)],
        out_specs=pl.BlockSpec((TM, D), behind),
        out_shape=jax.ShapeDtypeStruct((N, D), F32),
        scratch_shapes=[pltpu.VMEM((2, TOP_K * TM * R, LANES), F32),
                        pltpu.SemaphoreType.DMA((2,))],
        compiler_params=_params("arbitrary"),
        name="combine",
    )(dest, wts, pre, row(ln_g), row(ln_b), y_sorted)


def _dispatch_meta(cnt, N, E):
    BLK = EXPERT_BLOCK
    NB = (N * TOP_K + E * (BLK - 1)) // BLK
    counts = cnt[:, 0].astype(jnp.int32)
    nblk_e = (counts + BLK - 1) // BLK
    bend = jnp.cumsum(nblk_e)
    first_blk = (bend - nblk_e).astype(jnp.int32)
    rowstart = first_blk * BLK
    nact = bend[-1:].astype(jnp.int32)
    return counts, rowstart, first_blk, nblk_e.astype(jnp.int32), nact, NB


def kernel(x, rel_bias_table, w_in, b_gate, w_pool_grp, pool_scale, w_branch_pool, lambda_q1,
           lambda_k1, lambda_q2, lambda_k2, subln_gain, w_branch_attn, w_out, ln1_g, ln1_b,
           w_router, router_bias, w_exp_gate, w_exp_up, w_exp_down, w_sh_gate, w_sh_up,
           w_sh_down, ln2_g, ln2_b):
    B, S, D = x.shape
    L = w_in.shape[0]
    E = w_router.shape[2]
    P = pool_scale.shape[1]
    A = w_branch_attn.shape[1]
    N = B * S
    alpha = (2 * L) ** 0.25
    assert S % TOKEN_TILE == 0 and S % MERGE_TILE == 0
    assert A == N_DIFF_HEADS * 2 * DIFF_HEAD_DIM and TOKEN_TILE >= REL_MAX_DIST
    assert D == SUBLANES * LANES, "a row tile holds exactly one (SUBLANES, LANES) tile per row"
    band, far_c = _bias_tables(rel_bias_table, S, TOKEN_TILE)

    h = x
    for i in range(L):
        w_uqkv = w_in[i][:, :P + 3 * A].astype(BF16)
        w_gate = w_in[i][:, P + 3 * A:].astype(BF16)
        u, qk, vt = _inproj(h, w_uqkv, P, A)
        o = _attention(qk, vt, band, far_c, lambda_q1[i], lambda_k1[i], lambda_q2[i],
                       lambda_k2[i], subln_gain[i], _lambda_init(i))
        h1, h1_tiles = _merge(h, u, o, w_gate, b_gate[i], w_pool_grp[i].astype(BF16), pool_scale[i],
                    w_branch_pool[i].astype(BF16), w_branch_attn[i].astype(BF16),
                    w_out[i].astype(BF16), ln1_g[i], ln1_b[i], alpha)
        t = h1.reshape(N, D)
        wr_hi, wr_lo = _split_bf16(w_router[i].astype(F32).T)
        idx, wts, rank, cnt, pre = _route(t, wr_hi, wr_lo, router_bias[i],
                                          w_sh_gate[i].astype(BF16), w_sh_up[i].astype(BF16),
                                          w_sh_down[i].astype(BF16), alpha)
        counts, rowstart, first_blk, nblk_e, nact, NB = _dispatch_meta(cnt, N, E)
        dest = _slots(idx, rank, rowstart)
        x_sorted = _dispatch(h1_tiles.reshape(N * SUBLANES, LANES), dest, rowstart, counts, nact, NB)
        y_sorted = _experts(x_sorted, first_blk, nblk_e, nact, NB, w_exp_gate[i], w_exp_up[i],
                            w_exp_down[i])
        h = _combine(y_sorted, dest, wts, pre, ln2_g[i], ln2_b[i]).reshape(B, S, D)
    return h
```

```python
import functools
import math

import numpy as np
import jax
import jax.numpy as jnp
from jax import lax
from jax.experimental import pallas as pl
from jax.experimental.pallas import tpu as pltpu

F32 = jnp.float32
BF16 = jnp.bfloat16

POOL_WINDOWS = (2, 4, 8, 16)
N_DIFF_HEADS = 4
DIFF_HEAD_DIM = 64
REL_BUCKETS = 32
REL_MAX_DIST = 128
TOP_K = 8
N_EXPERT_GROUPS = 8
TOPK_GROUPS = 4
ROUTED_SCALE = 2.5
LN_EPS = 1e-5
LOG2E = math.log2(math.e)

LANES = 128
SUBLANES = 8
VMEM_LIMIT_BYTES = 56 * 1024 * 1024

TOKEN_TILE = 512
MERGE_TILE = 256
EXPERT_BLOCK = 256
STREAMS = 1
SLOTS = 4

_NT = (((1,), (1,)), ((), ()))


def _lambda_init(layer_idx):
    return 0.8 - 0.6 * math.exp(-0.3 * layer_idx)


def _dot(a, b):
    return jnp.dot(a, b, preferred_element_type=F32)


def _split_bf16(a):
    hi = a.astype(BF16)
    lo = (a - hi.astype(F32)).astype(BF16)
    return hi, lo


def _sigmoid(z):
    return 1.0 / (1.0 + jnp.exp(-z))


def _layer_norm(z, g, b):
    mu = jnp.mean(z, axis=-1, keepdims=True)
    zc = z - mu
    var = jnp.mean(zc * zc, axis=-1, keepdims=True)
    return zc * lax.rsqrt(var + LN_EPS) * g + b


def _params(*sem):
    return pltpu.CompilerParams(dimension_semantics=sem, vmem_limit_bytes=VMEM_LIMIT_BYTES)


def _const_spec(shape):
    nd = len(shape)
    return pl.BlockSpec(shape, lambda *_: (0,) * nd)


def _to_row_tiles(ref, x):
    rows = x.shape[0]
    for s in range(SUBLANES):
        ref[pl.ds(s, rows, stride=SUBLANES), :] = x[:, s * LANES:(s + 1) * LANES]


def _from_row_tiles(ref, start, rows):
    return jnp.concatenate([ref[pl.ds(start + s, rows, stride=SUBLANES), :]
                            for s in range(SUBLANES)], axis=1)


def _inproj_kernel(x_ref, w_ref, u_ref, qk_ref, vt_ref, *, pool_w, attn_w, q_scale):
    xb = x_ref[...].astype(BF16)
    p = _dot(xb, w_ref[...])
    u_ref[...] = p[:, :pool_w]
    q = p[:, pool_w:pool_w + attn_w] * q_scale
    k = p[:, pool_w + attn_w:pool_w + 2 * attn_w]
    qk_ref[:, :attn_w] = q.astype(BF16)
    qk_ref[:, attn_w:] = k.astype(BF16)
    v = p[:, pool_w + 2 * attn_w:pool_w + 3 * attn_w]
    vt_ref[...] = v.T.astype(BF16)


def _inproj(x, w_uqkv, pool_w, attn_w):
    B, S, D = x.shape
    T = TOKEN_TILE
    ns = S // T
    width = w_uqkv.shape[1]
    kern = functools.partial(_inproj_kernel, pool_w=pool_w, attn_w=attn_w,
                             q_scale=DIFF_HEAD_DIM ** -0.5 * LOG2E)
    return pl.pallas_call(
        kern,
        grid=(B, ns),
        in_specs=[pl.BlockSpec((None, T, D), lambda b, s: (b, s, 0)),
                  _const_spec((D, width))],
        out_specs=[pl.BlockSpec((None, T, pool_w), lambda b, s: (b, s, 0)),
                   pl.BlockSpec((None, T, 2 * attn_w), lambda b, s: (b, s, 0)),
                   pl.BlockSpec((None, None, attn_w, T), lambda b, s: (b, s, 0, 0))],
        out_shape=[jax.ShapeDtypeStruct((B, S, pool_w), F32),
                   jax.ShapeDtypeStruct((B, S, 2 * attn_w), BF16),
                   jax.ShapeDtypeStruct((B, ns, attn_w, T), BF16)],
        compiler_params=_params("parallel", "parallel"),
        name="inproj",
    )(x, w_uqkv)


def _t5_bucket_np(rel):
    half = REL_BUCKETS // 2
    max_exact = half // 2
    ret = np.where(rel > 0, half, 0)
    n = np.abs(rel)
    nf = np.maximum(n, 1).astype(np.float32)
    large = max_exact + (np.log(nf / max_exact) / math.log(REL_MAX_DIST / max_exact)
                         * (half - max_exact)).astype(np.int32)
    large = np.minimum(large, half - 1)
    return ret + np.where(n < max_exact, n, large)


def _t5_bucket(rel):
    half = REL_BUCKETS // 2
    max_exact = half // 2
    ret = jnp.where(rel > 0, half, 0)
    n = jnp.abs(rel)
    nf = jnp.maximum(n, 1).astype(F32)
    large = max_exact + (jnp.log(nf / max_exact) / math.log(REL_MAX_DIST / max_exact)
                         * (half - max_exact)).astype(jnp.int32)
    large = jnp.minimum(large, half - 1)
    return ret + jnp.where(n < max_exact, n, large)


def _bias_tables(table, S, T):
    far = np.arange(T + 1, S)
    if far.size:
        assert np.all(_t5_bucket_np(-far) == _t5_bucket_np(-far[-1]))
        assert np.all(_t5_bucket_np(far) == _t5_bucket_np(far[-1]))
    rel = jnp.arange(-(S - 1), S, dtype=jnp.int32)
    dist_bias = table[_t5_bucket(rel)].astype(F32).T
    pad = jnp.pad(dist_bias, ((0, 0), (2 * T, 2 * T)), mode='edge')
    diag = jnp.stack([lax.slice_in_dim(pad, (d - 1) * T - (T - 1) + S - 1 + 2 * T,
                                       (d - 1) * T + T + S - 1 + 2 * T, axis=1)
                      for d in range(3)], axis=1)
    w = jnp.concatenate([diag[..., ::-1], diag[..., :1]], axis=-1)
    flat = jnp.tile(w, (1, 1, T))[..., :T * (2 * T - 1)]
    band = flat.reshape(w.shape[0], 3, T, 2 * T - 1)[..., T - 1:]
    far_c = jnp.stack([dist_bias[:, 0], dist_bias[:, 2 * S - 2]], axis=1)
    return band * LOG2E, far_c * LOG2E


def _attn_kernel(far_ref, q_ref, k_ref, vt_ref, band_ref, lq1_ref, lk1_ref, lq2_ref, lk2_ref,
                 gain_ref, o_ref, qz_ref, s00, s01, s10, s11, e00, e01, e10, e11, acc0, acc1,
                 mt_ref, al_ref, m_ref, l_ref, *, T, nk, lam_init):
    h = pl.program_id(1)
    i = pl.program_id(2)
    s_buf = ((s00, s01), (s10, s11))
    e_buf = ((e00, e01), (e10, e11))
    acc = (acc0, acc1)
    c_left = far_ref[h, 0]
    c_right = far_ref[h, 1]

    q = q_ref[...]
    lane = lax.broadcasted_iota(jnp.int32, q.shape, 1)
    zero = jnp.zeros_like(q)
    qz_ref[0] = jnp.where(lane < DIFF_HEAD_DIM, q, zero)
    qz_ref[1] = jnp.where(lane >= DIFF_HEAD_DIM, q, zero)
    m_ref[...] = jnp.full(m_ref.shape, -jnp.inf, F32)
    l_ref[...] = jnp.zeros(l_ref.shape, F32)
    acc0[...] = jnp.zeros(acc0.shape, F32)
    acc1[...] = jnp.zeros(acc1.shape, F32)

    has_prev = i >= 1
    has_next = i <= nk - 2
    n_left = jnp.maximum(i - 1, 0) - jnp.where(has_next, 0, 1)
    right0 = i + 2 + jnp.where(has_prev, 0, 1)

    def tile_of(p):
        if isinstance(p, int) and p == 0:
            return i, band_ref[1], None
        if isinstance(p, int) and p == 1:
            return (jnp.where(has_prev, i - 1, i + 2),
                    jnp.where(has_prev, band_ref[0], c_right), None)
        if isinstance(p, int) and p == 2:
            return (jnp.where(has_next, i + 1, i - 2),
                    jnp.where(has_next, band_ref[2], c_left), None)
        f = p - 3
        is_left = f < n_left
        return (jnp.where(is_left, f, f - n_left + right0), None,
                jnp.where(is_left, c_left, c_right))

    def stage_a(p, x):
        j, bias, const = tile_of(p)
        kt = k_ref[pl.ds(pl.multiple_of(j * T, T), T), :]
        for c in range(2):
            s = lax.dot_general(kt, qz_ref[c], _NT, preferred_element_type=F32)
            if bias is not None:
                s = s + bias
            s_buf[x][c][...] = s
            mt = jnp.max(s, axis=0, keepdims=True)
            mt_ref[2 * x + c] = mt if const is None else mt + const

    def stage_b(p, x):
        _, _, const = tile_of(p)
        for c in range(2):
            m_old = m_ref[c]
            m_new = jnp.maximum(m_old, mt_ref[2 * x + c])
            alpha = jnp.exp2(m_old - m_new)
            shift = m_new if const is None else m_new - const
            e = jnp.exp2(s_buf[x][c][...] - shift)
            l_ref[c] = alpha * l_ref[c] + jnp.sum(e, axis=0, keepdims=True)
            e_buf[x][c][...] = e.astype(BF16)
            al_ref[2 * x + c] = alpha
            m_ref[c] = m_new

    def stage_c(p, x):
        j, _, _ = tile_of(p)
        vt = vt_ref[j]
        for c in range(2):
            acc[c][...] = al_ref[2 * x + c] * acc[c][...] + _dot(vt, e_buf[x][c][...])

    def iteration(n, parity):
        static = isinstance(n, int)
        if not static or n < nk:
            stage_a(n, parity)
        if not static or 1 <= n <= nk:
            stage_b(n - 1, 1 - parity)
        if not static or 2 <= n <= nk + 1:
            stage_c(n - 2, parity)

    head = 5
    for n in range(head):
        iteration(n, n % 2)

    def pair(it, carry):
        n = head + 2 * it
        iteration(n, head % 2)
        iteration(n + 1, 1 - head % 2)
        return carry

    n_pairs = (nk - 1 - head) // 2
    lax.fori_loop(0, n_pairs, pair, 0)
    for n in range(head + 2 * n_pairs, nk + 2):
        iteration(n, n % 2)

    lam = (jnp.exp(jnp.sum(lq1_ref[...] * lk1_ref[...], keepdims=True))
           - jnp.exp(jnp.sum(lq2_ref[...] * lk2_ref[...], keepdims=True)) + lam_init)
    o = acc0[...] / l_ref[0] - lam * (acc1[...] / l_ref[1])
    ms = jnp.mean(o * o, axis=0, keepdims=True)
    y = o * lax.rsqrt(ms + LN_EPS) * gain_ref[...] * (1.0 - lam_init)
    o_ref[...] = y.T.astype(BF16)


def _attention(qk, vt, band, far_c, lq1, lk1, lq2, lk2, gain, lam_init):
    B, S, _ = qk.shape
    T = TOKEN_TILE
    nk = S // T
    assert nk >= 6, "the pipeline prologue assumes at least three far key tiles"
    H = N_DIFF_HEADS
    hw = 2 * DIFF_HEAD_DIM
    kern = functools.partial(_attn_kernel, T=T, nk=nk, lam_init=lam_init)
    vec = lambda a: a.reshape(1, DIFF_HEAD_DIM).astype(F32)
    return pl.pallas_call(
        kern,
        grid=(B, H, nk),
        in_specs=[pl.BlockSpec(memory_space=pltpu.SMEM),
                  pl.BlockSpec((None, T, hw), lambda b, h, i: (b, i, h)),
                  pl.BlockSpec((None, S, hw), lambda b, h, i: (b, 0, H + h)),
                  pl.BlockSpec((None, nk, hw, T), lambda b, h, i: (b, 0, h, 0)),
                  pl.BlockSpec((None, 3, T, T), lambda b, h, i: (h, 0, 0, 0)),
                  _const_spec((1, DIFF_HEAD_DIM)), _const_spec((1, DIFF_HEAD_DIM)),
                  _const_spec((1, DIFF_HEAD_DIM)), _const_spec((1, DIFF_HEAD_DIM)),
                  _const_spec((hw, 1))],
        out_specs=pl.BlockSpec((None, T, hw), lambda b, h, i: (b, i, h)),
        out_shape=jax.ShapeDtypeStruct((B, S, H * hw), BF16),
        scratch_shapes=([pltpu.VMEM((2, T, hw), BF16)]
                        + [pltpu.VMEM((T, T), F32)] * 4 + [pltpu.VMEM((T, T), BF16)] * 4
                        + [pltpu.VMEM((hw, T), F32)] * 2
                        + [pltpu.VMEM((4, 1, T), F32), pltpu.VMEM((4, 1, T), F32),
                           pltpu.VMEM((2, 1, T), F32), pltpu.VMEM((2, 1, T), F32)]),
        compiler_params=_params("parallel", "parallel", "parallel"),
        name="diff_attn",
    )(far_c, qk, qk, vt, band, vec(lq1), vec(lk1), vec(lq2), vec(lk2),
      gain.reshape(hw, 1).astype(F32))


def _pool_matrices(TM):
    r = np.arange(TM)[:, None]
    c = np.arange(TM)[None, :]
    hpos = np.concatenate([np.arange(-SUBLANES, 0), np.arange(TM, TM + SUBLANES),
                           np.full(LANES - 2 * SUBLANES, 10 ** 6)])[None, :]
    main, halo = [], []
    for w in POOL_WINDOWS:
        lo, hi = r - w // 2, r + w - w // 2
        main.append((c >= lo) & (c < hi))
        halo.append((hpos >= lo) & (hpos < hi))
    return (jnp.asarray(np.stack(main), BF16), jnp.asarray(np.stack(halo), BF16))


def _merge_kernel(x_ref, u_ref, up_ref, un_ref, o_ref, wg_ref, bg_ref, am_ref, ah_ref, wgrp_ref,
                  ps_ref, wbp_ref, wba_ref, wo_ref, lng_ref, lnb_ref, h_ref, ht_ref, *, TM, S, alpha):
    s_idx = pl.program_id(1)
    ns = pl.num_programs(1)
    x = x_ref[...]
    D = x.shape[1]
    xb = x.astype(BF16)
    gates = _sigmoid(_dot(xb, wg_ref[...]) + bg_ref[...])

    u = u_ref[...]
    P = u.shape[1]
    prev = jnp.where(s_idx > 0, up_ref[...], 0.0)
    nxt = jnp.where(s_idx < ns - 1, un_ref[...], 0.0)
    halo = jnp.concatenate([prev, nxt, jnp.zeros((LANES - 2 * SUBLANES, P), F32)], axis=0)
    u_hi, u_lo = _split_bf16(u)
    h_hi, h_lo = _split_bf16(halo)
    pos = s_idx * TM + lax.broadcasted_iota(jnp.int32, (TM, 1), 0)
    G = P // len(POOL_WINDOWS)
    mixed = []
    for g, w in enumerate(POOL_WINDOWS):
        sl = slice(g * G, (g + 1) * G)
        win2 = (_dot(am_ref[g], jnp.concatenate([u_hi[:, sl], u_lo[:, sl]], axis=1))
                + _dot(ah_ref[g], jnp.concatenate([h_hi[:, sl], h_lo[:, sl]], axis=1)))
        win = win2[:, :G] + win2[:, G:]
        cnt = (jnp.minimum(pos + (w - w // 2), S) - jnp.maximum(pos - w // 2, 0)).astype(F32)
        pooled = win / cnt - u[:, sl]
        mixed.append(_dot(pooled.astype(BF16), wgrp_ref[g]))
    mix = jnp.concatenate(mixed, axis=1) * ps_ref[...]
    y_pool = _dot(mix.astype(BF16), wbp_ref[...])
    y_attn = _dot(o_ref[...], wba_ref[...])
    merged = gates[:, :D] * y_pool + gates[:, D:] * y_attn
    y = _dot(merged.astype(BF16), wo_ref[...])
    h = _layer_norm(alpha * x + y, lng_ref[...], lnb_ref[...])
    h_ref[...] = h
    _to_row_tiles(ht_ref, h)


def _merge(x, u, o, w_gate, b_gate, w_grp, pool_scale, w_bp, w_ba, w_out, ln_g, ln_b, alpha):
    B, S, D = x.shape
    P = u.shape[2]
    A = o.shape[2]
    TM = MERGE_TILE
    ns = S // TM
    hb = TM // SUBLANES
    am, ah = _pool_matrices(TM)
    nw = len(POOL_WINDOWS)
    G = P // nw
    row = lambda a: a.reshape(1, -1).astype(F32)
    kern = functools.partial(_merge_kernel, TM=TM, S=S, alpha=alpha)
    return pl.pallas_call(
        kern,
        grid=(B, ns),
        in_specs=[pl.BlockSpec((None, TM, D), lambda b, s: (b, s, 0)),
                  pl.BlockSpec((None, TM, P), lambda b, s: (b, s, 0)),
                  pl.BlockSpec((None, SUBLANES, P),
                               lambda b, s: (b, jnp.maximum(s * hb - 1, 0), 0)),
                  pl.BlockSpec((None, SUBLANES, P),
                               lambda b, s: (b, jnp.minimum((s + 1) * hb, S // SUBLANES - 1), 0)),
                  pl.BlockSpec((None, TM, A), lambda b, s: (b, s, 0)),
                  _const_spec((D, 2 * D)), _const_spec((1, 2 * D)),
                  _const_spec((nw, TM, TM)), _const_spec((nw, TM, LANES)),
                  _const_spec((nw, G, G)), _const_spec((1, P)),
                  _const_spec((P, D)), _const_spec((A, D)), _const_spec((D, D)),
                  _const_spec((1, D)), _const_spec((1, D))],
        out_specs=[pl.BlockSpec((None, TM, D), lambda b, s: (b, s, 0)),
                   pl.BlockSpec((None, TM * SUBLANES, LANES), lambda b, s: (b, s, 0))],
        out_shape=[jax.ShapeDtypeStruct((B, S, D), F32),
                   jax.ShapeDtypeStruct((B, S * SUBLANES, LANES), F32)],
        compiler_params=_params("parallel", "parallel"),
        name="merge",
    )(x, u, u, u, o, w_gate, row(b_gate), am, ah, w_grp, row(pool_scale), w_bp, w_ba, w_out,
      row(ln_g), row(ln_b))


def _route_kernel(h_ref, wrh_ref, wrl_ref, rb_ref, tri_ref, ones_ref, wsg_ref, wsu_ref, wsd_ref,
                  idx_ref, wt_ref, rank_ref, cnt_ref, pre_ref, cnt_acc, *, TM, E, alpha):
    t = pl.program_id(0)

    @pl.when(t == 0)
    def _():
        cnt_acc[...] = jnp.zeros(cnt_acc.shape, F32)

    h = h_ref[...]
    hb, hl = _split_bf16(h)
    logits = (lax.dot_general(wrh_ref[...], hb, _NT, preferred_element_type=F32)
              + lax.dot_general(wrh_ref[...], hl, _NT, preferred_element_type=F32)
              + lax.dot_general(wrl_ref[...], hb, _NT, preferred_element_type=F32))
    s = _sigmoid(logits)
    biased = s + rb_ref[...]
    gsz = E // N_EXPERT_GROUPS
    sub = lax.broadcasted_iota(jnp.int32, (gsz, TM), 0).astype(F32)
    neg = -jnp.inf
    grp, gscore = [], []
    for g in range(N_EXPERT_GROUPS):
        bg = biased[g * gsz:(g + 1) * gsz, :]
        m1 = jnp.max(bg, axis=0, keepdims=True)
        first = jnp.min(jnp.where(bg == m1, sub, float(gsz)), axis=0, keepdims=True)
        m2 = jnp.max(jnp.where(sub == first, neg, bg), axis=0, keepdims=True)
        grp.append(bg)
        gscore.append(m1 + m2)
    masked = []
    for g in range(N_EXPERT_GROUPS):
        beaten = jnp.zeros((1, TM), F32)
        for g2 in range(N_EXPERT_GROUPS):
            if g2 == g:
                continue
            wins = (gscore[g2] > gscore[g]) if g2 > g else (gscore[g2] >= gscore[g])
            beaten = beaten + wins.astype(F32)
        masked.append(jnp.where(beaten < float(TOPK_GROUPS), grp[g], neg))
    masked = jnp.concatenate(masked, axis=0)

    row = lax.broadcasted_iota(jnp.int32, (E, TM), 0).astype(F32)
    sel = jnp.zeros((E, TM), F32)
    firsts, ws = [], []
    for _ in range(TOP_K):
        mx = jnp.max(masked, axis=0, keepdims=True)
        first = jnp.min(jnp.where(masked == mx, row, float(E)), axis=0, keepdims=True)
        oh = row == first
        ws.append(jnp.sum(jnp.where(oh, s, 0.0), axis=0, keepdims=True))
        masked = jnp.where(oh, neg, masked)
        sel = jnp.where(oh, 1.0, sel)
        firsts.append(first)
    wsum = ws[0]
    for w in ws[1:]:
        wsum = wsum + w

    sel_b = sel.astype(BF16)
    base = jnp.concatenate([cnt_acc[...]] * (TM // LANES), axis=1)
    rank_all = _dot(sel_b, tri_ref[...]) + base
    cnt_acc[...] = cnt_acc[...] + _dot(sel_b, ones_ref[...])
    cnt_ref[...] = cnt_acc[...]
    for k in range(TOP_K):
        oh = row == firsts[k]
        idx_k = firsts[k].astype(jnp.int32)
        wt_k = ws[k] / wsum * ROUTED_SCALE
        rank_k = jnp.sum(jnp.where(oh, rank_all, 0.0), axis=0, keepdims=True).astype(jnp.int32)
        for c in range(TM // LANES):
            sl = slice(c * LANES, (c + 1) * LANES)
            idx_ref[c, k:k + 1, :] = idx_k[:, sl]
            wt_ref[c, k:k + 1, :] = wt_k[:, sl]
            rank_ref[c, k:k + 1, :] = rank_k[:, sl]

    g_act = _dot(hb, wsg_ref[...])
    hid = g_act * _sigmoid(g_act) * _dot(hb, wsu_ref[...])
    pre_ref[...] = alpha * h + _dot(hid.astype(BF16), wsd_ref[...])


def _route(h, wr_t_hi, wr_t_lo, router_bias, w_sg, w_su, w_sd, alpha):
    N, D = h.shape
    E = wr_t_hi.shape[0]
    TM = TOKEN_TILE
    Hs = w_sg.shape[1]
    tri = jnp.asarray(np.triu(np.ones((TM, TM), np.float32), k=1), BF16)
    ones = jnp.ones((TM, LANES), BF16)
    kern = functools.partial(_route_kernel, TM=TM, E=E, alpha=alpha)
    kt = pl.BlockSpec((TM // LANES, TOP_K, LANES), lambda t: (t, 0, 0))
    return pl.pallas_call(
        kern,
        grid=(N // TM,),
        in_specs=[pl.BlockSpec((TM, D), lambda t: (t, 0)),
                  _const_spec((E, D)), _const_spec((E, D)), _const_spec((E, 1)),
                  _const_spec((TM, TM)), _const_spec((TM, LANES)),
                  _const_spec((D, Hs)), _const_spec((D, Hs)), _const_spec((Hs, D))],
        out_specs=[kt, kt, kt, _const_spec((E, LANES)), pl.BlockSpec((TM, D), lambda t: (t, 0))],
        out_shape=[jax.ShapeDtypeStruct((N // LANES, TOP_K, LANES), jnp.int32),
                   jax.ShapeDtypeStruct((N // LANES, TOP_K, LANES), F32),
                   jax.ShapeDtypeStruct((N // LANES, TOP_K, LANES), jnp.int32),
                   jax.ShapeDtypeStruct((E, LANES), F32),
                   jax.ShapeDtypeStruct((N, D), F32)],
        scratch_shapes=[pltpu.VMEM((E, LANES), F32)],
        compiler_params=_params("arbitrary"),
        name="route_shared",
    )(h, wr_t_hi, wr_t_lo, router_bias.reshape(E, 1).astype(F32), tri, ones, w_sg, w_su, w_sd)


def _slots_kernel(idx_ref, rank_ref, rowstart_ref, dest_ref, *, E):
    expert = lax.broadcasted_iota(jnp.int32, (E, LANES), 0)
    rowstart = rowstart_ref[...]
    for c in range(idx_ref.shape[0]):
        idx = idx_ref[c]
        rank = rank_ref[c]
        for k in range(TOP_K):
            base = jnp.sum(jnp.where(expert == idx[k:k + 1, :], rowstart, 0.0), axis=0,
                           keepdims=True)
            dest_ref[c, k:k + 1, :] = base.astype(jnp.int32) + rank[k:k + 1, :]


def _slots(idx, rank, rowstart):
    nb = idx.shape[0]
    E = rowstart.shape[0]
    step = SUBLANES
    spec = pl.BlockSpec((step, TOP_K, LANES), lambda t: (t, 0, 0))
    return pl.pallas_call(
        functools.partial(_slots_kernel, E=E),
        grid=(nb // step,),
        in_specs=[spec, spec, _const_spec((E, 1))],
        out_specs=spec,
        out_shape=jax.ShapeDtypeStruct(idx.shape, jnp.int32),
        compiler_params=_params("parallel"),
        name="slots",
    )(idx, rank, rowstart.astype(F32).reshape(E, 1))


def _dispatch_kernel(rowstart_ref, cnt_ref, nact_ref, dest_ref, h_ref, xs_hbm,
                     stage, zeros, sem, zsem, *, TM, BLK, NB, E):
    t = pl.program_id(0)
    nt = pl.num_programs(0)
    slot = t % 2
    R = SUBLANES

    def zero_copy(first, n, start):
        cp = pltpu.make_async_copy(zeros.at[pl.ds(0, n * R)],
                                   xs_hbm.at[pl.ds(pl.multiple_of(first * R, R), n * R)], zsem)
        cp.start() if start else cp.wait()

    def wait_slot(s):
        for _ in range(TOP_K):
            pltpu.make_async_copy(stage.at[s], xs_hbm.at[pl.ds(0, TM * R)], sem.at[s]).wait()

    @pl.when(t == 0)
    def _():
        zeros[...] = jnp.zeros(zeros.shape, F32)
        nact = nact_ref[0]

        def tail(e, c, *, start):
            first = rowstart_ref[e] + cnt_ref[e]
            n_pad = (cnt_ref[e] + BLK - 1) // BLK * BLK - cnt_ref[e]
            done = jnp.int32(0)
            size = BLK // 2
            while size >= 1:
                piece = n_pad & size

                @pl.when(piece != 0)
                def _(size=size, done=done):
                    zero_copy(first + done, size, start)

                done = done + piece
                size //= 2
            return c

        def idle(b, c, *, start):
            zero_copy(b * BLK, BLK, start)
            return c

        for start in (True, False):
            lax.fori_loop(0, E, functools.partial(tail, start=start), 0)
            lax.fori_loop(nact, NB, functools.partial(idle, start=start), 0)

    @pl.when(t >= 2)
    def _():
        wait_slot(slot)

    stage[slot] = h_ref[...]
    for c in range(TM // LANES):
        def row_body(r, carry, c=c):
            src = pl.multiple_of((c * LANES + r) * R, R)
            for k in range(TOP_K):
                dst = pl.multiple_of(dest_ref[c, k, r] * R, R)
                pltpu.make_async_copy(stage.at[slot, pl.ds(src, R)], xs_hbm.at[pl.ds(dst, R)],
                                      sem.at[slot]).start(priority=k % 2)
            return carry

        lax.fori_loop(0, LANES, row_body, 0)

    @pl.when(t == nt - 1)
    def _():
        @pl.when(t >= 1)
        def _():
            wait_slot(1 - slot)

        wait_slot(slot)


def _dispatch(h_tiles, dest, rowstart, counts, nact, NB):
    N = h_tiles.shape[0] // SUBLANES
    E = rowstart.shape[0]
    TM = MERGE_TILE
    BLK = EXPERT_BLOCK
    R = SUBLANES
    kern = functools.partial(_dispatch_kernel, TM=TM, BLK=BLK, NB=NB, E=E)
    grid_spec = pltpu.PrefetchScalarGridSpec(
        num_scalar_prefetch=3,
        grid=(N // TM,),
        in_specs=[pl.BlockSpec((TM // LANES, TOP_K, LANES), lambda t, *_: (t, 0, 0),
                               memory_space=pltpu.SMEM),
                  pl.BlockSpec((TM * R, LANES), lambda t, *_: (t, 0))],
        out_specs=pl.BlockSpec(memory_space=pl.ANY),
        scratch_shapes=[pltpu.VMEM((2, TM * R, LANES), F32), pltpu.VMEM((BLK * R, LANES), F32),
                        pltpu.SemaphoreType.DMA((2,)), pltpu.SemaphoreType.DMA(())],
    )
    return pl.pallas_call(
        kern,
        grid_spec=grid_spec,
        out_shape=jax.ShapeDtypeStruct((NB * BLK * R, LANES), F32),
        compiler_params=_params("arbitrary"),
        name="dispatch",
    )(rowstart, counts, nact, dest, h_tiles)


def _expert_kernel(first_ref, nblk_ref, nact_ref, wg_ref, wu_ref, wd_ref, x_hbm, y_hbm,
                   xbuf, ybuf, wg_s, wu_s, wd_s, xsem, ysem, *, BLK, NB):
    e = pl.program_id(0)
    nact = nact_ref[0]
    rows = BLK * SUBLANES

    class _Parts:
        def __init__(self, hbm, buf, g, slot, sem, to_hbm):
            part = rows // STREAMS
            self.copies = []
            for p in range(STREAMS):
                far = hbm.at[pl.ds(pl.multiple_of(g * rows + p * part, part), part)]
                near = buf.at[slot, pl.ds(p * part, part)]
                src, dst = (near, far) if to_hbm else (far, near)
                self.copies.append(pltpu.make_async_copy(src, dst, sem.at[slot]))

        def start(self):
            for p, cp in enumerate(self.copies):
                cp.start(priority=p % 2)

        def wait(self):
            for cp in self.copies:
                cp.wait()

    def x_copy(g, slot):
        return _Parts(x_hbm, xbuf, g, slot, xsem, False)

    def y_copy(g, slot):
        return _Parts(y_hbm, ybuf, g, slot, ysem, True)

    ahead = SLOTS - 1

    @pl.when(e == 0)
    def _():
        for g0 in range(ahead):
            @pl.when(g0 < nact)
            def _(g0=g0):
                x_copy(g0, g0).start()

    @pl.when(nblk_ref[e] > 0)
    def _():
        wg_s[...] = wg_ref[...].astype(BF16)
        wu_s[...] = wu_ref[...].astype(BF16)
        wd_s[...] = wd_ref[...].astype(BF16)

    def body(j, carry):
        g = first_ref[e] + j
        slot = g % SLOTS
        x_copy(g, slot).wait()

        @pl.when(g + ahead < nact)
        def _():
            x_copy(g + ahead, (g + ahead) % SLOTS).start()

        @pl.when(g >= SLOTS)
        def _():
            y_copy(g - SLOTS, slot).wait()

        x = _from_row_tiles(xbuf.at[slot], 0, BLK).astype(BF16)
        g_act = _dot(x, wg_s[...])
        hid = g_act * _sigmoid(g_act) * _dot(x, wu_s[...])
        _to_row_tiles(ybuf.at[slot], _dot(hid.astype(BF16), wd_s[...]))
        y_copy(g, slot).start()
        return carry

    lax.fori_loop(0, nblk_ref[e], body, 0)

    @pl.when(e == pl.num_programs(0) - 1)
    def _():
        for back in range(SLOTS, 0, -1):
            @pl.when(nact >= back)
            def _(back=back):
                y_copy(nact - back, (nact - back) % SLOTS).wait()

        ybuf[0] = jnp.zeros(ybuf.shape[1:], F32)

        def idle(g, carry, *, start):
            cp = y_copy(g, 0)
            cp.start() if start else cp.wait()
            return carry

        for start in (True, False):
            lax.fori_loop(nact, NB, functools.partial(idle, start=start), 0)


def _experts(x_sorted, first_blk, nblk_e, nact, NB, w_eg, w_eu, w_ed):
    E, D, He = w_eg.shape
    BLK = EXPERT_BLOCK
    R = SUBLANES
    per_expert = lambda shape: pl.BlockSpec((None,) + shape, lambda e, *_: (e, 0, 0))
    grid_spec = pltpu.PrefetchScalarGridSpec(
        num_scalar_prefetch=3,
        grid=(E,),
        in_specs=[per_expert((D, He)), per_expert((D, He)), per_expert((He, D)),
                  pl.BlockSpec(memory_space=pl.ANY)],
        out_specs=pl.BlockSpec(memory_space=pl.ANY),
        scratch_shapes=[pltpu.VMEM((SLOTS, BLK * R, LANES), F32),
                        pltpu.VMEM((SLOTS, BLK * R, LANES), F32),
                        pltpu.VMEM((D, He), BF16), pltpu.VMEM((D, He), BF16),
                        pltpu.VMEM((He, D), BF16),
                        pltpu.SemaphoreType.DMA((SLOTS,)), pltpu.SemaphoreType.DMA((SLOTS,))],
    )
    return pl.pallas_call(
        functools.partial(_expert_kernel, BLK=BLK, NB=NB),
        grid_spec=grid_spec,
        out_shape=jax.ShapeDtypeStruct((NB * BLK * R, LANES), F32),
        compiler_params=_params("arbitrary"),
        name="routed_experts",
    )(first_blk, nblk_e, nact, w_eg, w_eu, w_ed, x_sorted)


def _combine_kernel(dest_ref, wt_ref, pre_ref, lng_ref, lnb_ref, y_hbm, o_ref, buf, sem, *, TM):
    t = pl.program_id(0)
    nt = pl.num_programs(0) - 1
    slot = t % 2
    R = SUBLANES

    @pl.when(t < nt)
    def _():
        for k in range(TOP_K):
            for r in range(TM):
                src = pl.multiple_of(dest_ref[0, k, r] * R, R)
                pltpu.make_async_copy(y_hbm.at[pl.ds(src, R)],
                                      buf.at[slot, pl.ds((k * TM + r) * R, R)],
                                      sem.at[slot]).start(priority=r % 2)

    @pl.when(t >= 1)
    def _():
        prev = 1 - slot
        for k in range(TOP_K):
            pltpu.make_async_copy(y_hbm.at[pl.ds(0, TM * R)],
                                  buf.at[prev, pl.ds(k * TM * R, TM * R)], sem.at[prev]).wait()
        w = jnp.concatenate([wt_ref[0], jnp.zeros((TM - TOP_K, TM), F32)], axis=0).T
        z = pre_ref[...]
        rows = buf.at[prev]
        for k in range(TOP_K):
            z = z + w[:, k:k + 1] * _from_row_tiles(rows, k * TM * R, TM)
        o_ref[...] = _layer_norm(z, lng_ref[...], lnb_ref[...])


def _combine(y_sorted, dest, wts, pre, ln_g, ln_b):
    N, D = pre.shape
    TM = LANES
    nt = N // TM
    R = SUBLANES
    kern = functools.partial(_combine_kernel, TM=TM)
    row = lambda a: a.reshape(1, -1).astype(F32)
    ahead = lambda t: (jnp.minimum(t, nt - 1), 0, 0)
    behind3 = lambda t: (jnp.maximum(t - 1, 0), 0, 0)
    behind = lambda t: (jnp.maximum(t - 1, 0), 0)
    return pl.pallas_call(
        kern,
        grid=(nt + 1,),
        in_specs=[pl.BlockSpec((1, TOP_K, TM), ahead, memory_space=pltpu.SMEM),
                  pl.BlockSpec((1, TOP_K, TM), behind3),
                  pl.BlockSpec((TM, D), behind),
                  _const_spec((1, D)), _const_spec((1, D)),
                  pl.BlockSpec(memory_space=pl.ANY)],
        out_specs=pl.BlockSpec((TM, D), behind),
        out_shape=jax.ShapeDtypeStruct((N, D), F32),
        scratch_shapes=[pltpu.VMEM((2, TOP_K * TM * R, LANES), F32),
                        pltpu.SemaphoreType.DMA((2,))],
        compiler_params=_params("arbitrary"),
        name="combine",
    )(dest, wts, pre, row(ln_g), row(ln_b), y_sorted)


def _dispatch_meta(cnt, N, E):
    BLK = EXPERT_BLOCK
    NB = (N * TOP_K + E * (BLK - 1)) // BLK
    counts = cnt[:, 0].astype(jnp.int32)
    nblk_e = (counts + BLK - 1) // BLK
    bend = jnp.cumsum(nblk_e)
    first_blk = (bend - nblk_e).astype(jnp.int32)
    rowstart = first_blk * BLK
    nact = bend[-1:].astype(jnp.int32)
    return counts, rowstart, first_blk, nblk_e.astype(jnp.int32), nact, NB


def kernel(x, rel_bias_table, w_in, b_gate, w_pool_grp, pool_scale, w_branch_pool, lambda_q1,
           lambda_k1, lambda_q2, lambda_k2, subln_gain, w_branch_attn, w_out, ln1_g, ln1_b,
           w_router, router_bias, w_exp_gate, w_exp_up, w_exp_down, w_sh_gate, w_sh_up,
           w_sh_down, ln2_g, ln2_b):
    B, S, D = x.shape
    L = w_in.shape[0]
    E = w_router.shape[2]
    P = pool_scale.shape[1]
    A = w_branch_attn.shape[1]
    N = B * S
    alpha = (2 * L) ** 0.25
    assert S % TOKEN_TILE == 0 and S % MERGE_TILE == 0
    assert A == N_DIFF_HEADS * 2 * DIFF_HEAD_DIM and TOKEN_TILE >= REL_MAX_DIST
    assert D == SUBLANES * LANES, "a row tile holds exactly one (SUBLANES, LANES) tile per row"
    band, far_c = _bias_tables(rel_bias_table, S, TOKEN_TILE)

    h = x
    for i in range(L):
        w_uqkv = w_in[i][:, :P + 3 * A].astype(BF16)
        w_gate = w_in[i][:, P + 3 * A:].astype(BF16)
        u, qk, vt = _inproj(h, w_uqkv, P, A)
        o = _attention(qk, vt, band, far_c, lambda_q1[i], lambda_k1[i], lambda_q2[i],
                       lambda_k2[i], subln_gain[i], _lambda_init(i))
        h1, h1_tiles = _merge(h, u, o, w_gate, b_gate[i], w_pool_grp[i].astype(BF16), pool_scale[i],
                    w_branch_pool[i].astype(BF16), w_branch_attn[i].astype(BF16),
                    w_out[i].astype(BF16), ln1_g[i], ln1_b[i], alpha)
        t = h1.reshape(N, D)
        wr_hi, wr_lo = _split_bf16(w_router[i].astype(F32).T)
        idx, wts, rank, cnt, pre = _route(t, wr_hi, wr_lo, router_bias[i],
                                          w_sh_gate[i].astype(BF16), w_sh_up[i].astype(BF16),
                                          w_sh_down[i].astype(BF16), alpha)
        counts, rowstart, first_blk, nblk_e, nact, NB = _dispatch_meta(cnt, N, E)
        dest = _slots(idx, rank, rowstart)
        x_sorted = _dispatch(h1_tiles.reshape(N * SUBLANES, LANES), dest, rowstart, counts, nact, NB)
        y_sorted = _experts(x_sorted, first_blk, nblk_e, nact, NB, w_exp_gate[i], w_exp_up[i],
                            w_exp_down[i])
        h = _combine(y_sorted, dest, wts, pre, ln2_g[i], ln2_b[i]).reshape(B, S, D)
    return h
```

```python
import functools
import math

import numpy as np
import jax
import jax.numpy as jnp
from jax import lax
from jax.experimental import pallas as pl
from jax.experimental.pallas import tpu as pltpu

F32 = jnp.float32
BF16 = jnp.bfloat16

POOL_WINDOWS = (2, 4, 8, 16)
N_DIFF_HEADS = 4
DIFF_HEAD_DIM = 64
REL_BUCKETS = 32
REL_MAX_DIST = 128
TOP_K = 8
N_EXPERT_GROUPS = 8
TOPK_GROUPS = 4
ROUTED_SCALE = 2.5
LN_EPS = 1e-5
LOG2E = math.log2(math.e)

LANES = 128
SUBLANES = 8
VMEM_LIMIT_BYTES = 56 * 1024 * 1024

TOKEN_TILE = 512
MERGE_TILE = 256
EXPERT_BLOCK = 256
STREAMS = 1
SLOTS = 4

_NT = (((1,), (1,)), ((), ()))


def _lambda_init(layer_idx):
    return 0.8 - 0.6 * math.exp(-0.3 * layer_idx)


def _dot(a, b):
    return jnp.dot(a, b, preferred_element_type=F32)


def _split_bf16(a):
    hi = a.astype(BF16)
    lo = (a - hi.astype(F32)).astype(BF16)
    return hi, lo


def _sigmoid(z):
    return 1.0 / (1.0 + jnp.exp(-z))


def _layer_norm(z, g, b):
    mu = jnp.mean(z, axis=-1, keepdims=True)
    zc = z - mu
    var = jnp.mean(zc * zc, axis=-1, keepdims=True)
    return zc * lax.rsqrt(var + LN_EPS) * g + b


def _params(*sem):
    return pltpu.CompilerParams(dimension_semantics=sem, vmem_limit_bytes=VMEM_LIMIT_BYTES)


def _const_spec(shape):
    nd = len(shape)
    return pl.BlockSpec(shape, lambda *_: (0,) * nd)


def _to_row_tiles(ref, x):
    rows = x.shape[0]
    for s in range(SUBLANES):
        ref[pl.ds(s, rows, stride=SUBLANES), :] = x[:, s * LANES:(s + 1) * LANES]


def _from_row_tiles(ref, start, rows):
    return jnp.concatenate([ref[pl.ds(start + s, rows, stride=SUBLANES), :]
                            for s in range(SUBLANES)], axis=1)


def _inproj_kernel(x_ref, w_ref, u_ref, qk_ref, vt_ref, *, pool_w, attn_w, q_scale):
    xb = x_ref[...].astype(BF16)
    p = _dot(xb, w_ref[...])
    u_ref[...] = p[:, :pool_w]
    q = p[:, pool_w:pool_w + attn_w] * q_scale
    k = p[:, pool_w + attn_w:pool_w + 2 * attn_w]
    qk_ref[:, :attn_w] = q.astype(BF16)
    qk_ref[:, attn_w:] = k.astype(BF16)
    v = p[:, pool_w + 2 * attn_w:pool_w + 3 * attn_w]
    vt_ref[...] = v.T.astype(BF16)


def _inproj(x, w_uqkv, pool_w, attn_w):
    B, S, D = x.shape
    T = TOKEN_TILE
    ns = S // T
    width = w_uqkv.shape[1]
    kern = functools.partial(_inproj_kernel, pool_w=pool_w, attn_w=attn_w,
                             q_scale=DIFF_HEAD_DIM ** -0.5 * LOG2E)
    return pl.pallas_call(
        kern,
        grid=(B, ns),
        in_specs=[pl.BlockSpec((None, T, D), lambda b, s: (b, s, 0)),
                  _const_spec((D, width))],
        out_specs=[pl.BlockSpec((None, T, pool_w), lambda b, s: (b, s, 0)),
                   pl.BlockSpec((None, T, 2 * attn_w), lambda b, s: (b, s, 0)),
                   pl.BlockSpec((None, None, attn_w, T), lambda b, s: (b, s, 0, 0))],
        out_shape=[jax.ShapeDtypeStruct((B, S, pool_w), F32),
                   jax.ShapeDtypeStruct((B, S, 2 * attn_w), BF16),
                   jax.ShapeDtypeStruct((B, ns, attn_w, T), BF16)],
        compiler_params=_params("parallel", "parallel"),
        name="inproj",
    )(x, w_uqkv)


def _t5_bucket_np(rel):
    half = REL_BUCKETS // 2
    max_exact = half // 2
    ret = np.where(rel > 0, half, 0)
    n = np.abs(rel)
    nf = np.maximum(n, 1).astype(np.float32)
    large = max_exact + (np.log(nf / max_exact) / math.log(REL_MAX_DIST / max_exact)
                         * (half - max_exact)).astype(np.int32)
    large = np.minimum(large, half - 1)
    return ret + np.where(n < max_exact, n, large)


def _t5_bucket(rel):
    half = REL_BUCKETS // 2
    max_exact = half // 2
    ret = jnp.where(rel > 0, half, 0)
    n = jnp.abs(rel)
    nf = jnp.maximum(n, 1).astype(F32)
    large = max_exact + (jnp.log(nf / max_exact) / math.log(REL_MAX_DIST / max_exact)
                         * (half - max_exact)).astype(jnp.int32)
    large = jnp.minimum(large, half - 1)
    return ret + jnp.where(n < max_exact, n, large)


def _bias_tables(table, S, T):
    far = np.arange(T + 1, S)
    if far.size:
        assert np.all(_t5_bucket_np(-far) == _t5_bucket_np(-far[-1]))
        assert np.all(_t5_bucket_np(far) == _t5_bucket_np(far[-1]))
    rel = jnp.arange(-(S - 1), S, dtype=jnp.int32)
    onehot = _t5_bucket(rel)[None, :, None] == jnp.arange(REL_BUCKETS)[None, None, :]
    dist_bias = jnp.sum(jnp.where(onehot, table.astype(F32).T[:, None, :] * LOG2E, 0.0),
                        axis=-1)
    pad = jnp.pad(dist_bias, ((0, 0), (2 * T, 2 * T)), mode='edge')
    diag = jnp.stack([lax.slice_in_dim(pad, (d - 1) * T - (T - 1) + S - 1 + 2 * T,
                                       (d - 1) * T + T + S - 1 + 2 * T, axis=1)
                      for d in range(3)], axis=1)
    w = jnp.concatenate([diag[..., ::-1], diag[..., :1]], axis=-1)
    flat = jnp.tile(w, (1, 1, T))[..., :T * (2 * T - 1)]
    band = flat.reshape(w.shape[0], 3, T, 2 * T - 1)[..., T - 1:]
    far_c = jnp.stack([dist_bias[:, 0], dist_bias[:, 2 * S - 2]], axis=1)
    return band, far_c


def _attn_kernel(far_ref, q_ref, k_ref, vt_ref, band_ref, lq1_ref, lk1_ref, lq2_ref, lk2_ref,
                 gain_ref, o_ref, qz_ref, s00, s01, s10, s11, e00, e01, e10, e11, acc0, acc1,
                 mt_ref, al_ref, m_ref, l_ref, *, T, nk, lam_init):
    h = pl.program_id(1)
    i = pl.program_id(2)
    s_buf = ((s00, s01), (s10, s11))
    e_buf = ((e00, e01), (e10, e11))
    acc = (acc0, acc1)
    c_left = far_ref[h, 0]
    c_right = far_ref[h, 1]

    qt = q_ref[...].astype(F32).T
    half = lax.broadcasted_iota(jnp.int32, qt.shape, 0) < DIFF_HEAD_DIM
    qz_ref[0] = jnp.where(half, qt, 0.0).astype(BF16)
    qz_ref[1] = jnp.where(half, 0.0, qt).astype(BF16)
    m_ref[...] = jnp.full(m_ref.shape, -jnp.inf, F32)
    l_ref[...] = jnp.zeros(l_ref.shape, F32)
    acc0[...] = jnp.zeros(acc0.shape, F32)
    acc1[...] = jnp.zeros(acc1.shape, F32)

    has_prev = i >= 1
    has_next = i <= nk - 2
    n_left = jnp.maximum(i - 1, 0) - jnp.where(has_next, 0, 1)
    right0 = i + 2 + jnp.where(has_prev, 0, 1)

    def tile_of(p):
        if isinstance(p, int) and p == 0:
            return i, band_ref[1], None
        if isinstance(p, int) and p == 1:
            return (jnp.where(has_prev, i - 1, i + 2),
                    jnp.where(has_prev, band_ref[0], c_right), None)
        if isinstance(p, int) and p == 2:
            return (jnp.where(has_next, i + 1, i - 2),
                    jnp.where(has_next, band_ref[2], c_left), None)
        f = p - 3
        is_left = f < n_left
        return (jnp.where(is_left, f, f - n_left + right0), None,
                jnp.where(is_left, c_left, c_right))

    def stage_a(p, x):
        j, bias, const = tile_of(p)
        kt = k_ref[pl.ds(pl.multiple_of(j * T, T), T), :]
        for c in range(2):
            s = _dot(kt, qz_ref[c])
            if bias is not None:
                s = s + bias
            s_buf[x][c][...] = s
            mt = jnp.max(s, axis=0, keepdims=True)
            mt_ref[2 * x + c] = mt if const is None else mt + const

    def stage_b(p, x):
        _, _, const = tile_of(p)
        for c in range(2):
            m_old = m_ref[c]
            m_new = jnp.maximum(m_old, mt_ref[2 * x + c])
            alpha = jnp.exp2(m_old - m_new)
            shift = m_new if const is None else m_new - const
            e = jnp.exp2(s_buf[x][c][...] - shift)
            l_ref[c] = alpha * l_ref[c] + jnp.sum(e, axis=0, keepdims=True)
            e_buf[x][c][...] = e.astype(BF16)
            al_ref[2 * x + c] = alpha
            m_ref[c] = m_new

    def stage_c(p, x):
        j, _, _ = tile_of(p)
        vt = vt_ref[j]
        for c in range(2):
            acc[c][...] = al_ref[2 * x + c] * acc[c][...] + _dot(vt, e_buf[x][c][...])

    def iteration(n, parity):
        static = isinstance(n, int)
        if not static or n < nk:
            stage_a(n, parity)
        if not static or 1 <= n <= nk:
            stage_b(n - 1, 1 - parity)
        if not static or 2 <= n <= nk + 1:
            stage_c(n - 2, parity)

    head = 5
    for n in range(head):
        iteration(n, n % 2)

    def pair(it, carry):
        n = head + 2 * it
        iteration(n, head % 2)
        iteration(n + 1, 1 - head % 2)
        return carry

    n_pairs = (nk - 1 - head) // 2
    lax.fori_loop(0, n_pairs, pair, 0)
    for n in range(head + 2 * n_pairs, nk + 2):
        iteration(n, n % 2)

    lam = (jnp.exp(jnp.sum(lq1_ref[...] * lk1_ref[...], keepdims=True))
           - jnp.exp(jnp.sum(lq2_ref[...] * lk2_ref[...], keepdims=True)) + lam_init)
    o = acc0[...] / l_ref[0] - lam * (acc1[...] / l_ref[1])
    ms = jnp.mean(o * o, axis=0, keepdims=True)
    y = o * lax.rsqrt(ms + LN_EPS) * gain_ref[...] * (1.0 - lam_init)
    o_ref[...] = y.T.astype(BF16)


def _attention(qk, vt, band, far_c, lq1, lk1, lq2, lk2, gain, lam_init):
    B, S, _ = qk.shape
    T = TOKEN_TILE
    nk = S // T
    assert nk >= 6, "the pipeline prologue assumes at least three far key tiles"
    H = N_DIFF_HEADS
    hw = 2 * DIFF_HEAD_DIM
    kern = functools.partial(_attn_kernel, T=T, nk=nk, lam_init=lam_init)
    vec = lambda a: a.reshape(1, DIFF_HEAD_DIM).astype(F32)
    return pl.pallas_call(
        kern,
        grid=(B, H, nk),
        in_specs=[pl.BlockSpec(memory_space=pltpu.SMEM),
                  pl.BlockSpec((None, T, hw), lambda b, h, i: (b, i, h)),
                  pl.BlockSpec((None, S, hw), lambda b, h, i: (b, 0, H + h)),
                  pl.BlockSpec((None, nk, hw, T), lambda b, h, i: (b, 0, h, 0)),
                  pl.BlockSpec((None, 3, T, T), lambda b, h, i: (h, 0, 0, 0)),
                  _const_spec((1, DIFF_HEAD_DIM)), _const_spec((1, DIFF_HEAD_DIM)),
                  _const_spec((1, DIFF_HEAD_DIM)), _const_spec((1, DIFF_HEAD_DIM)),
                  _const_spec((hw, 1))],
        out_specs=pl.BlockSpec((None, T, hw), lambda b, h, i: (b, i, h)),
        out_shape=jax.ShapeDtypeStruct((B, S, H * hw), BF16),
        scratch_shapes=([pltpu.VMEM((2, hw, T), BF16)]
                        + [pltpu.VMEM((T, T), F32)] * 4 + [pltpu.VMEM((T, T), BF16)] * 4
                        + [pltpu.VMEM((hw, T), F32)] * 2
                        + [pltpu.VMEM((4, 1, T), F32), pltpu.VMEM((4, 1, T), F32),
                           pltpu.VMEM((2, 1, T), F32), pltpu.VMEM((2, 1, T), F32)]),
        compiler_params=_params("parallel", "parallel", "parallel"),
        name="diff_attn",
    )(far_c, qk, qk, vt, band, vec(lq1), vec(lk1), vec(lq2), vec(lk2),
      gain.reshape(hw, 1).astype(F32))


def _pool_matrices(TM):
    r = np.arange(TM)[:, None]
    c = np.arange(TM)[None, :]
    hpos = np.concatenate([np.arange(-SUBLANES, 0), np.arange(TM, TM + SUBLANES),
                           np.full(LANES - 2 * SUBLANES, 10 ** 6)])[None, :]
    main, halo = [], []
    for w in POOL_WINDOWS:
        lo, hi = r - w // 2, r + w - w // 2
        main.append((c >= lo) & (c < hi))
        halo.append((hpos >= lo) & (hpos < hi))
    return (jnp.asarray(np.stack(main), BF16), jnp.asarray(np.stack(halo), BF16))


def _merge_kernel(x_ref, u_ref, up_ref, un_ref, o_ref, wg_ref, bg_ref, am_ref, ah_ref, wgrp_ref,
                  ps_ref, wbp_ref, wba_ref, wo_ref, lng_ref, lnb_ref, h_ref, ht_ref, *, TM, S, alpha):
    s_idx = pl.program_id(1)
    ns = pl.num_programs(1)
    x = x_ref[...]
    D = x.shape[1]
    xb = x.astype(BF16)
    gates = _sigmoid(_dot(xb, wg_ref[...]) + bg_ref[...])

    u = u_ref[...]
    P = u.shape[1]
    prev = jnp.where(s_idx > 0, up_ref[...], 0.0)
    nxt = jnp.where(s_idx < ns - 1, un_ref[...], 0.0)
    halo = jnp.concatenate([prev, nxt, jnp.zeros((LANES - 2 * SUBLANES, P), F32)], axis=0)
    u_hi, u_lo = _split_bf16(u)
    h_hi, h_lo = _split_bf16(halo)
    pos = s_idx * TM + lax.broadcasted_iota(jnp.int32, (TM, 1), 0)
    G = P // len(POOL_WINDOWS)
    mixed = []
    for g, w in enumerate(POOL_WINDOWS):
        sl = slice(g * G, (g + 1) * G)
        win2 = (_dot(am_ref[g], jnp.concatenate([u_hi[:, sl], u_lo[:, sl]], axis=1))
                + _dot(ah_ref[g], jnp.concatenate([h_hi[:, sl], h_lo[:, sl]], axis=1)))
        win = win2[:, :G] + win2[:, G:]
        cnt = (jnp.minimum(pos + (w - w // 2), S) - jnp.maximum(pos - w // 2, 0)).astype(F32)
        pooled = win / cnt - u[:, sl]
        mixed.append(_dot(pooled.astype(BF16), wgrp_ref[g]))
    mix = jnp.concatenate(mixed, axis=1) * ps_ref[...]
    y_pool = _dot(mix.astype(BF16), wbp_ref[...])
    y_attn = _dot(o_ref[...], wba_ref[...])
    merged = gates[:, :D] * y_pool + gates[:, D:] * y_attn
    y = _dot(merged.astype(BF16), wo_ref[...])
    h = _layer_norm(alpha * x + y, lng_ref[...], lnb_ref[...])
    h_ref[...] = h
    _to_row_tiles(ht_ref, h)


def _merge(x, u, o, w_gate, b_gate, w_grp, pool_scale, w_bp, w_ba, w_out, ln_g, ln_b, alpha):
    B, S, D = x.shape
    P = u.shape[2]
    A = o.shape[2]
    TM = MERGE_TILE
    ns = S // TM
    hb = TM // SUBLANES
    am, ah = _pool_matrices(TM)
    nw = len(POOL_WINDOWS)
    G = P // nw
    row = lambda a: a.reshape(1, -1).astype(F32)
    kern = functools.partial(_merge_kernel, TM=TM, S=S, alpha=alpha)
    return pl.pallas_call(
        kern,
        grid=(B, ns),
        in_specs=[pl.BlockSpec((None, TM, D), lambda b, s: (b, s, 0)),
                  pl.BlockSpec((None, TM, P), lambda b, s: (b, s, 0)),
                  pl.BlockSpec((None, SUBLANES, P),
                               lambda b, s: (b, jnp.maximum(s * hb - 1, 0), 0)),
                  pl.BlockSpec((None, SUBLANES, P),
                               lambda b, s: (b, jnp.minimum((s + 1) * hb, S // SUBLANES - 1), 0)),
                  pl.BlockSpec((None, TM, A), lambda b, s: (b, s, 0)),
                  _const_spec((D, 2 * D)), _const_spec((1, 2 * D)),
                  _const_spec((nw, TM, TM)), _const_spec((nw, TM, LANES)),
                  _const_spec((nw, G, G)), _const_spec((1, P)),
                  _const_spec((P, D)), _const_spec((A, D)), _const_spec((D, D)),
                  _const_spec((1, D)), _const_spec((1, D))],
        out_specs=[pl.BlockSpec((None, TM, D), lambda b, s: (b, s, 0)),
                   pl.BlockSpec((None, TM * SUBLANES, LANES), lambda b, s: (b, s, 0))],
        out_shape=[jax.ShapeDtypeStruct((B, S, D), F32),
                   jax.ShapeDtypeStruct((B, S * SUBLANES, LANES), F32)],
        compiler_params=_params("parallel", "parallel"),
        name="merge",
    )(x, u, u, u, o, w_gate, row(b_gate), am, ah, w_grp, row(pool_scale), w_bp, w_ba, w_out,
      row(ln_g), row(ln_b))


def _route_kernel(h_ref, wrh_ref, wrl_ref, rb_ref, tri_ref, ones_ref, wsg_ref, wsu_ref, wsd_ref,
                  idx_ref, wt_ref, rank_ref, cnt_ref, pre_ref, cnt_acc, *, TM, E, alpha):
    t = pl.program_id(0)

    @pl.when(t == 0)
    def _():
        cnt_acc[...] = jnp.zeros(cnt_acc.shape, F32)

    h = h_ref[...]
    hb, hl = _split_bf16(h)
    logits = (lax.dot_general(wrh_ref[...], hb, _NT, preferred_element_type=F32)
              + lax.dot_general(wrh_ref[...], hl, _NT, preferred_element_type=F32)
              + lax.dot_general(wrl_ref[...], hb, _NT, preferred_element_type=F32))
    s = _sigmoid(logits)
    biased = s + rb_ref[...]
    gsz = E // N_EXPERT_GROUPS
    sub = lax.broadcasted_iota(jnp.int32, (gsz, TM), 0).astype(F32)
    neg = -jnp.inf
    grp, gscore = [], []
    for g in range(N_EXPERT_GROUPS):
        bg = biased[g * gsz:(g + 1) * gsz, :]
        m1 = jnp.max(bg, axis=0, keepdims=True)
        first = jnp.min(jnp.where(bg == m1, sub, float(gsz)), axis=0, keepdims=True)
        m2 = jnp.max(jnp.where(sub == first, neg, bg), axis=0, keepdims=True)
        grp.append(bg)
        gscore.append(m1 + m2)
    masked = []
    for g in range(N_EXPERT_GROUPS):
        beaten = jnp.zeros((1, TM), F32)
        for g2 in range(N_EXPERT_GROUPS):
            if g2 == g:
                continue
            wins = (gscore[g2] > gscore[g]) if g2 > g else (gscore[g2] >= gscore[g])
            beaten = beaten + wins.astype(F32)
        masked.append(jnp.where(beaten < float(TOPK_GROUPS), grp[g], neg))
    masked = jnp.concatenate(masked, axis=0)

    row = lax.broadcasted_iota(jnp.int32, (E, TM), 0).astype(F32)
    sel = jnp.zeros((E, TM), F32)
    firsts, ws = [], []
    for _ in range(TOP_K):
        mx = jnp.max(masked, axis=0, keepdims=True)
        first = jnp.min(jnp.where(masked == mx, row, float(E)), axis=0, keepdims=True)
        oh = row == first
        ws.append(jnp.sum(jnp.where(oh, s, 0.0), axis=0, keepdims=True))
        masked = jnp.where(oh, neg, masked)
        sel = jnp.where(oh, 1.0, sel)
        firsts.append(first)
    wsum = ws[0]
    for w in ws[1:]:
        wsum = wsum + w

    sel_b = sel.astype(BF16)
    base = jnp.concatenate([cnt_acc[...]] * (TM // LANES), axis=1)
    rank_all = _dot(sel_b, tri_ref[...]) + base
    cnt_acc[...] = cnt_acc[...] + _dot(sel_b, ones_ref[...])
    cnt_ref[...] = cnt_acc[...]
    for k in range(TOP_K):
        oh = row == firsts[k]
        idx_k = firsts[k].astype(jnp.int32)
        wt_k = ws[k] / wsum * ROUTED_SCALE
        rank_k = jnp.sum(jnp.where(oh, rank_all, 0.0), axis=0, keepdims=True).astype(jnp.int32)
        for c in range(TM // LANES):
            sl = slice(c * LANES, (c + 1) * LANES)
            idx_ref[c, k:k + 1, :] = idx_k[:, sl]
            wt_ref[c, k:k + 1, :] = wt_k[:, sl]
            rank_ref[c, k:k + 1, :] = rank_k[:, sl]

    g_act = _dot(hb, wsg_ref[...])
    hid = g_act * _sigmoid(g_act) * _dot(hb, wsu_ref[...])
    pre_ref[...] = alpha * h + _dot(hid.astype(BF16), wsd_ref[...])


def _route(h, wr_t_hi, wr_t_lo, router_bias, w_sg, w_su, w_sd, alpha):
    N, D = h.shape
    E = wr_t_hi.shape[0]
    TM = TOKEN_TILE
    Hs = w_sg.shape[1]
    tri = jnp.asarray(np.triu(np.ones((TM, TM), np.float32), k=1), BF16)
    ones = jnp.ones((TM, LANES), BF16)
    kern = functools.partial(_route_kernel, TM=TM, E=E, alpha=alpha)
    kt = pl.BlockSpec((TM // LANES, TOP_K, LANES), lambda t: (t, 0, 0))
    return pl.pallas_call(
        kern,
        grid=(N // TM,),
        in_specs=[pl.BlockSpec((TM, D), lambda t: (t, 0)),
                  _const_spec((E, D)), _const_spec((E, D)), _const_spec((E, 1)),
                  _const_spec((TM, TM)), _const_spec((TM, LANES)),
                  _const_spec((D, Hs)), _const_spec((D, Hs)), _const_spec((Hs, D))],
        out_specs=[kt, kt, kt, _const_spec((E, LANES)), pl.BlockSpec((TM, D), lambda t: (t, 0))],
        out_shape=[jax.ShapeDtypeStruct((N // LANES, TOP_K, LANES), jnp.int32),
                   jax.ShapeDtypeStruct((N // LANES, TOP_K, LANES), F32),
                   jax.ShapeDtypeStruct((N // LANES, TOP_K, LANES), jnp.int32),
                   jax.ShapeDtypeStruct((E, LANES), F32),
                   jax.ShapeDtypeStruct((N, D), F32)],
        scratch_shapes=[pltpu.VMEM((E, LANES), F32)],
        compiler_params=_params("arbitrary"),
        name="route_shared",
    )(h, wr_t_hi, wr_t_lo, router_bias.reshape(E, 1).astype(F32), tri, ones, w_sg, w_su, w_sd)


def _slots_kernel(idx_ref, rank_ref, rowstart_ref, dest_ref, *, E):
    expert = lax.broadcasted_iota(jnp.int32, (E, LANES), 0)
    rowstart = rowstart_ref[...]
    for c in range(idx_ref.shape[0]):
        idx = idx_ref[c]
        rank = rank_ref[c]
        for k in range(TOP_K):
            base = jnp.sum(jnp.where(expert == idx[k:k + 1, :], rowstart, 0.0), axis=0,
                           keepdims=True)
            dest_ref[c, k:k + 1, :] = base.astype(jnp.int32) + rank[k:k + 1, :]


def _slots(idx, rank, rowstart):
    nb = idx.shape[0]
    E = rowstart.shape[0]
    step = SUBLANES
    spec = pl.BlockSpec((step, TOP_K, LANES), lambda t: (t, 0, 0))
    return pl.pallas_call(
        functools.partial(_slots_kernel, E=E),
        grid=(nb // step,),
        in_specs=[spec, spec, _const_spec((E, 1))],
        out_specs=spec,
        out_shape=jax.ShapeDtypeStruct(idx.shape, jnp.int32),
        compiler_params=_params("parallel"),
        name="slots",
    )(idx, rank, rowstart.astype(F32).reshape(E, 1))


def _dispatch_kernel(rowstart_ref, cnt_ref, nact_ref, dest_ref, h_ref, xs_hbm,
                     stage, zeros, sem, zsem, *, TM, BLK, NB, E):
    t = pl.program_id(0)
    nt = pl.num_programs(0)
    slot = t % 2
    R = SUBLANES

    def zero_copy(first, n, start):
        cp = pltpu.make_async_copy(zeros.at[pl.ds(0, n * R)],
                                   xs_hbm.at[pl.ds(pl.multiple_of(first * R, R), n * R)], zsem)
        cp.start() if start else cp.wait()

    def wait_slot(s):
        for _ in range(TOP_K):
            pltpu.make_async_copy(stage.at[s], xs_hbm.at[pl.ds(0, TM * R)], sem.at[s]).wait()

    @pl.when(t == 0)
    def _():
        zeros[...] = jnp.zeros(zeros.shape, F32)
        nact = nact_ref[0]

        def tail(e, c, *, start):
            first = rowstart_ref[e] + cnt_ref[e]
            n_pad = (cnt_ref[e] + BLK - 1) // BLK * BLK - cnt_ref[e]
            done = jnp.int32(0)
            size = BLK // 2
            while size >= 1:
                piece = n_pad & size

                @pl.when(piece != 0)
                def _(size=size, done=done):
                    zero_copy(first + done, size, start)

                done = done + piece
                size //= 2
            return c

        def idle(b, c, *, start):
            zero_copy(b * BLK, BLK, start)
            return c

        for start in (True, False):
            lax.fori_loop(0, E, functools.partial(tail, start=start), 0)
            lax.fori_loop(nact, NB, functools.partial(idle, start=start), 0)

    @pl.when(t >= 2)
    def _():
        wait_slot(slot)

    stage[slot] = h_ref[...]
    for c in range(TM // LANES):
        def row_body(r, carry, c=c):
            src = pl.multiple_of((c * LANES + r) * R, R)
            for k in range(TOP_K):
                dst = pl.multiple_of(dest_ref[c, k, r] * R, R)
                pltpu.make_async_copy(stage.at[slot, pl.ds(src, R)], xs_hbm.at[pl.ds(dst, R)],
                                      sem.at[slot]).start(priority=k % 2)
            return carry

        lax.fori_loop(0, LANES, row_body, 0)

    @pl.when(t == nt - 1)
    def _():
        @pl.when(t >= 1)
        def _():
            wait_slot(1 - slot)

        wait_slot(slot)


def _dispatch(h_tiles, dest, rowstart, counts, nact, NB):
    N = h_tiles.shape[0] // SUBLANES
    E = rowstart.shape[0]
    TM = MERGE_TILE
    BLK = EXPERT_BLOCK
    R = SUBLANES
    kern = functools.partial(_dispatch_kernel, TM=TM, BLK=BLK, NB=NB, E=E)
    grid_spec = pltpu.PrefetchScalarGridSpec(
        num_scalar_prefetch=3,
        grid=(N // TM,),
        in_specs=[pl.BlockSpec((TM // LANES, TOP_K, LANES), lambda t, *_: (t, 0, 0),
                               memory_space=pltpu.SMEM),
                  pl.BlockSpec((TM * R, LANES), lambda t, *_: (t, 0))],
        out_specs=pl.BlockSpec(memory_space=pl.ANY),
        scratch_shapes=[pltpu.VMEM((2, TM * R, LANES), F32), pltpu.VMEM((BLK * R, LANES), F32),
                        pltpu.SemaphoreType.DMA((2,)), pltpu.SemaphoreType.DMA(())],
    )
    return pl.pallas_call(
        kern,
        grid_spec=grid_spec,
        out_shape=jax.ShapeDtypeStruct((NB * BLK * R, LANES), F32),
        compiler_params=_params("arbitrary"),
        name="dispatch",
    )(rowstart, counts, nact, dest, h_tiles)


def _expert_kernel(first_ref, nblk_ref, nact_ref, wg_ref, wu_ref, wd_ref, x_hbm, y_hbm,
                   xbuf, ybuf, wg_s, wu_s, wd_s, xsem, ysem, *, BLK, NB):
    e = pl.program_id(0)
    nact = nact_ref[0]
    rows = BLK * SUBLANES

    class _Parts:
        def __init__(self, hbm, buf, g, slot, sem, to_hbm):
            part = rows // STREAMS
            self.copies = []
            for p in range(STREAMS):
                far = hbm.at[pl.ds(pl.multiple_of(g * rows + p * part, part), part)]
                near = buf.at[slot, pl.ds(p * part, part)]
                src, dst = (near, far) if to_hbm else (far, near)
                self.copies.append(pltpu.make_async_copy(src, dst, sem.at[slot]))

        def start(self):
            for p, cp in enumerate(self.copies):
                cp.start(priority=p % 2)

        def wait(self):
            for cp in self.copies:
                cp.wait()

    def x_copy(g, slot):
        return _Parts(x_hbm, xbuf, g, slot, xsem, False)

    def y_copy(g, slot):
        return _Parts(y_hbm, ybuf, g, slot, ysem, True)

    ahead = SLOTS - 1

    @pl.when(e == 0)
    def _():
        for g0 in range(ahead):
            @pl.when(g0 < nact)
            def _(g0=g0):
                x_copy(g0, g0).start()

    @pl.when(nblk_ref[e] > 0)
    def _():
        wg_s[...] = wg_ref[...].astype(BF16)
        wu_s[...] = wu_ref[...].astype(BF16)
        wd_s[...] = wd_ref[...].astype(BF16)

    def body(j, carry):
        g = first_ref[e] + j
        slot = g % SLOTS
        x_copy(g, slot).wait()

        @pl.when(g + ahead < nact)
        def _():
            x_copy(g + ahead, (g + ahead) % SLOTS).start()

        @pl.when(g >= SLOTS)
        def _():
            y_copy(g - SLOTS, slot).wait()

        x = _from_row_tiles(xbuf.at[slot], 0, BLK).astype(BF16)
        g_act = _dot(x, wg_s[...])
        hid = g_act * _sigmoid(g_act) * _dot(x, wu_s[...])
        _to_row_tiles(ybuf.at[slot], _dot(hid.astype(BF16), wd_s[...]))
        y_copy(g, slot).start()
        return carry

    lax.fori_loop(0, nblk_ref[e], body, 0)

    @pl.when(e == pl.num_programs(0) - 1)
    def _():
        for back in range(SLOTS, 0, -1):
            @pl.when(nact >= back)
            def _(back=back):
                y_copy(nact - back, (nact - back) % SLOTS).wait()

        ybuf[0] = jnp.zeros(ybuf.shape[1:], F32)

        def idle(g, carry, *, start):
            cp = y_copy(g, 0)
            cp.start() if start else cp.wait()
            return carry

        for start in (True, False):
            lax.fori_loop(nact, NB, functools.partial(idle, start=start), 0)


def _experts(x_sorted, first_blk, nblk_e, nact, NB, w_eg, w_eu, w_ed):
    E, D, He = w_eg.shape
    BLK = EXPERT_BLOCK
    R = SUBLANES
    per_expert = lambda shape: pl.BlockSpec((None,) + shape, lambda e, *_: (e, 0, 0))
    grid_spec = pltpu.PrefetchScalarGridSpec(
        num_scalar_prefetch=3,
        grid=(E,),
        in_specs=[per_expert((D, He)), per_expert((D, He)), per_expert((He, D)),
                  pl.BlockSpec(memory_space=pl.ANY)],
        out_specs=pl.BlockSpec(memory_space=pl.ANY),
        scratch_shapes=[pltpu.VMEM((SLOTS, BLK * R, LANES), F32),
                        pltpu.VMEM((SLOTS, BLK * R, LANES), F32),
                        pltpu.VMEM((D, He), BF16), pltpu.VMEM((D, He), BF16),
                        pltpu.VMEM((He, D), BF16),
                        pltpu.SemaphoreType.DMA((SLOTS,)), pltpu.SemaphoreType.DMA((SLOTS,))],
    )
    return pl.pallas_call(
        functools.partial(_expert_kernel, BLK=BLK, NB=NB),
        grid_spec=grid_spec,
        out_shape=jax.ShapeDtypeStruct((NB * BLK * R, LANES), F32),
        compiler_params=_params("arbitrary"),
        name="routed_experts",
    )(first_blk, nblk_e, nact, w_eg, w_eu, w_ed, x_sorted)


def _combine_kernel(dest_ref, wt_ref, pre_ref, lng_ref, lnb_ref, y_hbm, o_ref, buf, sem, *, TM):
    t = pl.program_id(0)
    nt = pl.num_programs(0) - 1
    slot = t % 2
    R = SUBLANES

    @pl.when(t < nt)
    def _():
        for k in range(TOP_K):
            for r in range(TM):
                src = pl.multiple_of(dest_ref[0, k, r] * R, R)
                pltpu.make_async_copy(y_hbm.at[pl.ds(src, R)],
                                      buf.at[slot, pl.ds((k * TM + r) * R, R)],
                                      sem.at[slot]).start(priority=r % 2)

    @pl.when(t >= 1)
    def _():
        prev = 1 - slot
        for k in range(TOP_K):
            pltpu.make_async_copy(y_hbm.at[pl.ds(0, TM * R)],
                                  buf.at[prev, pl.ds(k * TM * R, TM * R)], sem.at[prev]).wait()
        w = jnp.concatenate([wt_ref[0], jnp.zeros((TM - TOP_K, TM), F32)], axis=0).T
        z = pre_ref[...]
        rows = buf.at[prev]
        for k in range(TOP_K):
            z = z + w[:, k:k + 1] * _from_row_tiles(rows, k * TM * R, TM)
        o_ref[...] = _layer_norm(z, lng_ref[...], lnb_ref[...])


def _combine(y_sorted, dest, wts, pre, ln_g, ln_b):
    N, D = pre.shape
    TM = LANES
    nt = N // TM
    R = SUBLANES
    kern = functools.partial(_combine_kernel, TM=TM)
    row = lambda a: a.reshape(1, -1).astype(F32)
    ahead = lambda t: (jnp.minimum(t, nt - 1), 0, 0)
    behind3 = lambda t: (jnp.maximum(t - 1, 0), 0, 0)
    behind = lambda t: (jnp.maximum(t - 1, 0), 0)
    return pl.pallas_call(
        kern,
        grid=(nt + 1,),
        in_specs=[pl.BlockSpec((1, TOP_K, TM), ahead, memory_space=pltpu.SMEM),
                  pl.BlockSpec((1, TOP_K, TM), behind3),
                  pl.BlockSpec((TM, D), behind),
                  _const_spec((1, D)), _const_spec((1, D)),
                  pl.BlockSpec(memory_space=pl.ANY)],
        out_specs=pl.BlockSpec((TM, D), behind),
        out_shape=jax.ShapeDtypeStruct((N, D), F32),
        scratch_shapes=[pltpu.VMEM((2, TOP_K * TM * R, LANES), F32),
                        pltpu.SemaphoreType.DMA((2,))],
        compiler_params=_params("arbitrary"),
        name="combine",
    )(dest, wts, pre, row(ln_g), row(ln_b), y_sorted)


def _dispatch_meta(cnt, N, E):
    BLK = EXPERT_BLOCK
    NB = (N * TOP_K + E * (BLK - 1)) // BLK
    counts = cnt[:, 0].astype(jnp.int32)
    nblk_e = (counts + BLK - 1) // BLK
    bend = jnp.cumsum(nblk_e)
    first_blk = (bend - nblk_e).astype(jnp.int32)
    rowstart = first_blk * BLK
    nact = bend[-1:].astype(jnp.int32)
    return counts, rowstart, first_blk, nblk_e.astype(jnp.int32), nact, NB


def kernel(x, rel_bias_table, w_in, b_gate, w_pool_grp, pool_scale, w_branch_pool, lambda_q1,
           lambda_k1, lambda_q2, lambda_k2, subln_gain, w_branch_attn, w_out, ln1_g, ln1_b,
           w_router, router_bias, w_exp_gate, w_exp_up, w_exp_down, w_sh_gate, w_sh_up,
           w_sh_down, ln2_g, ln2_b):
    B, S, D = x.shape
    L = w_in.shape[0]
    E = w_router.shape[2]
    P = pool_scale.shape[1]
    A = w_branch_attn.shape[1]
    N = B * S
    alpha = (2 * L) ** 0.25
    assert S % TOKEN_TILE == 0 and S % MERGE_TILE == 0
    assert A == N_DIFF_HEADS * 2 * DIFF_HEAD_DIM and TOKEN_TILE >= REL_MAX_DIST
    assert D == SUBLANES * LANES, "a row tile holds exactly one (SUBLANES, LANES) tile per row"
    band, far_c = _bias_tables(rel_bias_table, S, TOKEN_TILE)

    h = x
    for i in range(L):
        w_uqkv = w_in[i][:, :P + 3 * A].astype(BF16)
        w_gate = w_in[i][:, P + 3 * A:].astype(BF16)
        u, qk, vt = _inproj(h, w_uqkv, P, A)
        o = _attention(qk, vt, band, far_c, lambda_q1[i], lambda_k1[i], lambda_q2[i],
                       lambda_k2[i], subln_gain[i], _lambda_init(i))
        h1, h1_tiles = _merge(h, u, o, w_gate, b_gate[i], w_pool_grp[i].astype(BF16), pool_scale[i],
                    w_branch_pool[i].astype(BF16), w_branch_attn[i].astype(BF16),
                    w_out[i].astype(BF16), ln1_g[i], ln1_b[i], alpha)
        t = h1.reshape(N, D)
        wr_hi, wr_lo = _split_bf16(w_router[i].astype(F32).T)
        idx, wts, rank, cnt, pre = _route(t, wr_hi, wr_lo, router_bias[i],
                                          w_sh_gate[i].astype(BF16), w_sh_up[i].astype(BF16),
                                          w_sh_down[i].astype(BF16), alpha)
        counts, rowstart, first_blk, nblk_e, nact, NB = _dispatch_meta(cnt, N, E)
        dest = _slots(idx, rank, rowstart)
        x_sorted = _dispatch(h1_tiles.reshape(N * SUBLANES, LANES), dest, rowstart, counts, nact, NB)
        y_sorted = _experts(x_sorted, first_blk, nblk_e, nact, NB, w_exp_gate[i], w_exp_up[i],
                            w_exp_down[i])
        h = _combine(y_sorted, dest, wts, pre, ln2_g[i], ln2_b[i]).reshape(B, S, D)
    return h
```

```python
import functools
import math

import numpy as np
import jax
import jax.numpy as jnp
from jax import lax
from jax.experimental import pallas as pl
from jax.experimental.pallas import tpu as pltpu

F32 = jnp.float32
BF16 = jnp.bfloat16

POOL_WINDOWS = (2, 4, 8, 16)
N_DIFF_HEADS = 4
DIFF_HEAD_DIM = 64
REL_BUCKETS = 32
REL_MAX_DIST = 128
TOP_K = 8
N_EXPERT_GROUPS = 8
TOPK_GROUPS = 4
ROUTED_SCALE = 2.5
LN_EPS = 1e-5
LOG2E = math.log2(math.e)

LANES = 128
SUBLANES = 8
VMEM_LIMIT_BYTES = 56 * 1024 * 1024

TOKEN_TILE = 512
MERGE_TILE = 256
EXPERT_BLOCK = 256
STREAMS = 1
SLOTS = 8

_NT = (((1,), (1,)), ((), ()))


def _lambda_init(layer_idx):
    return 0.8 - 0.6 * math.exp(-0.3 * layer_idx)


def _dot(a, b):
    return jnp.dot(a, b, preferred_element_type=F32)


def _split_bf16(a):
    hi = a.astype(BF16)
    lo = (a - hi.astype(F32)).astype(BF16)
    return hi, lo


def _sigmoid(z):
    return 1.0 / (1.0 + jnp.exp(-z))


def _layer_norm(z, g, b):
    mu = jnp.mean(z, axis=-1, keepdims=True)
    zc = z - mu
    var = jnp.mean(zc * zc, axis=-1, keepdims=True)
    return zc * lax.rsqrt(var + LN_EPS) * g + b


def _params(*sem):
    return pltpu.CompilerParams(dimension_semantics=sem, vmem_limit_bytes=VMEM_LIMIT_BYTES)


def _const_spec(shape):
    nd = len(shape)
    return pl.BlockSpec(shape, lambda *_: (0,) * nd)


def _to_row_tiles(ref, x):
    rows = x.shape[0]
    for s in range(SUBLANES):
        ref[pl.ds(s, rows, stride=SUBLANES), :] = x[:, s * LANES:(s + 1) * LANES]


def _from_row_tiles(ref, start, rows):
    return jnp.concatenate([ref[pl.ds(start + s, rows, stride=SUBLANES), :]
                            for s in range(SUBLANES)], axis=1)


def _inproj_kernel(x_ref, w_ref, u_ref, qk_ref, vt_ref, *, pool_w, attn_w, q_scale):
    xb = x_ref[...].astype(BF16)
    p = _dot(xb, w_ref[...])
    u_ref[...] = p[:, :pool_w]
    q = p[:, pool_w:pool_w + attn_w] * q_scale
    k = p[:, pool_w + attn_w:pool_w + 2 * attn_w]
    qk_ref[:, :attn_w] = q.astype(BF16)
    qk_ref[:, attn_w:] = k.astype(BF16)
    v = p[:, pool_w + 2 * attn_w:pool_w + 3 * attn_w]
    vt_ref[...] = v.T.astype(BF16)


def _inproj(x, w_uqkv, pool_w, attn_w):
    B, S, D = x.shape
    T = TOKEN_TILE
    ns = S // T
    width = w_uqkv.shape[1]
    kern = functools.partial(_inproj_kernel, pool_w=pool_w, attn_w=attn_w,
                             q_scale=DIFF_HEAD_DIM ** -0.5 * LOG2E)
    return pl.pallas_call(
        kern,
        grid=(B, ns),
        in_specs=[pl.BlockSpec((None, T, D), lambda b, s: (b, s, 0)),
                  _const_spec((D, width))],
        out_specs=[pl.BlockSpec((None, T, pool_w), lambda b, s: (b, s, 0)),
                   pl.BlockSpec((None, T, 2 * attn_w), lambda b, s: (b, s, 0)),
                   pl.BlockSpec((None, None, attn_w, T), lambda b, s: (b, s, 0, 0))],
        out_shape=[jax.ShapeDtypeStruct((B, S, pool_w), F32),
                   jax.ShapeDtypeStruct((B, S, 2 * attn_w), BF16),
                   jax.ShapeDtypeStruct((B, ns, attn_w, T), BF16)],
        compiler_params=_params("parallel", "parallel"),
        name="inproj",
    )(x, w_uqkv)


def _t5_bucket_np(rel):
    half = REL_BUCKETS // 2
    max_exact = half // 2
    ret = np.where(rel > 0, half, 0)
    n = np.abs(rel)
    nf = np.maximum(n, 1).astype(np.float32)
    large = max_exact + (np.log(nf / max_exact) / math.log(REL_MAX_DIST / max_exact)
                         * (half - max_exact)).astype(np.int32)
    large = np.minimum(large, half - 1)
    return ret + np.where(n < max_exact, n, large)


def _t5_bucket(rel):
    half = REL_BUCKETS // 2
    max_exact = half // 2
    ret = jnp.where(rel > 0, half, 0)
    n = jnp.abs(rel)
    nf = jnp.maximum(n, 1).astype(F32)
    large = max_exact + (jnp.log(nf / max_exact) / math.log(REL_MAX_DIST / max_exact)
                         * (half - max_exact)).astype(jnp.int32)
    large = jnp.minimum(large, half - 1)
    return ret + jnp.where(n < max_exact, n, large)


def _bias_tables(table, S, T):
    far = np.arange(T + 1, S)
    if far.size:
        assert np.all(_t5_bucket_np(-far) == _t5_bucket_np(-far[-1]))
        assert np.all(_t5_bucket_np(far) == _t5_bucket_np(far[-1]))
    rel = jnp.arange(-(S - 1), S, dtype=jnp.int32)
    onehot = _t5_bucket(rel)[None, :, None] == jnp.arange(REL_BUCKETS)[None, None, :]
    dist_bias = jnp.sum(jnp.where(onehot, table.astype(F32).T[:, None, :] * LOG2E, 0.0),
                        axis=-1)
    pad = jnp.pad(dist_bias, ((0, 0), (2 * T, 2 * T)), mode='edge')
    diag = jnp.stack([lax.slice_in_dim(pad, (d - 1) * T - (T - 1) + S - 1 + 2 * T,
                                       (d - 1) * T + T + S - 1 + 2 * T, axis=1)
                      for d in range(3)], axis=1)
    w = jnp.concatenate([diag[..., ::-1], diag[..., :1]], axis=-1)
    flat = jnp.tile(w, (1, 1, T))[..., :T * (2 * T - 1)]
    band = flat.reshape(w.shape[0], 3, T, 2 * T - 1)[..., T - 1:]
    far_c = jnp.stack([dist_bias[:, 0], dist_bias[:, 2 * S - 2]], axis=1)
    return band, far_c


def _attn_kernel(far_ref, q_ref, k_ref, vt_ref, band_ref, lq1_ref, lk1_ref, lq2_ref, lk2_ref,
                 gain_ref, o_ref, qz_ref, s00, s01, s10, s11, e00, e01, e10, e11, acc0, acc1,
                 mt_ref, al_ref, m_ref, l_ref, *, T, nk, lam_init):
    h = pl.program_id(1)
    i = pl.program_id(2)
    s_buf = ((s00, s01), (s10, s11))
    e_buf = ((e00, e01), (e10, e11))
    acc = (acc0, acc1)
    c_left = far_ref[h, 0]
    c_right = far_ref[h, 1]

    qt = q_ref[...].astype(F32).T
    half = lax.broadcasted_iota(jnp.int32, qt.shape, 0) < DIFF_HEAD_DIM
    qz_ref[0] = jnp.where(half, qt, 0.0).astype(BF16)
    qz_ref[1] = jnp.where(half, 0.0, qt).astype(BF16)
    m_ref[...] = jnp.full(m_ref.shape, -jnp.inf, F32)
    l_ref[...] = jnp.zeros(l_ref.shape, F32)
    acc0[...] = jnp.zeros(acc0.shape, F32)
    acc1[...] = jnp.zeros(acc1.shape, F32)

    has_prev = i >= 1
    has_next = i <= nk - 2
    n_left = jnp.maximum(i - 1, 0) - jnp.where(has_next, 0, 1)
    right0 = i + 2 + jnp.where(has_prev, 0, 1)

    def tile_of(p):
        if isinstance(p, int) and p == 0:
            return i, band_ref[1], None
        if isinstance(p, int) and p == 1:
            return (jnp.where(has_prev, i - 1, i + 2),
                    jnp.where(has_prev, band_ref[0], c_right), None)
        if isinstance(p, int) and p == 2:
            return (jnp.where(has_next, i + 1, i - 2),
                    jnp.where(has_next, band_ref[2], c_left), None)
        f = p - 3
        is_left = f < n_left
        return (jnp.where(is_left, f, f - n_left + right0), None,
                jnp.where(is_left, c_left, c_right))

    def stage_a(p, x):
        j, bias, const = tile_of(p)
        kt = k_ref[pl.ds(pl.multiple_of(j * T, T), T), :]
        for c in range(2):
            s = _dot(kt, qz_ref[c])
            if bias is not None:
                s = s + bias
            s_buf[x][c][...] = s
            mt = jnp.max(s, axis=0, keepdims=True)
            mt_ref[2 * x + c] = mt if const is None else mt + const

    def stage_b(p, x):
        _, _, const = tile_of(p)
        for c in range(2):
            m_old = m_ref[c]
            m_new = jnp.maximum(m_old, mt_ref[2 * x + c])
            alpha = jnp.exp2(m_old - m_new)
            shift = m_new if const is None else m_new - const
            e = jnp.exp2(s_buf[x][c][...] - shift)
            l_ref[c] = alpha * l_ref[c] + jnp.sum(e, axis=0, keepdims=True)
            e_buf[x][c][...] = e.astype(BF16)
            al_ref[2 * x + c] = alpha
            m_ref[c] = m_new

    def stage_c(p, x):
        j, _, _ = tile_of(p)
        vt = vt_ref[j]
        for c in range(2):
            acc[c][...] = al_ref[2 * x + c] * acc[c][...] + _dot(vt, e_buf[x][c][...])

    def iteration(n, parity):
        static = isinstance(n, int)
        if not static or n < nk:
            stage_a(n, parity)
        if not static or 1 <= n <= nk:
            stage_b(n - 1, 1 - parity)
        if not static or 2 <= n <= nk + 1:
            stage_c(n - 2, parity)

    head = 5
    for n in range(head):
        iteration(n, n % 2)

    def pair(it, carry):
        n = head + 2 * it
        iteration(n, head % 2)
        iteration(n + 1, 1 - head % 2)
        return carry

    n_pairs = (nk - 1 - head) // 2
    lax.fori_loop(0, n_pairs, pair, 0)
    for n in range(head + 2 * n_pairs, nk + 2):
        iteration(n, n % 2)

    lam = (jnp.exp(jnp.sum(lq1_ref[...] * lk1_ref[...], keepdims=True))
           - jnp.exp(jnp.sum(lq2_ref[...] * lk2_ref[...], keepdims=True)) + lam_init)
    o = acc0[...] / l_ref[0] - lam * (acc1[...] / l_ref[1])
    ms = jnp.mean(o * o, axis=0, keepdims=True)
    y = o * lax.rsqrt(ms + LN_EPS) * gain_ref[...] * (1.0 - lam_init)
    o_ref[...] = y.T.astype(BF16)


def _attention(qk, vt, band, far_c, lq1, lk1, lq2, lk2, gain, lam_init):
    B, S, _ = qk.shape
    T = TOKEN_TILE
    nk = S // T
    assert nk >= 6, "the pipeline prologue assumes at least three far key tiles"
    H = N_DIFF_HEADS
    hw = 2 * DIFF_HEAD_DIM
    kern = functools.partial(_attn_kernel, T=T, nk=nk, lam_init=lam_init)
    vec = lambda a: a.reshape(1, DIFF_HEAD_DIM).astype(F32)
    return pl.pallas_call(
        kern,
        grid=(B, H, nk),
        in_specs=[pl.BlockSpec(memory_space=pltpu.SMEM),
                  pl.BlockSpec((None, T, hw), lambda b, h, i: (b, i, h)),
                  pl.BlockSpec((None, S, hw), lambda b, h, i: (b, 0, H + h)),
                  pl.BlockSpec((None, nk, hw, T), lambda b, h, i: (b, 0, h, 0)),
                  pl.BlockSpec((None, 3, T, T), lambda b, h, i: (h, 0, 0, 0)),
                  _const_spec((1, DIFF_HEAD_DIM)), _const_spec((1, DIFF_HEAD_DIM)),
                  _const_spec((1, DIFF_HEAD_DIM)), _const_spec((1, DIFF_HEAD_DIM)),
                  _const_spec((hw, 1))],
        out_specs=pl.BlockSpec((None, T, hw), lambda b, h, i: (b, i, h)),
        out_shape=jax.ShapeDtypeStruct((B, S, H * hw), BF16),
        scratch_shapes=([pltpu.VMEM((2, hw, T), BF16)]
                        + [pltpu.VMEM((T, T), F32)] * 4 + [pltpu.VMEM((T, T), BF16)] * 4
                        + [pltpu.VMEM((hw, T), F32)] * 2
                        + [pltpu.VMEM((4, 1, T), F32), pltpu.VMEM((4, 1, T), F32),
                           pltpu.VMEM((2, 1, T), F32), pltpu.VMEM((2, 1, T), F32)]),
        compiler_params=_params("parallel", "parallel", "parallel"),
        name="diff_attn",
    )(far_c, qk, qk, vt, band, vec(lq1), vec(lk1), vec(lq2), vec(lk2),
      gain.reshape(hw, 1).astype(F32))


def _pool_matrices(TM):
    r = np.arange(TM)[:, None]
    c = np.arange(TM)[None, :]
    hpos = np.concatenate([np.arange(-SUBLANES, 0), np.arange(TM, TM + SUBLANES),
                           np.full(LANES - 2 * SUBLANES, 10 ** 6)])[None, :]
    main, halo = [], []
    for w in POOL_WINDOWS:
        lo, hi = r - w // 2, r + w - w // 2
        main.append((c >= lo) & (c < hi))
        halo.append((hpos >= lo) & (hpos < hi))
    return (jnp.asarray(np.stack(main), BF16), jnp.asarray(np.stack(halo), BF16))


def _merge_kernel(x_ref, u_ref, up_ref, un_ref, o_ref, wg_ref, bg_ref, am_ref, ah_ref, wgrp_ref,
                  ps_ref, wbp_ref, wba_ref, wo_ref, lng_ref, lnb_ref, h_ref, ht_ref, *, TM, S, alpha):
    s_idx = pl.program_id(1)
    ns = pl.num_programs(1)
    x = x_ref[...]
    D = x.shape[1]
    xb = x.astype(BF16)
    gates = _sigmoid(_dot(xb, wg_ref[...]) + bg_ref[...])

    u = u_ref[...]
    P = u.shape[1]
    prev = jnp.where(s_idx > 0, up_ref[...], 0.0)
    nxt = jnp.where(s_idx < ns - 1, un_ref[...], 0.0)
    halo = jnp.concatenate([prev, nxt, jnp.zeros((LANES - 2 * SUBLANES, P), F32)], axis=0)
    u_hi, u_lo = _split_bf16(u)
    h_hi, h_lo = _split_bf16(halo)
    pos = s_idx * TM + lax.broadcasted_iota(jnp.int32, (TM, 1), 0)
    G = P // len(POOL_WINDOWS)
    mixed = []
    for g, w in enumerate(POOL_WINDOWS):
        sl = slice(g * G, (g + 1) * G)
        win2 = (_dot(am_ref[g], jnp.concatenate([u_hi[:, sl], u_lo[:, sl]], axis=1))
                + _dot(ah_ref[g], jnp.concatenate([h_hi[:, sl], h_lo[:, sl]], axis=1)))
        win = win2[:, :G] + win2[:, G:]
        cnt = (jnp.minimum(pos + (w - w // 2), S) - jnp.maximum(pos - w // 2, 0)).astype(F32)
        pooled = win / cnt - u[:, sl]
        mixed.append(_dot(pooled.astype(BF16), wgrp_ref[g]))
    mix = jnp.concatenate(mixed, axis=1) * ps_ref[...]
    y_pool = _dot(mix.astype(BF16), wbp_ref[...])
    y_attn = _dot(o_ref[...], wba_ref[...])
    merged = gates[:, :D] * y_pool + gates[:, D:] * y_attn
    y = _dot(merged.astype(BF16), wo_ref[...])
    h = _layer_norm(alpha * x + y, lng_ref[...], lnb_ref[...])
    h_ref[...] = h
    _to_row_tiles(ht_ref, h)


def _merge(x, u, o, w_gate, b_gate, w_grp, pool_scale, w_bp, w_ba, w_out, ln_g, ln_b, alpha):
    B, S, D = x.shape
    P = u.shape[2]
    A = o.shape[2]
    TM = MERGE_TILE
    ns = S // TM
    hb = TM // SUBLANES
    am, ah = _pool_matrices(TM)
    nw = len(POOL_WINDOWS)
    G = P // nw
    row = lambda a: a.reshape(1, -1).astype(F32)
    kern = functools.partial(_merge_kernel, TM=TM, S=S, alpha=alpha)
    return pl.pallas_call(
        kern,
        grid=(B, ns),
        in_specs=[pl.BlockSpec((None, TM, D), lambda b, s: (b, s, 0)),
                  pl.BlockSpec((None, TM, P), lambda b, s: (b, s, 0)),
                  pl.BlockSpec((None, SUBLANES, P),
                               lambda b, s: (b, jnp.maximum(s * hb - 1, 0), 0)),
                  pl.BlockSpec((None, SUBLANES, P),
                               lambda b, s: (b, jnp.minimum((s + 1) * hb, S // SUBLANES - 1), 0)),
                  pl.BlockSpec((None, TM, A), lambda b, s: (b, s, 0)),
                  _const_spec((D, 2 * D)), _const_spec((1, 2 * D)),
                  _const_spec((nw, TM, TM)), _const_spec((nw, TM, LANES)),
                  _const_spec((nw, G, G)), _const_spec((1, P)),
                  _const_spec((P, D)), _const_spec((A, D)), _const_spec((D, D)),
                  _const_spec((1, D)), _const_spec((1, D))],
        out_specs=[pl.BlockSpec((None, TM, D), lambda b, s: (b, s, 0)),
                   pl.BlockSpec((None, TM * SUBLANES, LANES), lambda b, s: (b, s, 0))],
        out_shape=[jax.ShapeDtypeStruct((B, S, D), F32),
                   jax.ShapeDtypeStruct((B, S * SUBLANES, LANES), F32)],
        compiler_params=_params("parallel", "parallel"),
        name="merge",
    )(x, u, u, u, o, w_gate, row(b_gate), am, ah, w_grp, row(pool_scale), w_bp, w_ba, w_out,
      row(ln_g), row(ln_b))


def _route_kernel(h_ref, wrh_ref, wrl_ref, rb_ref, tri_ref, ones_ref, wsg_ref, wsu_ref, wsd_ref,
                  idx_ref, wt_ref, rank_ref, cnt_ref, pre_ref, cnt_acc, *, TM, E, alpha):
    t = pl.program_id(0)

    @pl.when(t == 0)
    def _():
        cnt_acc[...] = jnp.zeros(cnt_acc.shape, F32)

    h = h_ref[...]
    hb, hl = _split_bf16(h)
    logits = (lax.dot_general(wrh_ref[...], hb, _NT, preferred_element_type=F32)
              + lax.dot_general(wrh_ref[...], hl, _NT, preferred_element_type=F32)
              + lax.dot_general(wrl_ref[...], hb, _NT, preferred_element_type=F32))
    s = _sigmoid(logits)
    biased = s + rb_ref[...]
    gsz = E // N_EXPERT_GROUPS
    sub = lax.broadcasted_iota(jnp.int32, (gsz, TM), 0).astype(F32)
    neg = -jnp.inf
    grp, gscore = [], []
    for g in range(N_EXPERT_GROUPS):
        bg = biased[g * gsz:(g + 1) * gsz, :]
        m1 = jnp.max(bg, axis=0, keepdims=True)
        first = jnp.min(jnp.where(bg == m1, sub, float(gsz)), axis=0, keepdims=True)
        m2 = jnp.max(jnp.where(sub == first, neg, bg), axis=0, keepdims=True)
        grp.append(bg)
        gscore.append(m1 + m2)
    masked = []
    for g in range(N_EXPERT_GROUPS):
        beaten = jnp.zeros((1, TM), F32)
        for g2 in range(N_EXPERT_GROUPS):
            if g2 == g:
                continue
            wins = (gscore[g2] > gscore[g]) if g2 > g else (gscore[g2] >= gscore[g])
            beaten = beaten + wins.astype(F32)
        masked.append(jnp.where(beaten < float(TOPK_GROUPS), grp[g], neg))
    masked = jnp.concatenate(masked, axis=0)

    row = lax.broadcasted_iota(jnp.int32, (E, TM), 0).astype(F32)
    sel = jnp.zeros((E, TM), F32)
    firsts, ws = [], []
    for _ in range(TOP_K):
        mx = jnp.max(masked, axis=0, keepdims=True)
        first = jnp.min(jnp.where(masked == mx, row, float(E)), axis=0, keepdims=True)
        oh = row == first
        ws.append(jnp.sum(jnp.where(oh, s, 0.0), axis=0, keepdims=True))
        masked = jnp.where(oh, neg, masked)
        sel = jnp.where(oh, 1.0, sel)
        firsts.append(first)
    wsum = ws[0]
    for w in ws[1:]:
        wsum = wsum + w

    sel_b = sel.astype(BF16)
    base = jnp.concatenate([cnt_acc[...]] * (TM // LANES), axis=1)
    rank_all = _dot(sel_b, tri_ref[...]) + base
    cnt_acc[...] = cnt_acc[...] + _dot(sel_b, ones_ref[...])
    cnt_ref[...] = cnt_acc[...]
    for k in range(TOP_K):
        oh = row == firsts[k]
        idx_k = firsts[k].astype(jnp.int32)
        wt_k = ws[k] / wsum * ROUTED_SCALE
        rank_k = jnp.sum(jnp.where(oh, rank_all, 0.0), axis=0, keepdims=True).astype(jnp.int32)
        for c in range(TM // LANES):
            sl = slice(c * LANES, (c + 1) * LANES)
            idx_ref[c, k:k + 1, :] = idx_k[:, sl]
            wt_ref[c, k:k + 1, :] = wt_k[:, sl]
            rank_ref[c, k:k + 1, :] = rank_k[:, sl]

    g_act = _dot(hb, wsg_ref[...])
    hid = g_act * _sigmoid(g_act) * _dot(hb, wsu_ref[...])
    pre_ref[...] = alpha * h + _dot(hid.astype(BF16), wsd_ref[...])


def _route(h, wr_t_hi, wr_t_lo, router_bias, w_sg, w_su, w_sd, alpha):
    N, D = h.shape
    E = wr_t_hi.shape[0]
    TM = TOKEN_TILE
    Hs = w_sg.shape[1]
    tri = jnp.asarray(np.triu(np.ones((TM, TM), np.float32), k=1), BF16)
    ones = jnp.ones((TM, LANES), BF16)
    kern = functools.partial(_route_kernel, TM=TM, E=E, alpha=alpha)
    kt = pl.BlockSpec((TM // LANES, TOP_K, LANES), lambda t: (t, 0, 0))
    return pl.pallas_call(
        kern,
        grid=(N // TM,),
        in_specs=[pl.BlockSpec((TM, D), lambda t: (t, 0)),
                  _const_spec((E, D)), _const_spec((E, D)), _const_spec((E, 1)),
                  _const_spec((TM, TM)), _const_spec((TM, LANES)),
                  _const_spec((D, Hs)), _const_spec((D, Hs)), _const_spec((Hs, D))],
        out_specs=[kt, kt, kt, _const_spec((E, LANES)), pl.BlockSpec((TM, D), lambda t: (t, 0))],
        out_shape=[jax.ShapeDtypeStruct((N // LANES, TOP_K, LANES), jnp.int32),
                   jax.ShapeDtypeStruct((N // LANES, TOP_K, LANES), F32),
                   jax.ShapeDtypeStruct((N // LANES, TOP_K, LANES), jnp.int32),
                   jax.ShapeDtypeStruct((E, LANES), F32),
                   jax.ShapeDtypeStruct((N, D), F32)],
        scratch_shapes=[pltpu.VMEM((E, LANES), F32)],
        compiler_params=_params("arbitrary"),
        name="route_shared",
    )(h, wr_t_hi, wr_t_lo, router_bias.reshape(E, 1).astype(F32), tri, ones, w_sg, w_su, w_sd)


def _slots_kernel(idx_ref, rank_ref, rowstart_ref, dest_ref, *, E):
    expert = lax.broadcasted_iota(jnp.int32, (E, LANES), 0)
    rowstart = rowstart_ref[...]
    for c in range(idx_ref.shape[0]):
        idx = idx_ref[c]
        rank = rank_ref[c]
        for k in range(TOP_K):
            base = jnp.sum(jnp.where(expert == idx[k:k + 1, :], rowstart, 0.0), axis=0,
                           keepdims=True)
            dest_ref[c, k:k + 1, :] = base.astype(jnp.int32) + rank[k:k + 1, :]


def _slots(idx, rank, rowstart):
    nb = idx.shape[0]
    E = rowstart.shape[0]
    step = SUBLANES
    spec = pl.BlockSpec((step, TOP_K, LANES), lambda t: (t, 0, 0))
    return pl.pallas_call(
        functools.partial(_slots_kernel, E=E),
        grid=(nb // step,),
        in_specs=[spec, spec, _const_spec((E, 1))],
        out_specs=spec,
        out_shape=jax.ShapeDtypeStruct(idx.shape, jnp.int32),
        compiler_params=_params("parallel"),
        name="slots",
    )(idx, rank, rowstart.astype(F32).reshape(E, 1))


def _dispatch_kernel(rowstart_ref, cnt_ref, nact_ref, dest_ref, h_ref, xs_hbm,
                     stage, zeros, sem, zsem, *, TM, BLK, NB, E):
    t = pl.program_id(0)
    nt = pl.num_programs(0)
    slot = t % 2
    R = SUBLANES

    def zero_copy(first, n, start):
        cp = pltpu.make_async_copy(zeros.at[pl.ds(0, n * R)],
                                   xs_hbm.at[pl.ds(pl.multiple_of(first * R, R), n * R)], zsem)
        cp.start() if start else cp.wait()

    def wait_slot(s):
        for _ in range(TOP_K):
            pltpu.make_async_copy(stage.at[s], xs_hbm.at[pl.ds(0, TM * R)], sem.at[s]).wait()

    @pl.when(t == 0)
    def _():
        zeros[...] = jnp.zeros(zeros.shape, F32)
        nact = nact_ref[0]

        def tail(e, c, *, start):
            first = rowstart_ref[e] + cnt_ref[e]
            n_pad = (cnt_ref[e] + BLK - 1) // BLK * BLK - cnt_ref[e]
            done = jnp.int32(0)
            size = BLK // 2
            while size >= 1:
                piece = n_pad & size

                @pl.when(piece != 0)
                def _(size=size, done=done):
                    zero_copy(first + done, size, start)

                done = done + piece
                size //= 2
            return c

        def idle(b, c, *, start):
            zero_copy(b * BLK, BLK, start)
            return c

        for start in (True, False):
            lax.fori_loop(0, E, functools.partial(tail, start=start), 0)
            lax.fori_loop(nact, NB, functools.partial(idle, start=start), 0)

    @pl.when(t >= 2)
    def _():
        wait_slot(slot)

    stage[slot] = h_ref[...]
    for c in range(TM // LANES):
        def row_body(r, carry, c=c):
            src = pl.multiple_of((c * LANES + r) * R, R)
            for k in range(TOP_K):
                dst = pl.multiple_of(dest_ref[c, k, r] * R, R)
                pltpu.make_async_copy(stage.at[slot, pl.ds(src, R)], xs_hbm.at[pl.ds(dst, R)],
                                      sem.at[slot]).start(priority=k % 2)
            return carry

        lax.fori_loop(0, LANES, row_body, 0)

    @pl.when(t == nt - 1)
    def _():
        @pl.when(t >= 1)
        def _():
            wait_slot(1 - slot)

        wait_slot(slot)


def _dispatch(h_tiles, dest, rowstart, counts, nact, NB):
    N = h_tiles.shape[0] // SUBLANES
    E = rowstart.shape[0]
    TM = MERGE_TILE
    BLK = EXPERT_BLOCK
    R = SUBLANES
    kern = functools.partial(_dispatch_kernel, TM=TM, BLK=BLK, NB=NB, E=E)
    grid_spec = pltpu.PrefetchScalarGridSpec(
        num_scalar_prefetch=3,
        grid=(N // TM,),
        in_specs=[pl.BlockSpec((TM // LANES, TOP_K, LANES), lambda t, *_: (t, 0, 0),
                               memory_space=pltpu.SMEM),
                  pl.BlockSpec((TM * R, LANES), lambda t, *_: (t, 0))],
        out_specs=pl.BlockSpec(memory_space=pl.ANY),
        scratch_shapes=[pltpu.VMEM((2, TM * R, LANES), F32), pltpu.VMEM((BLK * R, LANES), F32),
                        pltpu.SemaphoreType.DMA((2,)), pltpu.SemaphoreType.DMA(())],
    )
    return pl.pallas_call(
        kern,
        grid_spec=grid_spec,
        out_shape=jax.ShapeDtypeStruct((NB * BLK * R, LANES), F32),
        compiler_params=_params("arbitrary"),
        name="dispatch",
    )(rowstart, counts, nact, dest, h_tiles)


def _expert_kernel(first_ref, nblk_ref, nact_ref, wg_ref, wu_ref, wd_ref, x_hbm, y_hbm,
                   xbuf, ybuf, wg_s, wu_s, wd_s, xsem, ysem, *, BLK, NB):
    e = pl.program_id(0)
    nact = nact_ref[0]
    rows = BLK * SUBLANES

    class _Parts:
        def __init__(self, hbm, buf, g, slot, sem, to_hbm):
            part = rows // STREAMS
            self.copies = []
            for p in range(STREAMS):
                far = hbm.at[pl.ds(pl.multiple_of(g * rows + p * part, part), part)]
                near = buf.at[slot, pl.ds(p * part, part)]
                src, dst = (near, far) if to_hbm else (far, near)
                self.copies.append(pltpu.make_async_copy(src, dst, sem.at[slot]))

        def start(self):
            for p, cp in enumerate(self.copies):
                cp.start(priority=p % 2)

        def wait(self):
            for cp in self.copies:
                cp.wait()

    def x_copy(g, slot):
        return _Parts(x_hbm, xbuf, g, slot, xsem, False)

    def y_copy(g, slot):
        return _Parts(y_hbm, ybuf, g, slot, ysem, True)

    ahead = SLOTS - 1

    @pl.when(e == 0)
    def _():
        for g0 in range(ahead):
            @pl.when(g0 < nact)
            def _(g0=g0):
                x_copy(g0, g0).start()

    @pl.when(nblk_ref[e] > 0)
    def _():
        wg_s[...] = wg_ref[...].astype(BF16)
        wu_s[...] = wu_ref[...].astype(BF16)
        wd_s[...] = wd_ref[...].astype(BF16)

    def body(j, carry):
        g = first_ref[e] + j
        slot = g % SLOTS
        x_copy(g, slot).wait()

        @pl.when(g + ahead < nact)
        def _():
            x_copy(g + ahead, (g + ahead) % SLOTS).start()

        @pl.when(g >= SLOTS)
        def _():
            y_copy(g - SLOTS, slot).wait()

        x = _from_row_tiles(xbuf.at[slot], 0, BLK).astype(BF16)
        g_act = _dot(x, wg_s[...])
        hid = g_act * _sigmoid(g_act) * _dot(x, wu_s[...])
        _to_row_tiles(ybuf.at[slot], _dot(hid.astype(BF16), wd_s[...]))
        y_copy(g, slot).start()
        return carry

    lax.fori_loop(0, nblk_ref[e], body, 0)

    @pl.when(e == pl.num_programs(0) - 1)
    def _():
        for back in range(SLOTS, 0, -1):
            @pl.when(nact >= back)
            def _(back=back):
                y_copy(nact - back, (nact - back) % SLOTS).wait()

        ybuf[0] = jnp.zeros(ybuf.shape[1:], F32)

        def idle(g, carry, *, start):
            cp = y_copy(g, 0)
            cp.start() if start else cp.wait()
            return carry

        for start in (True, False):
            lax.fori_loop(nact, NB, functools.partial(idle, start=start), 0)


def _experts(x_sorted, first_blk, nblk_e, nact, NB, w_eg, w_eu, w_ed):
    E, D, He = w_eg.shape
    BLK = EXPERT_BLOCK
    R = SUBLANES
    per_expert = lambda shape: pl.BlockSpec((None,) + shape, lambda e, *_: (e, 0, 0))
    grid_spec = pltpu.PrefetchScalarGridSpec(
        num_scalar_prefetch=3,
        grid=(E,),
        in_specs=[per_expert((D, He)), per_expert((D, He)), per_expert((He, D)),
                  pl.BlockSpec(memory_space=pl.ANY)],
        out_specs=pl.BlockSpec(memory_space=pl.ANY),
        scratch_shapes=[pltpu.VMEM((SLOTS, BLK * R, LANES), F32),
                        pltpu.VMEM((SLOTS, BLK * R, LANES), F32),
                        pltpu.VMEM((D, He), BF16), pltpu.VMEM((D, He), BF16),
                        pltpu.VMEM((He, D), BF16),
                        pltpu.SemaphoreType.DMA((SLOTS,)), pltpu.SemaphoreType.DMA((SLOTS,))],
    )
    return pl.pallas_call(
        functools.partial(_expert_kernel, BLK=BLK, NB=NB),
        grid_spec=grid_spec,
        out_shape=jax.ShapeDtypeStruct((NB * BLK * R, LANES), F32),
        compiler_params=_params("arbitrary"),
        name="routed_experts",
    )(first_blk, nblk_e, nact, w_eg, w_eu, w_ed, x_sorted)


def _combine_kernel(dest_ref, wt_ref, pre_ref, lng_ref, lnb_ref, y_hbm, o_ref, buf, sem, *, TM):
    t = pl.program_id(0)
    nt = pl.num_programs(0) - 1
    slot = t % 2
    R = SUBLANES

    @pl.when(t < nt)
    def _():
        for k in range(TOP_K):
            for r in range(TM):
                src = pl.multiple_of(dest_ref[0, k, r] * R, R)
                pltpu.make_async_copy(y_hbm.at[pl.ds(src, R)],
                                      buf.at[slot, pl.ds((k * TM + r) * R, R)],
                                      sem.at[slot]).start(priority=r % 2)

    @pl.when(t >= 1)
    def _():
        prev = 1 - slot
        for k in range(TOP_K):
            pltpu.make_async_copy(y_hbm.at[pl.ds(0, TM * R)],
                                  buf.at[prev, pl.ds(k * TM * R, TM * R)], sem.at[prev]).wait()
        w = jnp.concatenate([wt_ref[0], jnp.zeros((TM - TOP_K, TM), F32)], axis=0).T
        z = pre_ref[...]
        rows = buf.at[prev]
        for k in range(TOP_K):
            z = z + w[:, k:k + 1] * _from_row_tiles(rows, k * TM * R, TM)
        o_ref[...] = _layer_norm(z, lng_ref[...], lnb_ref[...])


def _combine(y_sorted, dest, wts, pre, ln_g, ln_b):
    N, D = pre.shape
    TM = LANES
    nt = N // TM
    R = SUBLANES
    kern = functools.partial(_combine_kernel, TM=TM)
    row = lambda a: a.reshape(1, -1).astype(F32)
    ahead = lambda t: (jnp.minimum(t, nt - 1), 0, 0)
    behind3 = lambda t: (jnp.maximum(t - 1, 0), 0, 0)
    behind = lambda t: (jnp.maximum(t - 1, 0), 0)
    return pl.pallas_call(
        kern,
        grid=(nt + 1,),
        in_specs=[pl.BlockSpec((1, TOP_K, TM), ahead, memory_space=pltpu.SMEM),
                  pl.BlockSpec((1, TOP_K, TM), behind3),
                  pl.BlockSpec((TM, D), behind),
                  _const_spec((1, D)), _const_spec((1, D)),
                  pl.BlockSpec(memory_space=pl.ANY)],
        out_specs=pl.BlockSpec((TM, D), behind),
        out_shape=jax.ShapeDtypeStruct((N, D), F32),
        scratch_shapes=[pltpu.VMEM((2, TOP_K * TM * R, LANES), F32),
                        pltpu.SemaphoreType.DMA((2,))],
        compiler_params=_params("arbitrary"),
        name="combine",
    )(dest, wts, pre, row(ln_g), row(ln_b), y_sorted)


def _dispatch_meta(cnt, N, E):
    BLK = EXPERT_BLOCK
    NB = (N * TOP_K + E * (BLK - 1)) // BLK
    counts = cnt[:, 0].astype(jnp.int32)
    nblk_e = (counts + BLK - 1) // BLK
    bend = jnp.cumsum(nblk_e)
    first_blk = (bend - nblk_e).astype(jnp.int32)
    rowstart = first_blk * BLK
    nact = bend[-1:].astype(jnp.int32)
    return counts, rowstart, first_blk, nblk_e.astype(jnp.int32), nact, NB


def kernel(x, rel_bias_table, w_in, b_gate, w_pool_grp, pool_scale, w_branch_pool, lambda_q1,
           lambda_k1, lambda_q2, lambda_k2, subln_gain, w_branch_attn, w_out, ln1_g, ln1_b,
           w_router, router_bias, w_exp_gate, w_exp_up, w_exp_down, w_sh_gate, w_sh_up,
           w_sh_down, ln2_g, ln2_b):
    B, S, D = x.shape
    L = w_in.shape[0]
    E = w_router.shape[2]
    P = pool_scale.shape[1]
    A = w_branch_attn.shape[1]
    N = B * S
    alpha = (2 * L) ** 0.25
    assert S % TOKEN_TILE == 0 and S % MERGE_TILE == 0
    assert A == N_DIFF_HEADS * 2 * DIFF_HEAD_DIM and TOKEN_TILE >= REL_MAX_DIST
    assert D == SUBLANES * LANES, "a row tile holds exactly one (SUBLANES, LANES) tile per row"
    band, far_c = _bias_tables(rel_bias_table, S, TOKEN_TILE)

    h = x
    for i in range(L):
        w_uqkv = w_in[i][:, :P + 3 * A].astype(BF16)
        w_gate = w_in[i][:, P + 3 * A:].astype(BF16)
        u, qk, vt = _inproj(h, w_uqkv, P, A)
        o = _attention(qk, vt, band, far_c, lambda_q1[i], lambda_k1[i], lambda_q2[i],
                       lambda_k2[i], subln_gain[i], _lambda_init(i))
        h1, h1_tiles = _merge(h, u, o, w_gate, b_gate[i], w_pool_grp[i].astype(BF16), pool_scale[i],
                    w_branch_pool[i].astype(BF16), w_branch_attn[i].astype(BF16),
                    w_out[i].astype(BF16), ln1_g[i], ln1_b[i], alpha)
        t = h1.reshape(N, D)
        wr_hi, wr_lo = _split_bf16(w_router[i].astype(F32).T)
        idx, wts, rank, cnt, pre = _route(t, wr_hi, wr_lo, router_bias[i],
                                          w_sh_gate[i].astype(BF16), w_sh_up[i].astype(BF16),
                                          w_sh_down[i].astype(BF16), alpha)
        counts, rowstart, first_blk, nblk_e, nact, NB = _dispatch_meta(cnt, N, E)
        dest = _slots(idx, rank, rowstart)
        x_sorted = _dispatch(h1_tiles.reshape(N * SUBLANES, LANES), dest, rowstart, counts, nact, NB)
        y_sorted = _experts(x_sorted, first_blk, nblk_e, nact, NB, w_exp_gate[i], w_exp_up[i],
                            w_exp_down[i])
        h = _combine(y_sorted, dest, wts, pre, ln2_g[i], ln2_b[i]).reshape(B, S, D)
    return h
```

```python
import functools
import math

import numpy as np
import jax
import jax.numpy as jnp
from jax import lax
from jax.experimental import pallas as pl
from jax.experimental.pallas import tpu as pltpu

F32 = jnp.float32
BF16 = jnp.bfloat16

POOL_WINDOWS = (2, 4, 8, 16)
N_DIFF_HEADS = 4
DIFF_HEAD_DIM = 64
REL_BUCKETS = 32
REL_MAX_DIST = 128
TOP_K = 8
N_EXPERT_GROUPS = 8
TOPK_GROUPS = 4
ROUTED_SCALE = 2.5
LN_EPS = 1e-5
LOG2E = math.log2(math.e)

LANES = 128
SUBLANES = 8
VMEM_LIMIT_BYTES = 56 * 1024 * 1024

TOKEN_TILE = 512
MERGE_TILE = 256
EXPERT_BLOCK = 256
SLOTS = 8

_NT = (((1,), (1,)), ((), ()))


def _lambda_init(layer_idx):
    return 0.8 - 0.6 * math.exp(-0.3 * layer_idx)


def _dot(a, b):
    return jnp.dot(a, b, preferred_element_type=F32)


def _split_bf16(a):
    hi = a.astype(BF16)
    lo = (a - hi.astype(F32)).astype(BF16)
    return hi, lo


def _sigmoid(z):
    return 1.0 / (1.0 + jnp.exp(-z))


def _layer_norm(z, g, b):
    mu = jnp.mean(z, axis=-1, keepdims=True)
    zc = z - mu
    var = jnp.mean(zc * zc, axis=-1, keepdims=True)
    return zc * lax.rsqrt(var + LN_EPS) * g + b


def _params(*sem):
    return pltpu.CompilerParams(dimension_semantics=sem, vmem_limit_bytes=VMEM_LIMIT_BYTES)


def _const_spec(shape):
    nd = len(shape)
    return pl.BlockSpec(shape, lambda *_: (0,) * nd)


def _to_row_tiles(ref, x):
    rows = x.shape[0]
    for s in range(SUBLANES):
        ref[pl.ds(s, rows, stride=SUBLANES), :] = x[:, s * LANES:(s + 1) * LANES]


def _from_row_tiles(ref, start, rows):
    return jnp.concatenate([ref[pl.ds(start + s, rows, stride=SUBLANES), :]
                            for s in range(SUBLANES)], axis=1)


def _inproj_kernel(x_ref, w_ref, u_ref, qk_ref, vt_ref, *, pool_w, attn_w, q_scale):
    xb = x_ref[...].astype(BF16)
    p = _dot(xb, w_ref[...])
    u_ref[...] = p[:, :pool_w]
    q = p[:, pool_w:pool_w + attn_w] * q_scale
    k = p[:, pool_w + attn_w:pool_w + 2 * attn_w]
    qk_ref[:, :attn_w] = q.astype(BF16)
    qk_ref[:, attn_w:] = k.astype(BF16)
    v = p[:, pool_w + 2 * attn_w:pool_w + 3 * attn_w]
    vt_ref[...] = v.T.astype(BF16)


def _inproj(x, w_uqkv, pool_w, attn_w):
    B, S, D = x.shape
    T = TOKEN_TILE
    ns = S // T
    width = w_uqkv.shape[1]
    kern = functools.partial(_inproj_kernel, pool_w=pool_w, attn_w=attn_w,
                             q_scale=DIFF_HEAD_DIM ** -0.5 * LOG2E)
    return pl.pallas_call(
        kern,
        grid=(B, ns),
        in_specs=[pl.BlockSpec((None, T, D), lambda b, s: (b, s, 0)),
                  _const_spec((D, width))],
        out_specs=[pl.BlockSpec((None, T, pool_w), lambda b, s: (b, s, 0)),
                   pl.BlockSpec((None, T, 2 * attn_w), lambda b, s: (b, s, 0)),
                   pl.BlockSpec((None, None, attn_w, T), lambda b, s: (b, s, 0, 0))],
        out_shape=[jax.ShapeDtypeStruct((B, S, pool_w), F32),
                   jax.ShapeDtypeStruct((B, S, 2 * attn_w), BF16),
                   jax.ShapeDtypeStruct((B, ns, attn_w, T), BF16)],
        compiler_params=_params("parallel", "parallel"),
        name="inproj",
    )(x, w_uqkv)


def _t5_bucket_np(rel):
    half = REL_BUCKETS // 2
    max_exact = half // 2
    ret = np.where(rel > 0, half, 0)
    n = np.abs(rel)
    nf = np.maximum(n, 1).astype(np.float32)
    large = max_exact + (np.log(nf / max_exact) / math.log(REL_MAX_DIST / max_exact)
                         * (half - max_exact)).astype(np.int32)
    large = np.minimum(large, half - 1)
    return ret + np.where(n < max_exact, n, large)


def _t5_bucket(rel):
    half = REL_BUCKETS // 2
    max_exact = half // 2
    ret = jnp.where(rel > 0, half, 0)
    n = jnp.abs(rel)
    nf = jnp.maximum(n, 1).astype(F32)
    large = max_exact + (jnp.log(nf / max_exact) / math.log(REL_MAX_DIST / max_exact)
                         * (half - max_exact)).astype(jnp.int32)
    large = jnp.minimum(large, half - 1)
    return ret + jnp.where(n < max_exact, n, large)


def _bias_tables(table, S, T):
    far = np.arange(T + 1, S)
    if far.size:
        assert np.all(_t5_bucket_np(-far) == _t5_bucket_np(-far[-1]))
        assert np.all(_t5_bucket_np(far) == _t5_bucket_np(far[-1]))
    rel = jnp.arange(-(S - 1), S, dtype=jnp.int32)
    onehot = _t5_bucket(rel)[None, :, None] == jnp.arange(REL_BUCKETS)[None, None, :]
    dist_bias = jnp.sum(jnp.where(onehot, table.astype(F32).T[:, None, :] * LOG2E, 0.0),
                        axis=-1)
    pad = jnp.pad(dist_bias, ((0, 0), (2 * T, 2 * T)), mode='edge')
    centre = [(d - 1) * T + S - 1 + 2 * T for d in range(3)]
    w = jnp.stack([jnp.concatenate(
        [lax.slice_in_dim(pad, c0 - T + 1, c0 + 1, axis=1)[:, ::-1],
         lax.slice_in_dim(pad, c0 + 1, c0 + T + 1, axis=1)[:, ::-1]], axis=1)
        for c0 in centre], axis=1)
    H = w.shape[0]
    band = pl.pallas_call(
        functools.partial(_band_kernel, T=T),
        grid=(H, 3),
        in_specs=[pl.BlockSpec((None, None, 1, 2 * T), lambda h, d: (h, d, 0, 0))],
        out_specs=pl.BlockSpec((None, None, T, T), lambda h, d: (h, d, 0, 0)),
        out_shape=jax.ShapeDtypeStruct((H, 3, T, T), F32),
        compiler_params=_params("parallel", "parallel"),
        name="band_tiles",
    )(w.reshape(H, 3, 1, 2 * T))
    far_c = jnp.stack([dist_bias[:, 0], dist_bias[:, 2 * S - 2]], axis=1)
    return band, far_c


def _band_kernel(w_ref, o_ref, *, T):
    x = jnp.broadcast_to(w_ref[...], (T, 2 * T))
    o_ref[...] = pltpu.roll(x, 0, 1, stride=1, stride_axis=0)[:, :T]


def _attn_kernel(far_ref, q_ref, k_ref, vt_ref, band_ref, lq1_ref, lk1_ref, lq2_ref, lk2_ref,
                 gain_ref, o_ref, qz_ref, s00, s01, s10, s11, e00, e01, e10, e11, acc0, acc1,
                 mt_ref, al_ref, m_ref, l_ref, *, T, nk, lam_init):
    h = pl.program_id(1)
    i = pl.program_id(2)
    s_buf = ((s00, s01), (s10, s11))
    e_buf = ((e00, e01), (e10, e11))
    acc = (acc0, acc1)
    c_left = far_ref[h, 0]
    c_right = far_ref[h, 1]

    qt = q_ref[...].astype(F32).T
    half = lax.broadcasted_iota(jnp.int32, qt.shape, 0) < DIFF_HEAD_DIM
    qz_ref[0] = jnp.where(half, qt, 0.0).astype(BF16)
    qz_ref[1] = jnp.where(half, 0.0, qt).astype(BF16)
    m_ref[...] = jnp.full(m_ref.shape, -jnp.inf, F32)
    l_ref[...] = jnp.zeros(l_ref.shape, F32)
    acc0[...] = jnp.zeros(acc0.shape, F32)
    acc1[...] = jnp.zeros(acc1.shape, F32)

    has_prev = i >= 1
    has_next = i <= nk - 2
    n_left = jnp.maximum(i - 1, 0) - jnp.where(has_next, 0, 1)
    right0 = i + 2 + jnp.where(has_prev, 0, 1)

    def tile_of(p):
        if isinstance(p, int) and p == 0:
            return i, band_ref[1], None
        if isinstance(p, int) and p == 1:
            return (jnp.where(has_prev, i - 1, i + 2),
                    jnp.where(has_prev, band_ref[0], c_right), None)
        if isinstance(p, int) and p == 2:
            return (jnp.where(has_next, i + 1, i - 2),
                    jnp.where(has_next, band_ref[2], c_left), None)
        f = p - 3
        is_left = f < n_left
        return (jnp.where(is_left, f, f - n_left + right0), None,
                jnp.where(is_left, c_left, c_right))

    def stage_a(p, x):
        j, bias, const = tile_of(p)
        kt = k_ref[pl.ds(pl.multiple_of(j * T, T), T), :]
        for c in range(2):
            s = _dot(kt, qz_ref[c])
            if bias is not None:
                s = s + bias
            s_buf[x][c][...] = s
            mt = jnp.max(s, axis=0, keepdims=True)
            mt_ref[2 * x + c] = mt if const is None else mt + const

    def stage_b(p, x):
        _, _, const = tile_of(p)
        for c in range(2):
            m_old = m_ref[c]
            m_new = jnp.maximum(m_old, mt_ref[2 * x + c])
            alpha = jnp.exp2(m_old - m_new)
            shift = m_new if const is None else m_new - const
            e = jnp.exp2(s_buf[x][c][...] - shift)
            l_ref[c] = alpha * l_ref[c] + jnp.sum(e, axis=0, keepdims=True)
            e_buf[x][c][...] = e.astype(BF16)
            al_ref[2 * x + c] = alpha
            m_ref[c] = m_new

    def stage_c(p, x):
        j, _, _ = tile_of(p)
        vt = vt_ref[j]
        for c in range(2):
            acc[c][...] = al_ref[2 * x + c] * acc[c][...] + _dot(vt, e_buf[x][c][...])

    def iteration(n, parity):
        static = isinstance(n, int)
        if not static or n < nk:
            stage_a(n, parity)
        if not static or 1 <= n <= nk:
            stage_b(n - 1, 1 - parity)
        if not static or 2 <= n <= nk + 1:
            stage_c(n - 2, parity)

    head = 5
    for n in range(head):
        iteration(n, n % 2)

    def pair(it, carry):
        n = head + 2 * it
        iteration(n, head % 2)
        iteration(n + 1, 1 - head % 2)
        return carry

    n_pairs = (nk - 1 - head) // 2
    lax.fori_loop(0, n_pairs, pair, 0)
    for n in range(head + 2 * n_pairs, nk + 2):
        iteration(n, n % 2)

    lam = (jnp.exp(jnp.sum(lq1_ref[...] * lk1_ref[...], keepdims=True))
           - jnp.exp(jnp.sum(lq2_ref[...] * lk2_ref[...], keepdims=True)) + lam_init)
    o = acc0[...] / l_ref[0] - lam * (acc1[...] / l_ref[1])
    ms = jnp.mean(o * o, axis=0, keepdims=True)
    y = o * lax.rsqrt(ms + LN_EPS) * gain_ref[...] * (1.0 - lam_init)
    o_ref[...] = y.T.astype(BF16)


def _attention(qk, vt, band, far_c, lq1, lk1, lq2, lk2, gain, lam_init):
    B, S, _ = qk.shape
    T = TOKEN_TILE
    nk = S // T
    assert nk >= 6, "the pipeline prologue assumes at least three far key tiles"
    H = N_DIFF_HEADS
    hw = 2 * DIFF_HEAD_DIM
    kern = functools.partial(_attn_kernel, T=T, nk=nk, lam_init=lam_init)
    vec = lambda a: a.reshape(1, DIFF_HEAD_DIM).astype(F32)
    return pl.pallas_call(
        kern,
        grid=(B, H, nk),
        in_specs=[pl.BlockSpec(memory_space=pltpu.SMEM),
                  pl.BlockSpec((None, T, hw), lambda b, h, i: (b, i, h)),
                  pl.BlockSpec((None, S, hw), lambda b, h, i: (b, 0, H + h)),
                  pl.BlockSpec((None, nk, hw, T), lambda b, h, i: (b, 0, h, 0)),
                  pl.BlockSpec((None, 3, T, T), lambda b, h, i: (h, 0, 0, 0)),
                  _const_spec((1, DIFF_HEAD_DIM)), _const_spec((1, DIFF_HEAD_DIM)),
                  _const_spec((1, DIFF_HEAD_DIM)), _const_spec((1, DIFF_HEAD_DIM)),
                  _const_spec((hw, 1))],
        out_specs=pl.BlockSpec((None, T, hw), lambda b, h, i: (b, i, h)),
        out_shape=jax.ShapeDtypeStruct((B, S, H * hw), BF16),
        scratch_shapes=([pltpu.VMEM((2, hw, T), BF16)]
                        + [pltpu.VMEM((T, T), F32)] * 4 + [pltpu.VMEM((T, T), BF16)] * 4
                        + [pltpu.VMEM((hw, T), F32)] * 2
                        + [pltpu.VMEM((4, 1, T), F32), pltpu.VMEM((4, 1, T), F32),
                           pltpu.VMEM((2, 1, T), F32), pltpu.VMEM((2, 1, T), F32)]),
        compiler_params=_params("parallel", "parallel", "parallel"),
        name="diff_attn",
    )(far_c, qk, qk, vt, band, vec(lq1), vec(lk1), vec(lq2), vec(lk2),
      gain.reshape(hw, 1).astype(F32))


def _pool_matrices(TM):
    r = np.arange(TM)[:, None]
    c = np.arange(TM)[None, :]
    hpos = np.concatenate([np.arange(-SUBLANES, 0), np.arange(TM, TM + SUBLANES),
                           np.full(LANES - 2 * SUBLANES, 10 ** 6)])[None, :]
    main, halo = [], []
    for w in POOL_WINDOWS:
        lo, hi = r - w // 2, r + w - w // 2
        main.append((c >= lo) & (c < hi))
        halo.append((hpos >= lo) & (hpos < hi))
    return (jnp.asarray(np.stack(main), BF16), jnp.asarray(np.stack(halo), BF16))


def _merge_kernel(x_ref, u_ref, up_ref, un_ref, o_ref, wg_ref, bg_ref, am_ref, ah_ref, wgrp_ref,
                  ps_ref, wbp_ref, wba_ref, wo_ref, lng_ref, lnb_ref, h_ref, ht_ref, *, TM, S, alpha):
    s_idx = pl.program_id(1)
    ns = pl.num_programs(1)
    x = x_ref[...]
    D = x.shape[1]
    xb = x.astype(BF16)
    gates = _sigmoid(_dot(xb, wg_ref[...]) + bg_ref[...])

    u = u_ref[...]
    P = u.shape[1]
    prev = jnp.where(s_idx > 0, up_ref[...], 0.0)
    nxt = jnp.where(s_idx < ns - 1, un_ref[...], 0.0)
    halo = jnp.concatenate([prev, nxt, jnp.zeros((LANES - 2 * SUBLANES, P), F32)], axis=0)
    u_hi, u_lo = _split_bf16(u)
    h_hi, h_lo = _split_bf16(halo)
    pos = s_idx * TM + lax.broadcasted_iota(jnp.int32, (TM, 1), 0)
    G = P // len(POOL_WINDOWS)
    mixed = []
    for g, w in enumerate(POOL_WINDOWS):
        sl = slice(g * G, (g + 1) * G)
        win2 = (_dot(am_ref[g], jnp.concatenate([u_hi[:, sl], u_lo[:, sl]], axis=1))
                + _dot(ah_ref[g], jnp.concatenate([h_hi[:, sl], h_lo[:, sl]], axis=1)))
        win = win2[:, :G] + win2[:, G:]
        cnt = (jnp.minimum(pos + (w - w // 2), S) - jnp.maximum(pos - w // 2, 0)).astype(F32)
        pooled = win / cnt - u[:, sl]
        mixed.append(_dot(pooled.astype(BF16), wgrp_ref[g]))
    mix = jnp.concatenate(mixed, axis=1) * ps_ref[...]
    y_pool = _dot(mix.astype(BF16), wbp_ref[...])
    y_attn = _dot(o_ref[...], wba_ref[...])
    merged = gates[:, :D] * y_pool + gates[:, D:] * y_attn
    y = _dot(merged.astype(BF16), wo_ref[...])
    h = _layer_norm(alpha * x + y, lng_ref[...], lnb_ref[...])
    h_ref[...] = h
    _to_row_tiles(ht_ref, h)


def _merge(x, u, o, w_gate, b_gate, w_grp, pool_scale, w_bp, w_ba, w_out, ln_g, ln_b, alpha):
    B, S, D = x.shape
    P = u.shape[2]
    A = o.shape[2]
    TM = MERGE_TILE
    ns = S // TM
    hb = TM // SUBLANES
    am, ah = _pool_matrices(TM)
    nw = len(POOL_WINDOWS)
    G = P // nw
    row = lambda a: a.reshape(1, -1).astype(F32)
    kern = functools.partial(_merge_kernel, TM=TM, S=S, alpha=alpha)
    return pl.pallas_call(
        kern,
        grid=(B, ns),
        in_specs=[pl.BlockSpec((None, TM, D), lambda b, s: (b, s, 0)),
                  pl.BlockSpec((None, TM, P), lambda b, s: (b, s, 0)),
                  pl.BlockSpec((None, SUBLANES, P),
                               lambda b, s: (b, jnp.maximum(s * hb - 1, 0), 0)),
                  pl.BlockSpec((None, SUBLANES, P),
                               lambda b, s: (b, jnp.minimum((s + 1) * hb, S // SUBLANES - 1), 0)),
                  pl.BlockSpec((None, TM, A), lambda b, s: (b, s, 0)),
                  _const_spec((D, 2 * D)), _const_spec((1, 2 * D)),
                  _const_spec((nw, TM, TM)), _const_spec((nw, TM, LANES)),
                  _const_spec((nw, G, G)), _const_spec((1, P)),
                  _const_spec((P, D)), _const_spec((A, D)), _const_spec((D, D)),
                  _const_spec((1, D)), _const_spec((1, D))],
        out_specs=[pl.BlockSpec((None, TM, D), lambda b, s: (b, s, 0)),
                   pl.BlockSpec((None, TM * SUBLANES, LANES), lambda b, s: (b, s, 0))],
        out_shape=[jax.ShapeDtypeStruct((B, S, D), F32),
                   jax.ShapeDtypeStruct((B, S * SUBLANES, LANES), F32)],
        compiler_params=_params("parallel", "parallel"),
        name="merge",
    )(x, u, u, u, o, w_gate, row(b_gate), am, ah, w_grp, row(pool_scale), w_bp, w_ba, w_out,
      row(ln_g), row(ln_b))


def _route_kernel(h_ref, wrh_ref, wrl_ref, rb_ref, tri_ref, ones_ref, wsg_ref, wsu_ref, wsd_ref,
                  idx_ref, wt_ref, rank_ref, cnt_ref, pre_ref, cnt_acc, *, TM, E, alpha):
    t = pl.program_id(0)

    @pl.when(t == 0)
    def _():
        cnt_acc[...] = jnp.zeros(cnt_acc.shape, F32)

    h = h_ref[...]
    hb, hl = _split_bf16(h)
    logits = (lax.dot_general(wrh_ref[...], hb, _NT, preferred_element_type=F32)
              + lax.dot_general(wrh_ref[...], hl, _NT, preferred_element_type=F32)
              + lax.dot_general(wrl_ref[...], hb, _NT, preferred_element_type=F32))
    s = _sigmoid(logits)
    biased = s + rb_ref[...]
    gsz = E // N_EXPERT_GROUPS
    sub = lax.broadcasted_iota(jnp.int32, (gsz, TM), 0).astype(F32)
    neg = -jnp.inf
    grp, gscore = [], []
    for g in range(N_EXPERT_GROUPS):
        bg = biased[g * gsz:(g + 1) * gsz, :]
        m1 = jnp.max(bg, axis=0, keepdims=True)
        first = jnp.min(jnp.where(bg == m1, sub, float(gsz)), axis=0, keepdims=True)
        m2 = jnp.max(jnp.where(sub == first, neg, bg), axis=0, keepdims=True)
        grp.append(bg)
        gscore.append(m1 + m2)
    masked = []
    for g in range(N_EXPERT_GROUPS):
        beaten = jnp.zeros((1, TM), F32)
        for g2 in range(N_EXPERT_GROUPS):
            if g2 == g:
                continue
            wins = (gscore[g2] > gscore[g]) if g2 > g else (gscore[g2] >= gscore[g])
            beaten = beaten + wins.astype(F32)
        masked.append(jnp.where(beaten < float(TOPK_GROUPS), grp[g], neg))
    masked = jnp.concatenate(masked, axis=0)

    row = lax.broadcasted_iota(jnp.int32, (E, TM), 0).astype(F32)
    sel = jnp.zeros((E, TM), F32)
    firsts, ws = [], []
    for _ in range(TOP_K):
        mx = jnp.max(masked, axis=0, keepdims=True)
        first = jnp.min(jnp.where(masked == mx, row, float(E)), axis=0, keepdims=True)
        oh = row == first
        ws.append(jnp.sum(jnp.where(oh, s, 0.0), axis=0, keepdims=True))
        masked = jnp.where(oh, neg, masked)
        sel = jnp.where(oh, 1.0, sel)
        firsts.append(first)
    wsum = ws[0]
    for w in ws[1:]:
        wsum = wsum + w

    sel_b = sel.astype(BF16)
    base = jnp.concatenate([cnt_acc[...]] * (TM // LANES), axis=1)
    rank_all = _dot(sel_b, tri_ref[...]) + base
    cnt_acc[...] = cnt_acc[...] + _dot(sel_b, ones_ref[...])
    cnt_ref[...] = cnt_acc[...]
    for k in range(TOP_K):
        oh = row == firsts[k]
        idx_k = firsts[k].astype(jnp.int32)
        wt_k = ws[k] / wsum * ROUTED_SCALE
        rank_k = jnp.sum(jnp.where(oh, rank_all, 0.0), axis=0, keepdims=True).astype(jnp.int32)
        for c in range(TM // LANES):
            sl = slice(c * LANES, (c + 1) * LANES)
            idx_ref[c, k:k + 1, :] = idx_k[:, sl]
            wt_ref[c, k:k + 1, :] = wt_k[:, sl]
            rank_ref[c, k:k + 1, :] = rank_k[:, sl]

    g_act = _dot(hb, wsg_ref[...])
    hid = g_act * _sigmoid(g_act) * _dot(hb, wsu_ref[...])
    pre_ref[...] = alpha * h + _dot(hid.astype(BF16), wsd_ref[...])


def _route(h, wr_t_hi, wr_t_lo, router_bias, w_sg, w_su, w_sd, alpha):
    N, D = h.shape
    E = wr_t_hi.shape[0]
    TM = TOKEN_TILE
    Hs = w_sg.shape[1]
    tri = jnp.asarray(np.triu(np.ones((TM, TM), np.float32), k=1), BF16)
    ones = jnp.ones((TM, LANES), BF16)
    kern = functools.partial(_route_kernel, TM=TM, E=E, alpha=alpha)
    kt = pl.BlockSpec((TM // LANES, TOP_K, LANES), lambda t: (t, 0, 0))
    return pl.pallas_call(
        kern,
        grid=(N // TM,),
        in_specs=[pl.BlockSpec((TM, D), lambda t: (t, 0)),
                  _const_spec((E, D)), _const_spec((E, D)), _const_spec((E, 1)),
                  _const_spec((TM, TM)), _const_spec((TM, LANES)),
                  _const_spec((D, Hs)), _const_spec((D, Hs)), _const_spec((Hs, D))],
        out_specs=[kt, kt, kt, _const_spec((E, LANES)), pl.BlockSpec((TM, D), lambda t: (t, 0))],
        out_shape=[jax.ShapeDtypeStruct((N // LANES, TOP_K, LANES), jnp.int32),
                   jax.ShapeDtypeStruct((N // LANES, TOP_K, LANES), F32),
                   jax.ShapeDtypeStruct((N // LANES, TOP_K, LANES), jnp.int32),
                   jax.ShapeDtypeStruct((E, LANES), F32),
                   jax.ShapeDtypeStruct((N, D), F32)],
        scratch_shapes=[pltpu.VMEM((E, LANES), F32)],
        compiler_params=_params("arbitrary"),
        name="route_shared",
    )(h, wr_t_hi, wr_t_lo, router_bias.reshape(E, 1).astype(F32), tri, ones, w_sg, w_su, w_sd)


def _slots_kernel(idx_ref, rank_ref, rowstart_ref, dest_ref, *, E):
    expert = lax.broadcasted_iota(jnp.int32, (E, LANES), 0)
    rowstart = rowstart_ref[...]
    for c in range(idx_ref.shape[0]):
        idx = idx_ref[c]
        rank = rank_ref[c]
        for k in range(TOP_K):
            base = jnp.sum(jnp.where(expert == idx[k:k + 1, :], rowstart, 0.0), axis=0,
                           keepdims=True)
            dest_ref[c, k:k + 1, :] = base.astype(jnp.int32) + rank[k:k + 1, :]


def _slots(idx, rank, rowstart):
    nb = idx.shape[0]
    E = rowstart.shape[0]
    step = SUBLANES
    spec = pl.BlockSpec((step, TOP_K, LANES), lambda t: (t, 0, 0))
    return pl.pallas_call(
        functools.partial(_slots_kernel, E=E),
        grid=(nb // step,),
        in_specs=[spec, spec, _const_spec((E, 1))],
        out_specs=spec,
        out_shape=jax.ShapeDtypeStruct(idx.shape, jnp.int32),
        compiler_params=_params("parallel"),
        name="slots",
    )(idx, rank, rowstart.astype(F32).reshape(E, 1))


def _dispatch_kernel(rowstart_ref, cnt_ref, nact_ref, dest_ref, h_ref, xs_hbm,
                     stage, zeros, sem, zsem, *, TM, BLK, NB, E):
    t = pl.program_id(0)
    nt = pl.num_programs(0)
    slot = t % 2
    R = SUBLANES

    def zero_copy(first, n, start):
        cp = pltpu.make_async_copy(zeros.at[pl.ds(0, n * R)],
                                   xs_hbm.at[pl.ds(pl.multiple_of(first * R, R), n * R)], zsem)
        cp.start() if start else cp.wait()

    def wait_slot(s):
        for _ in range(TOP_K):
            pltpu.make_async_copy(stage.at[s], xs_hbm.at[pl.ds(0, TM * R)], sem.at[s]).wait()

    @pl.when(t == 0)
    def _():
        zeros[...] = jnp.zeros(zeros.shape, F32)
        nact = nact_ref[0]

        def tail(e, c, *, start):
            first = rowstart_ref[e] + cnt_ref[e]
            n_pad = (cnt_ref[e] + BLK - 1) // BLK * BLK - cnt_ref[e]
            done = jnp.int32(0)
            size = BLK // 2
            while size >= 1:
                piece = n_pad & size

                @pl.when(piece != 0)
                def _(size=size, done=done):
                    zero_copy(first + done, size, start)

                done = done + piece
                size //= 2
            return c

        def idle(b, c, *, start):
            zero_copy(b * BLK, BLK, start)
            return c

        for start in (True, False):
            lax.fori_loop(0, E, functools.partial(tail, start=start), 0)
            lax.fori_loop(nact, NB, functools.partial(idle, start=start), 0)

    @pl.when(t >= 2)
    def _():
        wait_slot(slot)

    stage[slot] = h_ref[...]
    for c in range(TM // LANES):
        def row_body(r, carry, c=c):
            src = pl.multiple_of((c * LANES + r) * R, R)
            for k in range(TOP_K):
                dst = pl.multiple_of(dest_ref[c, k, r] * R, R)
                pltpu.make_async_copy(stage.at[slot, pl.ds(src, R)], xs_hbm.at[pl.ds(dst, R)],
                                      sem.at[slot]).start(priority=k % 2)
            return carry

        lax.fori_loop(0, LANES, row_body, 0)

    @pl.when(t == nt - 1)
    def _():
        @pl.when(t >= 1)
        def _():
            wait_slot(1 - slot)

        wait_slot(slot)


def _dispatch(h_tiles, dest, rowstart, counts, nact, NB):
    N = h_tiles.shape[0] // SUBLANES
    E = rowstart.shape[0]
    TM = MERGE_TILE
    BLK = EXPERT_BLOCK
    R = SUBLANES
    kern = functools.partial(_dispatch_kernel, TM=TM, BLK=BLK, NB=NB, E=E)
    grid_spec = pltpu.PrefetchScalarGridSpec(
        num_scalar_prefetch=3,
        grid=(N // TM,),
        in_specs=[pl.BlockSpec((TM // LANES, TOP_K, LANES), lambda t, *_: (t, 0, 0),
                               memory_space=pltpu.SMEM),
                  pl.BlockSpec((TM * R, LANES), lambda t, *_: (t, 0))],
        out_specs=pl.BlockSpec(memory_space=pl.ANY),
        scratch_shapes=[pltpu.VMEM((2, TM * R, LANES), F32), pltpu.VMEM((BLK * R, LANES), F32),
                        pltpu.SemaphoreType.DMA((2,)), pltpu.SemaphoreType.DMA(())],
    )
    return pl.pallas_call(
        kern,
        grid_spec=grid_spec,
        out_shape=jax.ShapeDtypeStruct((NB * BLK * R, LANES), F32),
        compiler_params=_params("arbitrary"),
        name="dispatch",
    )(rowstart, counts, nact, dest, h_tiles)


def _expert_kernel(first_ref, nblk_ref, nact_ref, wg_ref, wu_ref, wd_ref, x_hbm, y_hbm,
                   xbuf, ybuf, wg_s, wu_s, wd_s, xsem, ysem, *, BLK, NB):
    e = pl.program_id(0)
    nact = nact_ref[0]
    rows = BLK * SUBLANES

    def block(ref, g):
        return ref.at[pl.ds(pl.multiple_of(g * rows, rows), rows)]

    def x_copy(g, slot):
        return pltpu.make_async_copy(block(x_hbm, g), xbuf.at[slot], xsem.at[slot])

    def y_copy(g, slot):
        return pltpu.make_async_copy(ybuf.at[slot], block(y_hbm, g), ysem.at[slot])

    ahead = SLOTS - 1

    @pl.when(e == 0)
    def _():
        for g0 in range(ahead):
            @pl.when(g0 < nact)
            def _(g0=g0):
                x_copy(g0, g0).start()

    @pl.when(nblk_ref[e] > 0)
    def _():
        wg_s[...] = wg_ref[...].astype(BF16)
        wu_s[...] = wu_ref[...].astype(BF16)
        wd_s[...] = wd_ref[...].astype(BF16)

    def body(j, carry):
        g = first_ref[e] + j
        slot = g % SLOTS
        x_copy(g, slot).wait()

        @pl.when(g + ahead < nact)
        def _():
            x_copy(g + ahead, (g + ahead) % SLOTS).start()

        @pl.when(g >= SLOTS)
        def _():
            y_copy(g - SLOTS, slot).wait()

        x = _from_row_tiles(xbuf.at[slot], 0, BLK).astype(BF16)
        g_act = _dot(x, wg_s[...])
        hid = g_act * _sigmoid(g_act) * _dot(x, wu_s[...])
        _to_row_tiles(ybuf.at[slot], _dot(hid.astype(BF16), wd_s[...]))
        y_copy(g, slot).start()
        return carry

    lax.fori_loop(0, nblk_ref[e], body, 0)

    @pl.when(e == pl.num_programs(0) - 1)
    def _():
        for back in range(SLOTS, 0, -1):
            @pl.when(nact >= back)
            def _(back=back):
                y_copy(nact - back, (nact - back) % SLOTS).wait()

        ybuf[0] = jnp.zeros(ybuf.shape[1:], F32)

        def idle(g, carry, *, start):
            cp = y_copy(g, 0)
            cp.start() if start else cp.wait()
            return carry

        for start in (True, False):
            lax.fori_loop(nact, NB, functools.partial(idle, start=start), 0)


def _experts(x_sorted, first_blk, nblk_e, nact, NB, w_eg, w_eu, w_ed):
    E, D, He = w_eg.shape
    BLK = EXPERT_BLOCK
    R = SUBLANES
    per_expert = lambda shape: pl.BlockSpec((None,) + shape, lambda e, *_: (e, 0, 0))
    grid_spec = pltpu.PrefetchScalarGridSpec(
        num_scalar_prefetch=3,
        grid=(E,),
        in_specs=[per_expert((D, He)), per_expert((D, He)), per_expert((He, D)),
                  pl.BlockSpec(memory_space=pl.ANY)],
        out_specs=pl.BlockSpec(memory_space=pl.ANY),
        scratch_shapes=[pltpu.VMEM((SLOTS, BLK * R, LANES), F32),
                        pltpu.VMEM((SLOTS, BLK * R, LANES), F32),
                        pltpu.VMEM((D, He), BF16), pltpu.VMEM((D, He), BF16),
                        pltpu.VMEM((He, D), BF16),
                        pltpu.SemaphoreType.DMA((SLOTS,)), pltpu.SemaphoreType.DMA((SLOTS,))],
    )
    return pl.pallas_call(
        functools.partial(_expert_kernel, BLK=BLK, NB=NB),
        grid_spec=grid_spec,
        out_shape=jax.ShapeDtypeStruct((NB * BLK * R, LANES), F32),
        compiler_params=_params("arbitrary"),
        name="routed_experts",
    )(first_blk, nblk_e, nact, w_eg, w_eu, w_ed, x_sorted)


def _combine_kernel(dest_ref, wt_ref, pre_ref, lng_ref, lnb_ref, y_hbm, o_ref, buf, sem, *, TM):
    t = pl.program_id(0)
    nt = pl.num_programs(0) - 1
    slot = t % 2
    R = SUBLANES

    @pl.when(t < nt)
    def _():
        for k in range(TOP_K):
            for r in range(TM):
                src = pl.multiple_of(dest_ref[0, k, r] * R, R)
                pltpu.make_async_copy(y_hbm.at[pl.ds(src, R)],
                                      buf.at[slot, pl.ds((k * TM + r) * R, R)],
                                      sem.at[slot]).start(priority=r % 2)

    @pl.when(t >= 1)
    def _():
        prev = 1 - slot
        for k in range(TOP_K):
            pltpu.make_async_copy(y_hbm.at[pl.ds(0, TM * R)],
                                  buf.at[prev, pl.ds(k * TM * R, TM * R)], sem.at[prev]).wait()
        w = jnp.concatenate([wt_ref[0], jnp.zeros((TM - TOP_K, TM), F32)], axis=0).T
        z = pre_ref[...]
        rows = buf.at[prev]
        for k in range(TOP_K):
            z = z + w[:, k:k + 1] * _from_row_tiles(rows, k * TM * R, TM)
        o_ref[...] = _layer_norm(z, lng_ref[...], lnb_ref[...])


def _combine(y_sorted, dest, wts, pre, ln_g, ln_b):
    N, D = pre.shape
    TM = LANES
    nt = N // TM
    R = SUBLANES
    kern = functools.partial(_combine_kernel, TM=TM)
    row = lambda a: a.reshape(1, -1).astype(F32)
    ahead = lambda t: (jnp.minimum(t, nt - 1), 0, 0)
    behind3 = lambda t: (jnp.maximum(t - 1, 0), 0, 0)
    behind = lambda t: (jnp.maximum(t - 1, 0), 0)
    return pl.pallas_call(
        kern,
        grid=(nt + 1,),
        in_specs=[pl.BlockSpec((1, TOP_K, TM), ahead, memory_space=pltpu.SMEM),
                  pl.BlockSpec((1, TOP_K, TM), behind3),
                  pl.BlockSpec((TM, D), behind),
                  _const_spec((1, D)), _const_spec((1, D)),
                  pl.BlockSpec(memory_space=pl.ANY)],
        out_specs=pl.BlockSpec((TM, D), behind),
        out_shape=jax.ShapeDtypeStruct((N, D), F32),
        scratch_shapes=[pltpu.VMEM((2, TOP_K * TM * R, LANES), F32),
                        pltpu.SemaphoreType.DMA((2,))],
        compiler_params=_params("arbitrary"),
        name="combine",
    )(dest, wts, pre, row(ln_g), row(ln_b), y_sorted)


def _dispatch_meta(cnt, N, E):
    BLK = EXPERT_BLOCK
    NB = (N * TOP_K + E * (BLK - 1)) // BLK
    counts = cnt[:, 0].astype(jnp.int32)
    nblk_e = (counts + BLK - 1) // BLK
    bend = jnp.cumsum(nblk_e)
    first_blk = (bend - nblk_e).astype(jnp.int32)
    rowstart = first_blk * BLK
    nact = bend[-1:].astype(jnp.int32)
    return counts, rowstart, first_blk, nblk_e.astype(jnp.int32), nact, NB


def kernel(x, rel_bias_table, w_in, b_gate, w_pool_grp, pool_scale, w_branch_pool, lambda_q1,
           lambda_k1, lambda_q2, lambda_k2, subln_gain, w_branch_attn, w_out, ln1_g, ln1_b,
           w_router, router_bias, w_exp_gate, w_exp_up, w_exp_down, w_sh_gate, w_sh_up,
           w_sh_down, ln2_g, ln2_b):
    B, S, D = x.shape
    L = w_in.shape[0]
    E = w_router.shape[2]
    P = pool_scale.shape[1]
    A = w_branch_attn.shape[1]
    N = B * S
    alpha = (2 * L) ** 0.25
    assert S % TOKEN_TILE == 0 and S % MERGE_TILE == 0
    assert A == N_DIFF_HEADS * 2 * DIFF_HEAD_DIM and TOKEN_TILE >= REL_MAX_DIST
    assert D == SUBLANES * LANES, "a row tile holds exactly one (SUBLANES, LANES) tile per row"
    band, far_c = _bias_tables(rel_bias_table, S, TOKEN_TILE)

    h = x
    for i in range(L):
        w_uqkv = w_in[i][:, :P + 3 * A].astype(BF16)
        w_gate = w_in[i][:, P + 3 * A:].astype(BF16)
        u, qk, vt = _inproj(h, w_uqkv, P, A)
        o = _attention(qk, vt, band, far_c, lambda_q1[i], lambda_k1[i], lambda_q2[i],
                       lambda_k2[i], subln_gain[i], _lambda_init(i))
        h1, h1_tiles = _merge(h, u, o, w_gate, b_gate[i], w_pool_grp[i].astype(BF16), pool_scale[i],
                    w_branch_pool[i].astype(BF16), w_branch_attn[i].astype(BF16),
                    w_out[i].astype(BF16), ln1_g[i], ln1_b[i], alpha)
        t = h1.reshape(N, D)
        wr_hi, wr_lo = _split_bf16(w_router[i].astype(F32).T)
        idx, wts, rank, cnt, pre = _route(t, wr_hi, wr_lo, router_bias[i],
                                          w_sh_gate[i].astype(BF16), w_sh_up[i].astype(BF16),
                                          w_sh_down[i].astype(BF16), alpha)
        counts, rowstart, first_blk, nblk_e, nact, NB = _dispatch_meta(cnt, N, E)
        dest = _slots(idx, rank, rowstart)
        x_sorted = _dispatch(h1_tiles.reshape(N * SUBLANES, LANES), dest, rowstart, counts, nact, NB)
        y_sorted = _experts(x_sorted, first_blk, nblk_e, nact, NB, w_exp_gate[i], w_exp_up[i],
                            w_exp_down[i])
        h = _combine(y_sorted, dest, wts, pre, ln2_g[i], ln2_b[i]).reshape(B, S, D)
    return h
```

```python
import functools
import math

import numpy as np
import jax
import jax.numpy as jnp
from jax import lax
from jax.experimental import pallas as pl
from jax.experimental.pallas import tpu as pltpu

F32 = jnp.float32
BF16 = jnp.bfloat16

POOL_WINDOWS = (2, 4, 8, 16)
N_DIFF_HEADS = 4
DIFF_HEAD_DIM = 64
REL_BUCKETS = 32
REL_MAX_DIST = 128
TOP_K = 8
N_EXPERT_GROUPS = 8
TOPK_GROUPS = 4
ROUTED_SCALE = 2.5
LN_EPS = 1e-5
LOG2E = math.log2(math.e)

LANES = 128
SUBLANES = 8
VMEM_LIMIT_BYTES = 56 * 1024 * 1024

TOKEN_TILE = 512
MERGE_TILE = 256
EXPERT_BLOCK = 256
SLOTS = 8

_NT = (((1,), (1,)), ((), ()))


def _lambda_init(layer_idx):
    return 0.8 - 0.6 * math.exp(-0.3 * layer_idx)


def _dot(a, b):
    return jnp.dot(a, b, preferred_element_type=F32)


def _split_bf16(a):
    hi = a.astype(BF16)
    lo = (a - hi.astype(F32)).astype(BF16)
    return hi, lo


def _sigmoid(z):
    return 1.0 / (1.0 + jnp.exp(-z))


def _layer_norm(z, g, b):
    mu = jnp.mean(z, axis=-1, keepdims=True)
    zc = z - mu
    var = jnp.mean(zc * zc, axis=-1, keepdims=True)
    return zc * lax.rsqrt(var + LN_EPS) * g + b


def _params(*sem):
    return pltpu.CompilerParams(dimension_semantics=sem, vmem_limit_bytes=VMEM_LIMIT_BYTES)


def _const_spec(shape):
    nd = len(shape)
    return pl.BlockSpec(shape, lambda *_: (0,) * nd)


def _to_row_tiles(ref, x):
    rows = x.shape[0]
    for s in range(SUBLANES):
        ref[pl.ds(s, rows, stride=SUBLANES), :] = x[:, s * LANES:(s + 1) * LANES]


def _from_row_tiles(ref, start, rows):
    return jnp.concatenate([ref[pl.ds(start + s, rows, stride=SUBLANES), :]
                            for s in range(SUBLANES)], axis=1)


def _inproj_kernel(x_ref, w_ref, u_ref, qk_ref, vt_ref, *, pool_w, attn_w, q_scale):
    xb = x_ref[...].astype(BF16)
    p = _dot(xb, w_ref[...])
    u_ref[...] = p[:, :pool_w]
    q = p[:, pool_w:pool_w + attn_w] * q_scale
    k = p[:, pool_w + attn_w:pool_w + 2 * attn_w]
    qk_ref[:, :attn_w] = q.astype(BF16)
    qk_ref[:, attn_w:] = k.astype(BF16)
    v = p[:, pool_w + 2 * attn_w:pool_w + 3 * attn_w]
    vt_ref[...] = v.T.astype(BF16)


def _inproj(x, w_uqkv, pool_w, attn_w):
    B, S, D = x.shape
    T = TOKEN_TILE
    ns = S // T
    width = w_uqkv.shape[1]
    kern = functools.partial(_inproj_kernel, pool_w=pool_w, attn_w=attn_w,
                             q_scale=DIFF_HEAD_DIM ** -0.5 * LOG2E)
    return pl.pallas_call(
        kern,
        grid=(B, ns),
        in_specs=[pl.BlockSpec((None, T, D), lambda b, s: (b, s, 0)),
                  _const_spec((D, width))],
        out_specs=[pl.BlockSpec((None, T, pool_w), lambda b, s: (b, s, 0)),
                   pl.BlockSpec((None, T, 2 * attn_w), lambda b, s: (b, s, 0)),
                   pl.BlockSpec((None, None, attn_w, T), lambda b, s: (b, s, 0, 0))],
        out_shape=[jax.ShapeDtypeStruct((B, S, pool_w), F32),
                   jax.ShapeDtypeStruct((B, S, 2 * attn_w), BF16),
                   jax.ShapeDtypeStruct((B, ns, attn_w, T), BF16)],
        compiler_params=_params("parallel", "parallel"),
        name="inproj",
    )(x, w_uqkv)


def _t5_bucket_np(rel):
    half = REL_BUCKETS // 2
    max_exact = half // 2
    ret = np.where(rel > 0, half, 0)
    n = np.abs(rel)
    nf = np.maximum(n, 1).astype(np.float32)
    large = max_exact + (np.log(nf / max_exact) / math.log(REL_MAX_DIST / max_exact)
                         * (half - max_exact)).astype(np.int32)
    large = np.minimum(large, half - 1)
    return ret + np.where(n < max_exact, n, large)


def _t5_bucket(rel):
    half = REL_BUCKETS // 2
    max_exact = half // 2
    ret = jnp.where(rel > 0, half, 0)
    n = jnp.abs(rel)
    nf = jnp.maximum(n, 1).astype(F32)
    large = max_exact + (jnp.log(nf / max_exact) / math.log(REL_MAX_DIST / max_exact)
                         * (half - max_exact)).astype(jnp.int32)
    large = jnp.minimum(large, half - 1)
    return ret + jnp.where(n < max_exact, n, large)


def _bias_tables(table, S, T):
    far = np.arange(T + 1, S)
    if far.size:
        assert np.all(_t5_bucket_np(-far) == _t5_bucket_np(-far[-1]))
        assert np.all(_t5_bucket_np(far) == _t5_bucket_np(far[-1]))
    rel = jnp.arange(-(S - 1), S, dtype=jnp.int32)
    onehot = _t5_bucket(rel)[None, :, None] == jnp.arange(REL_BUCKETS)[None, None, :]
    dist_bias = jnp.sum(jnp.where(onehot, table.astype(F32).T[:, None, :] * LOG2E, 0.0),
                        axis=-1)
    pad = jnp.pad(dist_bias, ((0, 0), (2 * T, 2 * T)), mode='edge')
    centre = [(d - 1) * T + S - 1 + 2 * T for d in range(3)]
    w = jnp.stack([jnp.concatenate(
        [lax.slice_in_dim(pad, c0 - T + 1, c0 + 1, axis=1)[:, ::-1],
         lax.slice_in_dim(pad, c0 + 1, c0 + T + 1, axis=1)[:, ::-1]], axis=1)
        for c0 in centre], axis=1)
    H = w.shape[0]
    band = pl.pallas_call(
        functools.partial(_band_kernel, T=T),
        grid=(H, 3),
        in_specs=[pl.BlockSpec((None, None, 1, 2 * T), lambda h, d: (h, d, 0, 0))],
        out_specs=pl.BlockSpec((None, None, T, T), lambda h, d: (h, d, 0, 0)),
        out_shape=jax.ShapeDtypeStruct((H, 3, T, T), F32),
        compiler_params=_params("parallel", "parallel"),
        name="band_tiles",
    )(w.reshape(H, 3, 1, 2 * T))
    far_c = jnp.stack([dist_bias[:, 0], dist_bias[:, 2 * S - 2]], axis=1)
    return band, far_c


def _band_kernel(w_ref, o_ref, *, T):
    x = jnp.broadcast_to(w_ref[...], (T, 2 * T))
    o_ref[...] = pltpu.roll(x, 0, 1, stride=1, stride_axis=0)[:, :T]


def _attn_kernel(far_ref, q_ref, k_ref, vt_ref, band_ref, lq1_ref, lk1_ref, lq2_ref, lk2_ref,
                 gain_ref, o_ref, qz_ref, s00, s01, s10, s11, e00, e01, e10, e11, acc0, acc1,
                 mt_ref, al_ref, m_ref, l_ref, *, T, nk, lam_init):
    h = pl.program_id(1)
    i = pl.program_id(2)
    s_buf = ((s00, s01), (s10, s11))
    e_buf = ((e00, e01), (e10, e11))
    acc = (acc0, acc1)
    c_left = far_ref[h, 0]
    c_right = far_ref[h, 1]

    qt = q_ref[...].astype(F32).T
    half = lax.broadcasted_iota(jnp.int32, qt.shape, 0) < DIFF_HEAD_DIM
    qz_ref[0] = jnp.where(half, qt, 0.0).astype(BF16)
    qz_ref[1] = jnp.where(half, 0.0, qt).astype(BF16)
    m_ref[...] = jnp.full(m_ref.shape, -jnp.inf, F32)
    l_ref[...] = jnp.zeros(l_ref.shape, F32)
    acc0[...] = jnp.zeros(acc0.shape, F32)
    acc1[...] = jnp.zeros(acc1.shape, F32)

    has_prev = i >= 1
    has_next = i <= nk - 2
    n_left = jnp.maximum(i - 1, 0) - jnp.where(has_next, 0, 1)
    right0 = i + 2 + jnp.where(has_prev, 0, 1)

    def tile_of(p):
        if isinstance(p, int) and p == 0:
            return i, band_ref[1], None
        if isinstance(p, int) and p == 1:
            return (jnp.where(has_prev, i - 1, i + 2),
                    jnp.where(has_prev, band_ref[0], c_right), None)
        if isinstance(p, int) and p == 2:
            return (jnp.where(has_next, i + 1, i - 2),
                    jnp.where(has_next, band_ref[2], c_left), None)
        f = p - 3
        is_left = f < n_left
        return (jnp.where(is_left, f, f - n_left + right0), None,
                jnp.where(is_left, c_left, c_right))

    def stage_a(p, x):
        j, bias, const = tile_of(p)
        kt = k_ref[pl.ds(pl.multiple_of(j * T, T), T), :]
        for c in range(2):
            s = _dot(kt, qz_ref[c])
            if bias is not None:
                s = s + bias
            s_buf[x][c][...] = s
            mt = jnp.max(s, axis=0, keepdims=True)
            mt_ref[2 * x + c] = mt if const is None else mt + const

    def stage_b(p, x):
        _, _, const = tile_of(p)
        for c in range(2):
            m_old = m_ref[c]
            m_new = jnp.maximum(m_old, mt_ref[2 * x + c])
            alpha = jnp.exp2(m_old - m_new)
            shift = m_new if const is None else m_new - const
            e = jnp.exp2(s_buf[x][c][...] - shift)
            l_ref[c] = alpha * l_ref[c] + jnp.sum(e, axis=0, keepdims=True)
            e_buf[x][c][...] = e.astype(BF16)
            al_ref[2 * x + c] = alpha
            m_ref[c] = m_new

    def stage_c(p, x):
        j, _, _ = tile_of(p)
        vt = vt_ref[j]
        for c in range(2):
            acc[c][...] = al_ref[2 * x + c] * acc[c][...] + _dot(vt, e_buf[x][c][...])

    def iteration(n, parity):
        static = isinstance(n, int)
        if not static or n < nk:
            stage_a(n, parity)
        if not static or 1 <= n <= nk:
            stage_b(n - 1, 1 - parity)
        if not static or 2 <= n <= nk + 1:
            stage_c(n - 2, parity)

    head = 5
    for n in range(head):
        iteration(n, n % 2)

    def pair(it, carry):
        n = head + 2 * it
        iteration(n, head % 2)
        iteration(n + 1, 1 - head % 2)
        return carry

    n_pairs = (nk - 1 - head) // 2
    lax.fori_loop(0, n_pairs, pair, 0)
    for n in range(head + 2 * n_pairs, nk + 2):
        iteration(n, n % 2)

    lam = (jnp.exp(jnp.sum(lq1_ref[...] * lk1_ref[...], keepdims=True))
           - jnp.exp(jnp.sum(lq2_ref[...] * lk2_ref[...], keepdims=True)) + lam_init)
    o = acc0[...] / l_ref[0] - lam * (acc1[...] / l_ref[1])
    ms = jnp.mean(o * o, axis=0, keepdims=True)
    y = o * lax.rsqrt(ms + LN_EPS) * gain_ref[...] * (1.0 - lam_init)
    o_ref[...] = y.T.astype(BF16)


def _attention(qk, vt, band, far_c, lq1, lk1, lq2, lk2, gain, lam_init):
    B, S, _ = qk.shape
    T = TOKEN_TILE
    nk = S // T
    assert nk >= 6, "the pipeline prologue assumes at least three far key tiles"
    H = N_DIFF_HEADS
    hw = 2 * DIFF_HEAD_DIM
    kern = functools.partial(_attn_kernel, T=T, nk=nk, lam_init=lam_init)
    vec = lambda a: a.reshape(1, DIFF_HEAD_DIM).astype(F32)
    return pl.pallas_call(
        kern,
        grid=(B, H, nk),
        in_specs=[pl.BlockSpec(memory_space=pltpu.SMEM),
                  pl.BlockSpec((None, T, hw), lambda b, h, i: (b, i, h)),
                  pl.BlockSpec((None, S, hw), lambda b, h, i: (b, 0, H + h)),
                  pl.BlockSpec((None, nk, hw, T), lambda b, h, i: (b, 0, h, 0)),
                  pl.BlockSpec((None, 3, T, T), lambda b, h, i: (h, 0, 0, 0)),
                  _const_spec((1, DIFF_HEAD_DIM)), _const_spec((1, DIFF_HEAD_DIM)),
                  _const_spec((1, DIFF_HEAD_DIM)), _const_spec((1, DIFF_HEAD_DIM)),
                  _const_spec((hw, 1))],
        out_specs=pl.BlockSpec((None, T, hw), lambda b, h, i: (b, i, h)),
        out_shape=jax.ShapeDtypeStruct((B, S, H * hw), BF16),
        scratch_shapes=([pltpu.VMEM((2, hw, T), BF16)]
                        + [pltpu.VMEM((T, T), F32)] * 4 + [pltpu.VMEM((T, T), BF16)] * 4
                        + [pltpu.VMEM((hw, T), F32)] * 2
                        + [pltpu.VMEM((4, 1, T), F32), pltpu.VMEM((4, 1, T), F32),
                           pltpu.VMEM((2, 1, T), F32), pltpu.VMEM((2, 1, T), F32)]),
        compiler_params=_params("parallel", "parallel", "parallel"),
        name="diff_attn",
    )(far_c, qk, qk, vt, band, vec(lq1), vec(lk1), vec(lq2), vec(lk2),
      gain.reshape(hw, 1).astype(F32))


def _pool_matrices(TM):
    r = np.arange(TM)[:, None]
    c = np.arange(TM)[None, :]
    hpos = np.concatenate([np.arange(-SUBLANES, 0), np.arange(TM, TM + SUBLANES),
                           np.full(LANES - 2 * SUBLANES, 10 ** 6)])[None, :]
    main, halo = [], []
    for w in POOL_WINDOWS:
        lo, hi = r - w // 2, r + w - w // 2
        main.append((c >= lo) & (c < hi))
        halo.append((hpos >= lo) & (hpos < hi))
    return (jnp.asarray(np.stack(main), BF16), jnp.asarray(np.stack(halo), BF16))


def _merge_kernel(x_ref, u_ref, up_ref, un_ref, o_ref, wg_ref, bg_ref, am_ref, ah_ref, wgrp_ref,
                  ps_ref, wbp_ref, wba_ref, wo_ref, lng_ref, lnb_ref, h_ref, ht_ref, *, TM, S, alpha):
    s_idx = pl.program_id(1)
    ns = pl.num_programs(1)
    x = x_ref[...]
    D = x.shape[1]
    xb = x.astype(BF16)
    gates = _sigmoid(_dot(xb, wg_ref[...]) + bg_ref[...])

    u = u_ref[...]
    P = u.shape[1]
    prev = jnp.where(s_idx > 0, up_ref[...], 0.0)
    nxt = jnp.where(s_idx < ns - 1, un_ref[...], 0.0)
    halo = jnp.concatenate([prev, nxt, jnp.zeros((LANES - 2 * SUBLANES, P), F32)], axis=0)
    u_hi, u_lo = _split_bf16(u)
    h_hi, h_lo = _split_bf16(halo)
    pos = s_idx * TM + lax.broadcasted_iota(jnp.int32, (TM, 1), 0)
    G = P // len(POOL_WINDOWS)
    mixed = []
    for g, w in enumerate(POOL_WINDOWS):
        sl = slice(g * G, (g + 1) * G)
        win2 = (_dot(am_ref[g], jnp.concatenate([u_hi[:, sl], u_lo[:, sl]], axis=1))
                + _dot(ah_ref[g], jnp.concatenate([h_hi[:, sl], h_lo[:, sl]], axis=1)))
        win = win2[:, :G] + win2[:, G:]
        cnt = (jnp.minimum(pos + (w - w // 2), S) - jnp.maximum(pos - w // 2, 0)).astype(F32)
        pooled = win / cnt - u[:, sl]
        mixed.append(_dot(pooled.astype(BF16), wgrp_ref[g]))
    mix = jnp.concatenate(mixed, axis=1) * ps_ref[...]
    y_pool = _dot(mix.astype(BF16), wbp_ref[...])
    y_attn = _dot(o_ref[...], wba_ref[...])
    merged = gates[:, :D] * y_pool + gates[:, D:] * y_attn
    y = _dot(merged.astype(BF16), wo_ref[...])
    h = _layer_norm(alpha * x + y, lng_ref[...], lnb_ref[...])
    h_ref[...] = h
    _to_row_tiles(ht_ref, h)


def _merge(x, u, o, w_gate, b_gate, w_grp, pool_scale, w_bp, w_ba, w_out, ln_g, ln_b, alpha):
    B, S, D = x.shape
    P = u.shape[2]
    A = o.shape[2]
    TM = MERGE_TILE
    ns = S // TM
    hb = TM // SUBLANES
    am, ah = _pool_matrices(TM)
    nw = len(POOL_WINDOWS)
    G = P // nw
    row = lambda a: a.reshape(1, -1).astype(F32)
    kern = functools.partial(_merge_kernel, TM=TM, S=S, alpha=alpha)
    return pl.pallas_call(
        kern,
        grid=(B, ns),
        in_specs=[pl.BlockSpec((None, TM, D), lambda b, s: (b, s, 0)),
                  pl.BlockSpec((None, TM, P), lambda b, s: (b, s, 0)),
                  pl.BlockSpec((None, SUBLANES, P),
                               lambda b, s: (b, jnp.maximum(s * hb - 1, 0), 0)),
                  pl.BlockSpec((None, SUBLANES, P),
                               lambda b, s: (b, jnp.minimum((s + 1) * hb, S // SUBLANES - 1), 0)),
                  pl.BlockSpec((None, TM, A), lambda b, s: (b, s, 0)),
                  _const_spec((D, 2 * D)), _const_spec((1, 2 * D)),
                  _const_spec((nw, TM, TM)), _const_spec((nw, TM, LANES)),
                  _const_spec((nw, G, G)), _const_spec((1, P)),
                  _const_spec((P, D)), _const_spec((A, D)), _const_spec((D, D)),
                  _const_spec((1, D)), _const_spec((1, D))],
        out_specs=[pl.BlockSpec((None, TM, D), lambda b, s: (b, s, 0)),
                   pl.BlockSpec((None, TM * SUBLANES, LANES), lambda b, s: (b, s, 0))],
        out_shape=[jax.ShapeDtypeStruct((B, S, D), F32),
                   jax.ShapeDtypeStruct((B, S * SUBLANES, LANES), F32)],
        compiler_params=_params("parallel", "parallel"),
        name="merge",
    )(x, u, u, u, o, w_gate, row(b_gate), am, ah, w_grp, row(pool_scale), w_bp, w_ba, w_out,
      row(ln_g), row(ln_b))


def _route_kernel(h_ref, wrh_ref, wrl_ref, rb_ref, tri_ref, ones_ref, wsg_ref, wsu_ref, wsd_ref,
                  idx_ref, wt_ref, rank_ref, cnt_ref, pre_ref, cnt_acc, *, TM, E, alpha):
    t = pl.program_id(0)

    @pl.when(t == 0)
    def _():
        cnt_acc[...] = jnp.zeros(cnt_acc.shape, F32)

    h = h_ref[...]
    hb, hl = _split_bf16(h)
    logits = (lax.dot_general(wrh_ref[...], hb, _NT, preferred_element_type=F32)
              + lax.dot_general(wrh_ref[...], hl, _NT, preferred_element_type=F32)
              + lax.dot_general(wrl_ref[...], hb, _NT, preferred_element_type=F32))
    s = _sigmoid(logits)
    biased = s + rb_ref[...]
    gsz = E // N_EXPERT_GROUPS
    sub = lax.broadcasted_iota(jnp.int32, (gsz, TM), 0).astype(F32)
    neg = -jnp.inf
    grp, gscore = [], []
    for g in range(N_EXPERT_GROUPS):
        bg = biased[g * gsz:(g + 1) * gsz, :]
        m1 = jnp.max(bg, axis=0, keepdims=True)
        first = jnp.min(jnp.where(bg == m1, sub, float(gsz)), axis=0, keepdims=True)
        m2 = jnp.max(jnp.where(sub == first, neg, bg), axis=0, keepdims=True)
        grp.append(bg)
        gscore.append(m1 + m2)
    masked = []
    for g in range(N_EXPERT_GROUPS):
        beaten = jnp.zeros((1, TM), F32)
        for g2 in range(N_EXPERT_GROUPS):
            if g2 == g:
                continue
            wins = (gscore[g2] > gscore[g]) if g2 > g else (gscore[g2] >= gscore[g])
            beaten = beaten + wins.astype(F32)
        masked.append(jnp.where(beaten < float(TOPK_GROUPS), grp[g], neg))
    masked = jnp.concatenate(masked, axis=0)

    row = lax.broadcasted_iota(jnp.int32, (E, TM), 0).astype(F32)
    sel = jnp.zeros((E, TM), F32)
    firsts, ws = [], []
    for _ in range(TOP_K):
        mx = jnp.max(masked, axis=0, keepdims=True)
        first = jnp.min(jnp.where(masked == mx, row, float(E)), axis=0, keepdims=True)
        oh = row == first
        ws.append(jnp.sum(jnp.where(oh, s, 0.0), axis=0, keepdims=True))
        masked = jnp.where(oh, neg, masked)
        sel = jnp.where(oh, 1.0, sel)
        firsts.append(first)
    wsum = ws[0]
    for w in ws[1:]:
        wsum = wsum + w

    sel_b = sel.astype(BF16)
    base = jnp.concatenate([cnt_acc[...]] * (TM // LANES), axis=1)
    rank_all = _dot(sel_b, tri_ref[...]) + base
    cnt_acc[...] = cnt_acc[...] + _dot(sel_b, ones_ref[...])
    cnt_ref[...] = cnt_acc[...]
    for k in range(TOP_K):
        oh = row == firsts[k]
        idx_k = firsts[k].astype(jnp.int32)
        wt_k = ws[k] / wsum * ROUTED_SCALE
        rank_k = jnp.sum(jnp.where(oh, rank_all, 0.0), axis=0, keepdims=True).astype(jnp.int32)
        for c in range(TM // LANES):
            sl = slice(c * LANES, (c + 1) * LANES)
            idx_ref[c, k:k + 1, :] = idx_k[:, sl]
            wt_ref[c, k:k + 1, :] = wt_k[:, sl]
            rank_ref[c, k:k + 1, :] = rank_k[:, sl]

    g_act = _dot(hb, wsg_ref[...])
    hid = g_act * _sigmoid(g_act) * _dot(hb, wsu_ref[...])
    pre_ref[...] = alpha * h + _dot(hid.astype(BF16), wsd_ref[...])


def _route(h, wr_t_hi, wr_t_lo, router_bias, w_sg, w_su, w_sd, alpha):
    N, D = h.shape
    E = wr_t_hi.shape[0]
    TM = TOKEN_TILE
    Hs = w_sg.shape[1]
    tri = jnp.asarray(np.triu(np.ones((TM, TM), np.float32), k=1), BF16)
    ones = jnp.ones((TM, LANES), BF16)
    kern = functools.partial(_route_kernel, TM=TM, E=E, alpha=alpha)
    kt = pl.BlockSpec((TM // LANES, TOP_K, LANES), lambda t: (t, 0, 0))
    return pl.pallas_call(
        kern,
        grid=(N // TM,),
        in_specs=[pl.BlockSpec((TM, D), lambda t: (t, 0)),
                  _const_spec((E, D)), _const_spec((E, D)), _const_spec((E, 1)),
                  _const_spec((TM, TM)), _const_spec((TM, LANES)),
                  _const_spec((D, Hs)), _const_spec((D, Hs)), _const_spec((Hs, D))],
        out_specs=[kt, kt, kt, _const_spec((E, LANES)), pl.BlockSpec((TM, D), lambda t: (t, 0))],
        out_shape=[jax.ShapeDtypeStruct((N // LANES, TOP_K, LANES), jnp.int32),
                   jax.ShapeDtypeStruct((N // LANES, TOP_K, LANES), F32),
                   jax.ShapeDtypeStruct((N // LANES, TOP_K, LANES), jnp.int32),
                   jax.ShapeDtypeStruct((E, LANES), F32),
                   jax.ShapeDtypeStruct((N, D), F32)],
        scratch_shapes=[pltpu.VMEM((E, LANES), F32)],
        compiler_params=_params("arbitrary"),
        name="route_shared",
    )(h, wr_t_hi, wr_t_lo, router_bias.reshape(E, 1).astype(F32), tri, ones, w_sg, w_su, w_sd)


def _slots_kernel(idx_ref, rank_ref, rowstart_ref, dest_ref, *, E):
    expert = lax.broadcasted_iota(jnp.int32, (E, LANES), 0)
    rowstart = rowstart_ref[...]
    for c in range(idx_ref.shape[0]):
        idx = idx_ref[c]
        rank = rank_ref[c]
        for k in range(TOP_K):
            base = jnp.sum(jnp.where(expert == idx[k:k + 1, :], rowstart, 0.0), axis=0,
                           keepdims=True)
            dest_ref[c, k:k + 1, :] = base.astype(jnp.int32) + rank[k:k + 1, :]


def _slots(idx, rank, rowstart):
    nb = idx.shape[0]
    E = rowstart.shape[0]
    step = SUBLANES
    spec = pl.BlockSpec((step, TOP_K, LANES), lambda t: (t, 0, 0))
    return pl.pallas_call(
        functools.partial(_slots_kernel, E=E),
        grid=(nb // step,),
        in_specs=[spec, spec, _const_spec((E, 1))],
        out_specs=spec,
        out_shape=jax.ShapeDtypeStruct(idx.shape, jnp.int32),
        compiler_params=_params("parallel"),
        name="slots",
    )(idx, rank, rowstart.astype(F32).reshape(E, 1))


def _dispatch_kernel(rowstart_ref, cnt_ref, nact_ref, dest_ref, h_ref, xs_hbm,
                     stage, zeros, sem, zsem, *, TM, BLK, NB, E):
    t = pl.program_id(0)
    nt = pl.num_programs(0)
    slot = t % 2
    R = SUBLANES

    def zero_copy(first, n, start):
        cp = pltpu.make_async_copy(zeros.at[pl.ds(0, n * R)],
                                   xs_hbm.at[pl.ds(pl.multiple_of(first * R, R), n * R)], zsem)
        cp.start() if start else cp.wait()

    def wait_slot(s):
        for _ in range(TOP_K):
            pltpu.make_async_copy(stage.at[s], xs_hbm.at[pl.ds(0, TM * R)], sem.at[s]).wait()

    def zero_fill(start):
        def tail(e, c):
            first = rowstart_ref[e] + cnt_ref[e]
            n_pad = (cnt_ref[e] + BLK - 1) // BLK * BLK - cnt_ref[e]
            done = jnp.int32(0)
            size = BLK // 2
            while size >= 1:
                piece = n_pad & size

                @pl.when(piece != 0)
                def _(size=size, done=done):
                    zero_copy(first + done, size, start)

                done = done + piece
                size //= 2
            return c

        def idle(b, c):
            zero_copy(b * BLK, BLK, start)
            return c

        lax.fori_loop(0, E, tail, 0)
        lax.fori_loop(nact_ref[0], NB, idle, 0)

    @pl.when(t == 0)
    def _():
        zeros[...] = jnp.zeros(zeros.shape, F32)
        zero_fill(True)

    @pl.when(t >= 2)
    def _():
        wait_slot(slot)

    stage[slot] = h_ref[...]
    for c in range(TM // LANES):
        def row_body(r, carry, c=c):
            src = pl.multiple_of((c * LANES + r) * R, R)
            for k in range(TOP_K):
                dst = pl.multiple_of(dest_ref[c, k, r] * R, R)
                pltpu.make_async_copy(stage.at[slot, pl.ds(src, R)], xs_hbm.at[pl.ds(dst, R)],
                                      sem.at[slot]).start(priority=k % 2)
            return carry

        lax.fori_loop(0, LANES, row_body, 0)

    @pl.when(t == nt - 1)
    def _():
        @pl.when(t >= 1)
        def _():
            wait_slot(1 - slot)

        wait_slot(slot)
        zero_fill(False)


def _dispatch(h_tiles, dest, rowstart, counts, nact, NB):
    N = h_tiles.shape[0] // SUBLANES
    E = rowstart.shape[0]
    TM = MERGE_TILE
    BLK = EXPERT_BLOCK
    R = SUBLANES
    kern = functools.partial(_dispatch_kernel, TM=TM, BLK=BLK, NB=NB, E=E)
    grid_spec = pltpu.PrefetchScalarGridSpec(
        num_scalar_prefetch=3,
        grid=(N // TM,),
        in_specs=[pl.BlockSpec((TM // LANES, TOP_K, LANES), lambda t, *_: (t, 0, 0),
                               memory_space=pltpu.SMEM),
                  pl.BlockSpec((TM * R, LANES), lambda t, *_: (t, 0))],
        out_specs=pl.BlockSpec(memory_space=pl.ANY),
        scratch_shapes=[pltpu.VMEM((2, TM * R, LANES), F32), pltpu.VMEM((BLK * R, LANES), F32),
                        pltpu.SemaphoreType.DMA((2,)), pltpu.SemaphoreType.DMA(())],
    )
    return pl.pallas_call(
        kern,
        grid_spec=grid_spec,
        out_shape=jax.ShapeDtypeStruct((NB * BLK * R, LANES), F32),
        compiler_params=_params("arbitrary"),
        name="dispatch",
    )(rowstart, counts, nact, dest, h_tiles)


def _expert_kernel(first_ref, nblk_ref, nact_ref, wg_ref, wu_ref, wd_ref, x_hbm, y_hbm,
                   xbuf, ybuf, wg_s, wu_s, wd_s, xsem, ysem, *, BLK, NB):
    e = pl.program_id(0)
    nact = nact_ref[0]
    rows = BLK * SUBLANES

    def block(ref, g):
        return ref.at[pl.ds(pl.multiple_of(g * rows, rows), rows)]

    def x_copy(g, slot):
        return pltpu.make_async_copy(block(x_hbm, g), xbuf.at[slot], xsem.at[slot])

    def y_copy(g, slot):
        return pltpu.make_async_copy(ybuf.at[slot], block(y_hbm, g), ysem.at[slot])

    ahead = SLOTS - 1

    @pl.when(e == 0)
    def _():
        for g0 in range(ahead):
            @pl.when(g0 < nact)
            def _(g0=g0):
                x_copy(g0, g0).start()

    @pl.when(nblk_ref[e] > 0)
    def _():
        wg_s[...] = wg_ref[...].astype(BF16)
        wu_s[...] = wu_ref[...].astype(BF16)
        wd_s[...] = wd_ref[...].astype(BF16)

    def body(j, carry):
        g = first_ref[e] + j
        slot = g % SLOTS
        x_copy(g, slot).wait()

        @pl.when(g + ahead < nact)
        def _():
            x_copy(g + ahead, (g + ahead) % SLOTS).start()

        @pl.when(g >= SLOTS)
        def _():
            y_copy(g - SLOTS, slot).wait()

        x = _from_row_tiles(xbuf.at[slot], 0, BLK).astype(BF16)
        g_act = _dot(x, wg_s[...])
        hid = g_act * _sigmoid(g_act) * _dot(x, wu_s[...])
        _to_row_tiles(ybuf.at[slot], _dot(hid.astype(BF16), wd_s[...]))
        y_copy(g, slot).start()
        return carry

    lax.fori_loop(0, nblk_ref[e], body, 0)

    @pl.when(e == pl.num_programs(0) - 1)
    def _():
        for back in range(SLOTS, 0, -1):
            @pl.when(nact >= back)
            def _(back=back):
                y_copy(nact - back, (nact - back) % SLOTS).wait()

        ybuf[0] = jnp.zeros(ybuf.shape[1:], F32)

        def idle(g, carry, *, start):
            cp = y_copy(g, 0)
            cp.start() if start else cp.wait()
            return carry

        for start in (True, False):
            lax.fori_loop(nact, NB, functools.partial(idle, start=start), 0)


def _experts(x_sorted, first_blk, nblk_e, nact, NB, w_eg, w_eu, w_ed):
    E, D, He = w_eg.shape
    BLK = EXPERT_BLOCK
    R = SUBLANES
    per_expert = lambda shape: pl.BlockSpec((None,) + shape, lambda e, *_: (e, 0, 0))
    grid_spec = pltpu.PrefetchScalarGridSpec(
        num_scalar_prefetch=3,
        grid=(E,),
        in_specs=[per_expert((D, He)), per_expert((D, He)), per_expert((He, D)),
                  pl.BlockSpec(memory_space=pl.ANY)],
        out_specs=pl.BlockSpec(memory_space=pl.ANY),
        scratch_shapes=[pltpu.VMEM((SLOTS, BLK * R, LANES), F32),
                        pltpu.VMEM((SLOTS, BLK * R, LANES), F32),
                        pltpu.VMEM((D, He), BF16), pltpu.VMEM((D, He), BF16),
                        pltpu.VMEM((He, D), BF16),
                        pltpu.SemaphoreType.DMA((SLOTS,)), pltpu.SemaphoreType.DMA((SLOTS,))],
    )
    return pl.pallas_call(
        functools.partial(_expert_kernel, BLK=BLK, NB=NB),
        grid_spec=grid_spec,
        out_shape=jax.ShapeDtypeStruct((NB * BLK * R, LANES), F32),
        compiler_params=_params("arbitrary"),
        name="routed_experts",
    )(first_blk, nblk_e, nact, w_eg, w_eu, w_ed, x_sorted)


def _combine_kernel(dest_ref, wt_ref, pre_ref, lng_ref, lnb_ref, y_hbm, o_ref, buf, sem, *, TM):
    t = pl.program_id(0)
    nt = pl.num_programs(0) - 1
    slot = t % 2
    R = SUBLANES

    @pl.when(t < nt)
    def _():
        for k in range(TOP_K):
            for r in range(TM):
                src = pl.multiple_of(dest_ref[0, k, r] * R, R)
                pltpu.make_async_copy(y_hbm.at[pl.ds(src, R)],
                                      buf.at[slot, pl.ds((k * TM + r) * R, R)],
                                      sem.at[slot]).start(priority=r % 2)

    @pl.when(t >= 1)
    def _():
        prev = 1 - slot
        for k in range(TOP_K):
            pltpu.make_async_copy(y_hbm.at[pl.ds(0, TM * R)],
                                  buf.at[prev, pl.ds(k * TM * R, TM * R)], sem.at[prev]).wait()
        w = jnp.concatenate([wt_ref[0], jnp.zeros((TM - TOP_K, TM), F32)], axis=0).T
        z = pre_ref[...]
        rows = buf.at[prev]
        for k in range(TOP_K):
            z = z + w[:, k:k + 1] * _from_row_tiles(rows, k * TM * R, TM)
        o_ref[...] = _layer_norm(z, lng_ref[...], lnb_ref[...])


def _combine(y_sorted, dest, wts, pre, ln_g, ln_b):
    N, D = pre.shape
    TM = LANES
    nt = N // TM
    R = SUBLANES
    kern = functools.partial(_combine_kernel, TM=TM)
    row = lambda a: a.reshape(1, -1).astype(F32)
    ahead = lambda t: (jnp.minimum(t, nt - 1), 0, 0)
    behind3 = lambda t: (jnp.maximum(t - 1, 0), 0, 0)
    behind = lambda t: (jnp.maximum(t - 1, 0), 0)
    return pl.pallas_call(
        kern,
        grid=(nt + 1,),
        in_specs=[pl.BlockSpec((1, TOP_K, TM), ahead, memory_space=pltpu.SMEM),
                  pl.BlockSpec((1, TOP_K, TM), behind3),
                  pl.BlockSpec((TM, D), behind),
                  _const_spec((1, D)), _const_spec((1, D)),
                  pl.BlockSpec(memory_space=pl.ANY)],
        out_specs=pl.BlockSpec((TM, D), behind),
        out_shape=jax.ShapeDtypeStruct((N, D), F32),
        scratch_shapes=[pltpu.VMEM((2, TOP_K * TM * R, LANES), F32),
                        pltpu.SemaphoreType.DMA((2,))],
        compiler_params=_params("arbitrary"),
        name="combine",
    )(dest, wts, pre, row(ln_g), row(ln_b), y_sorted)


def _dispatch_meta(cnt, N, E):
    BLK = EXPERT_BLOCK
    NB = (N * TOP_K + E * (BLK - 1)) // BLK
    counts = cnt[:, 0].astype(jnp.int32)
    nblk_e = (counts + BLK - 1) // BLK
    bend = jnp.cumsum(nblk_e)
    first_blk = (bend - nblk_e).astype(jnp.int32)
    rowstart = first_blk * BLK
    nact = bend[-1:].astype(jnp.int32)
    return counts, rowstart, first_blk, nblk_e.astype(jnp.int32), nact, NB


def kernel(x, rel_bias_table, w_in, b_gate, w_pool_grp, pool_scale, w_branch_pool, lambda_q1,
           lambda_k1, lambda_q2, lambda_k2, subln_gain, w_branch_attn, w_out, ln1_g, ln1_b,
           w_router, router_bias, w_exp_gate, w_exp_up, w_exp_down, w_sh_gate, w_sh_up,
           w_sh_down, ln2_g, ln2_b):
    B, S, D = x.shape
    L = w_in.shape[0]
    E = w_router.shape[2]
    P = pool_scale.shape[1]
    A = w_branch_attn.shape[1]
    N = B * S
    alpha = (2 * L) ** 0.25
    assert S % TOKEN_TILE == 0 and S % MERGE_TILE == 0
    assert A == N_DIFF_HEADS * 2 * DIFF_HEAD_DIM and TOKEN_TILE >= REL_MAX_DIST
    assert D == SUBLANES * LANES, "a row tile holds exactly one (SUBLANES, LANES) tile per row"
    band, far_c = _bias_tables(rel_bias_table, S, TOKEN_TILE)

    h = x
    for i in range(L):
        w_uqkv = w_in[i][:, :P + 3 * A].astype(BF16)
        w_gate = w_in[i][:, P + 3 * A:].astype(BF16)
        u, qk, vt = _inproj(h, w_uqkv, P, A)
        o = _attention(qk, vt, band, far_c, lambda_q1[i], lambda_k1[i], lambda_q2[i],
                       lambda_k2[i], subln_gain[i], _lambda_init(i))
        h1, h1_tiles = _merge(h, u, o, w_gate, b_gate[i], w_pool_grp[i].astype(BF16), pool_scale[i],
                    w_branch_pool[i].astype(BF16), w_branch_attn[i].astype(BF16),
                    w_out[i].astype(BF16), ln1_g[i], ln1_b[i], alpha)
        t = h1.reshape(N, D)
        wr_hi, wr_lo = _split_bf16(w_router[i].astype(F32).T)
        idx, wts, rank, cnt, pre = _route(t, wr_hi, wr_lo, router_bias[i],
                                          w_sh_gate[i].astype(BF16), w_sh_up[i].astype(BF16),
                                          w_sh_down[i].astype(BF16), alpha)
        counts, rowstart, first_blk, nblk_e, nact, NB = _dispatch_meta(cnt, N, E)
        dest = _slots(idx, rank, rowstart)
        x_sorted = _dispatch(h1_tiles.reshape(N * SUBLANES, LANES), dest, rowstart, counts, nact, NB)
        y_sorted = _experts(x_sorted, first_blk, nblk_e, nact, NB, w_exp_gate[i], w_exp_up[i],
                            w_exp_down[i])
        h = _combine(y_sorted, dest, wts, pre, ln2_g[i], ln2_b[i]).reshape(B, S, D)
    return h
```
